```python
import jax, jax.numpy as jnp
from jax import lax
import numpy as np

D_MODEL = 2048
BATCH = 16
SEQ = 256
DEPTH = 4
DEC_BATCH = 4
DEC_SEQ = 4096
PAST_LEN = 256

GRID_W = 64
N_HEADS = 8
HEAD_DIM = 128
ATTN_DIM = N_HEADS * HEAD_DIM
NA_KH = 8
NA_KW = 16
F_GROUPS = 4
F_GROUP_DIM = 256
FOURIER_DIM = F_GROUPS * F_GROUP_DIM
FFN_DIM = 5632
N_MOD = 6
RMS_EPS = 1e-6
IN_DIM = 3 * ATTN_DIM + FOURIER_DIM + 2 * D_MODEL
IN_SPLITS = (ATTN_DIM, 2 * ATTN_DIM, 3 * ATTN_DIM, 3 * ATTN_DIM + FOURIER_DIM,
             3 * ATTN_DIM + FOURIER_DIM + D_MODEL)
ATTN_SCALE = 1.0 / math.sqrt(HEAD_DIM) if False else HEAD_DIM ** -0.5

kernel_name = 'hybrid_natten_fnet_prefix_diffusion_step'


def rms_norm(x, g):
    x32 = x.astype(jnp.float32)
    y = x32 * lax.rsqrt(jnp.mean(x32 * x32, axis=-1, keepdims=True) + RMS_EPS)
    return (y * g.astype(jnp.float32)).astype(x.dtype)


def adaln_params(cvec, w_mod_l, b_mod_l):
    m = jax.nn.silu(cvec) @ w_mod_l + b_mod_l
    m = m.reshape(cvec.shape[:-1] + (N_MOD, D_MODEL))
    return tuple(m[..., i, None, :] for i in range(N_MOD))


def context_attention(q, k, v):
    s = jnp.einsum('bhqd,bhkd->bhqk', q, k).astype(jnp.float32) * ATTN_SCALE
    p = jax.nn.softmax(s, axis=-1).astype(v.dtype)
    return jnp.einsum('bhqk,bhkd->bhqd', p, v)


def neighbourhood_attention(q, k, v, k_ctx, v_ctx, rpb_l):
    B, H, N, dh = q.shape
    rows = N // GRID_W
    kh = min(NA_KH, rows)
    kw = min(NA_KW, GRID_W)
    qg = q.reshape(B, H, rows, GRID_W, dh)
    kg = k.reshape(B, H, rows, GRID_W, dh)
    vg = v.reshape(B, H, rows, GRID_W, dh)
    r = jnp.arange(rows)
    r0 = jnp.clip(r - kh // 2, 0, rows - kh)
    key_rows = r0[:, None] + jnp.arange(kh)[None, :]
    k_nb = jnp.take(kg, key_rows, axis=2)
    v_nb = jnp.take(vg, key_rows, axis=2)
    s_nb = jnp.einsum('bhrqd,bhrikd->bhrqik', qg, k_nb).astype(jnp.float32) * ATTN_SCALE
    col = jnp.arange(GRID_W)
    c0 = jnp.clip(col - kw // 2, 0, GRID_W - kw)
    in_win = (col[None, :] >= c0[:, None]) & (col[None, :] < c0[:, None] + kw)
    dr = key_rows - r[:, None]
    dc = jnp.clip(col[None, :] - col[:, None], -(NA_KW - 1), NA_KW - 1)
    bias = rpb_l[:, dr[:, None, :, None] + (NA_KH - 1), dc[None, :, None, :] + (NA_KW - 1)]
    s_nb = s_nb + bias[None].astype(jnp.float32)
    s_nb = jnp.where(in_win[:, None, :], s_nb, -jnp.inf)
    s_nb = s_nb.reshape(B, H, rows, GRID_W, kh * GRID_W)
    s_ctx = jnp.einsum('bhrqd,bhtd->bhrqt', qg, k_ctx).astype(jnp.float32) * ATTN_SCALE
    s = jnp.concatenate([s_nb, s_ctx], axis=-1)
    p = jax.nn.softmax(s, axis=-1).astype(v.dtype)
    p_nb = p[..., :kh * GRID_W].reshape(B, H, rows, GRID_W, kh, GRID_W)
    p_ctx = p[..., kh * GRID_W:]
    o = (jnp.einsum('bhrqik,bhrikd->bhrqd', p_nb, v_nb)
         + jnp.einsum('bhrqt,bhtd->bhrqd', p_ctx, v_ctx))
    return o.reshape(B, H, N, dh)


def fourier_mix(f):
    B, N, _ = f.shape
    f32 = f.astype(jnp.float32).reshape(B, N, F_GROUPS, F_GROUP_DIM)
    y = jnp.fft.fft2(f32, axes=(1, 3), norm='ortho').real
    return y.reshape(B, N, FOURIER_DIM).astype(f.dtype)


def dwconv_seq(u, w, b):
    up = jnp.pad(u, ((0, 0), (1, 1), (0, 0)))
    return up[:, :-2] * w[0] + up[:, 1:-1] * w[1] + up[:, 2:] * w[2] + b


def conv_ffn(h, w_up_l, conv_w_l, conv_b_l, w_down_l):
    u = dwconv_seq(h @ w_up_l, conv_w_l, conv_b_l)
    a, g = jnp.split(u, 2, axis=-1)
    return (jax.nn.silu(g) * a) @ w_down_l


def trunk_layer(x, mods, g1, w_in_l, w_pa_l, w_pb_l, w_out_l, g2, w_up_l, conv_w_l, conv_b_l,
                w_down_l, attend):
    shift1, scale1, gate1, shift2, scale2, gate2 = mods
    B, N, _ = x.shape
    h = rms_norm(x, g1) * (1 + scale1) + shift1
    q, k, v, f, ga, gb = jnp.split(h @ w_in_l, IN_SPLITS, axis=-1)
    q = q.reshape(B, N, N_HEADS, HEAD_DIM).transpose(0, 2, 1, 3)
    k = k.reshape(B, N, N_HEADS, HEAD_DIM).transpose(0, 2, 1, 3)
    v = v.reshape(B, N, N_HEADS, HEAD_DIM).transpose(0, 2, 1, 3)
    a = attend(q, k, v).transpose(0, 2, 1, 3).reshape(B, N, ATTN_DIM)
    fo = fourier_mix(f)
    y = jax.nn.sigmoid(ga) * (a @ w_pa_l) + jax.nn.sigmoid(gb) * (fo @ w_pb_l)
    x = x + gate1 * (y @ w_out_l)
    h = rms_norm(x, g2) * (1 + scale2) + shift2
    x = x + gate2 * conv_ffn(h, w_up_l, conv_w_l, conv_b_l, w_down_l)
    return x, k, v


def setup_inputs(seed: int = 0) -> dict:
    key = jax.random.key(seed)
    ks = jax.random.split(key, 20)
    nrm = jax.random.normal
    f32 = jnp.float32
    return {
        'x_prompt': nrm(ks[0], (BATCH, SEQ, D_MODEL), f32),
        'x_sample': nrm(ks[1], (DEC_BATCH, DEC_SEQ, D_MODEL), f32),
        'cache_k': nrm(ks[2], (DEC_BATCH, DEPTH, N_HEADS, PAST_LEN, HEAD_DIM), f32),
        'cache_v': nrm(ks[3], (DEC_BATCH, DEPTH, N_HEADS, PAST_LEN, HEAD_DIM), f32),
        'c': nrm(ks[4], (DEC_BATCH, D_MODEL), f32),
        'c_ctx': nrm(ks[5], (D_MODEL,), f32),
        'w_mod': nrm(ks[6], (DEPTH, D_MODEL, N_MOD * D_MODEL), f32) * (0.5 * D_MODEL ** -0.5),
        'b_mod': nrm(ks[7], (DEPTH, N_MOD * D_MODEL), f32) * 0.01,
        'norm1_g': 1.0 + 0.05 * nrm(ks[8], (DEPTH, D_MODEL), f32),
        'w_in': nrm(ks[9], (DEPTH, D_MODEL, IN_DIM), f32) * D_MODEL ** -0.5,
        'rpb': nrm(ks[10], (DEPTH, N_HEADS, 2 * NA_KH - 1, 2 * NA_KW - 1), f32) * 0.1,
        'w_pa': nrm(ks[11], (DEPTH, ATTN_DIM, D_MODEL), f32) * ATTN_DIM ** -0.5,
        'w_pb': nrm(ks[12], (DEPTH, FOURIER_DIM, D_MODEL), f32) * FOURIER_DIM ** -0.5,
        'w_out': nrm(ks[13], (DEPTH, D_MODEL, D_MODEL), f32) * D_MODEL ** -0.5,
        'norm2_g': 1.0 + 0.05 * nrm(ks[14], (DEPTH, D_MODEL), f32),
        'w_up': nrm(ks[15], (DEPTH, D_MODEL, 2 * FFN_DIM), f32) * D_MODEL ** -0.5,
        'conv_w': nrm(ks[16], (DEPTH, 3, 2 * FFN_DIM), f32) * 3 ** -0.5,
        'conv_b': nrm(ks[17], (DEPTH, 2 * FFN_DIM), f32) * 0.01,
        'w_down': nrm(ks[18], (DEPTH, FFN_DIM, D_MODEL), f32) * FFN_DIM ** -0.5,
        'norm_f_g': 1.0 + 0.05 * nrm(ks[19], (D_MODEL,), f32),
    }


def reference(x_prompt, x_sample, cache_k, cache_v, c, c_ctx, w_mod, b_mod, norm1_g, w_in, rpb,
              w_pa, w_pb, w_out, norm2_g, w_up, conv_w, conv_b, w_down, norm_f_g):
    xp = x_prompt
    xs = x_sample
    new_k = []
    new_v = []
    for l in range(DEPTH):
        lw = (norm1_g[l], w_in[l], w_pa[l], w_pb[l], w_out[l], norm2_g[l], w_up[l], conv_w[l],
              conv_b[l], w_down[l])
        mods_ctx = adaln_params(c_ctx, w_mod[l], b_mod[l])
        mods_lat = adaln_params(c, w_mod[l], b_mod[l])
        xp, k_l, v_l = trunk_layer(xp, mods_ctx, *lw, attend=context_attention)
        new_k.append(k_l)
        new_v.append(v_l)
        k_ctx = cache_k[:, l]
        v_ctx = cache_v[:, l]
        rpb_l = rpb[l]
        xs, _, _ = trunk_layer(
            xs, mods_lat, *lw,
            attend=lambda q, k, v: neighbourhood_attention(q, k, v, k_ctx, v_ctx, rpb_l))
    y_prompt = rms_norm(xp, norm_f_g)
    y_sample = rms_norm(xs, norm_f_g)
    new_cache_k = jnp.stack(new_k, axis=1)
    new_cache_v = jnp.stack(new_v, axis=1)
    return (y_prompt, y_sample, new_cache_k, new_cache_v)
```

```python
import functools
import math

import numpy as np
import jax
import jax.numpy as jnp
from jax import lax
from jax.experimental import pallas as pl
from jax.experimental.pallas import tpu as pltpu

F32 = jnp.float32
BF16 = jnp.bfloat16

RMS_EPS = 1e-6
N_MOD = 6
N_HEADS = 8
HEAD_DIM = 128
GRID_W = 64
NA_KH = 8
NA_KW = 16
F_GROUPS = 4
NBR_ROWS_PER_GROUP = 4
MOD_ROWS = 8
V7X_LANES = 128
V7X_BF16_SUBLANES = 16
V7X_VMEM_BYTES = 64 * 1024 * 1024
VMEM_LIMIT = V7X_VMEM_BYTES - 8 * 1024 * 1024

TM_IN, TN_IN = 1024, 512
TN_MOD = 1024
TM_MERGE, TC_MERGE = 512, 512
TM_FFN, TF_FFN = 512, 512
TM_FOURIER = 256
TM_NORM = 512


def _cparams(n_axes):
    return pltpu.CompilerParams(dimension_semantics=("arbitrary",) * n_axes,
                                vmem_limit_bytes=VMEM_LIMIT)


def _norm_mod(x, g, scale, shift):
    r = lax.rsqrt(jnp.mean(x * x, axis=-1, keepdims=True) + RMS_EPS)
    return ((x * r) * g) * (1.0 + scale) + shift


def _mod_kernel(c_ref, w_ref, b_ref, o_ref):
    cv = c_ref[...]
    s = (cv * jax.nn.sigmoid(cv)).astype(BF16)
    o_ref[...] = jnp.dot(s, w_ref[...].astype(BF16), preferred_element_type=F32) + b_ref[...]


def _modulation(c_all, w_mod, b_mod):
    n_layers, d, nd = w_mod.shape
    tn = min(TN_MOD, nd)
    assert nd % tn == 0
    return pl.pallas_call(
        _mod_kernel,
        grid=(n_layers, nd // tn),
        in_specs=[pl.BlockSpec((MOD_ROWS, d), lambda l, j: (0, 0)),
                  pl.BlockSpec((None, d, tn), lambda l, j: (l, 0, j)),
                  pl.BlockSpec((None, 1, tn), lambda l, j: (l, 0, j))],
        out_specs=pl.BlockSpec((None, MOD_ROWS, tn), lambda l, j: (l, 0, j)),
        out_shape=jax.ShapeDtypeStruct((n_layers, MOD_ROWS, nd), F32),
        compiler_params=_cparams(2),
        name="modulation",
    )(c_all, w_mod, b_mod.reshape(n_layers, 1, nd))


def _in_proj_kernel(x_ref, mod_ref, g_ref, w_ref, o_ref, *rest, kv_tiles):
    h_ref = rest[-1]
    j = pl.program_id(1)

    @pl.when(j == 0)
    def _():
        h = _norm_mod(x_ref[...], g_ref[...], mod_ref[1:2, :], mod_ref[0:1, :])
        h_ref[...] = h.astype(BF16)

    res = jnp.dot(h_ref[...], w_ref[...], preferred_element_type=F32)
    o_ref[...] = res.astype(o_ref.dtype)

    if kv_tiles is not None:
        kc_ref, vc_ref = rest[0], rest[1]
        k0, v0, v1 = kv_tiles
        nb, nh, s, dh = kc_ref.shape

        def scatter(dst_ref):
            for bb in range(nb):
                for hh in range(nh):
                    dst_ref[bb, hh] = res[bb * s:(bb + 1) * s, hh * dh:(hh + 1) * dh]

        @pl.when((j >= k0) & (j < v0))
        def _():
            scatter(kc_ref)

        @pl.when((j >= v0) & (j < v1))
        def _():
            scatter(vc_ref)


def _in_proj(x, mods_l, g, w, *, rows_per_mod, mod_row0, cache_seq=None):
    m, d = x.shape
    n_out = w.shape[1]
    tm, tn = min(TM_IN, m), min(TN_IN, n_out)
    assert m % tm == 0 and n_out % tn == 0 and (rows_per_mod % tm == 0 or tm % rows_per_mod == 0)
    attn = N_HEADS * HEAD_DIM
    in_specs = [pl.BlockSpec((tm, d), lambda i, j: (i, 0)),
                pl.BlockSpec((None, N_MOD, d), lambda i, j: (mod_row0 + (i * tm) // rows_per_mod, 0, 0)),
                pl.BlockSpec((1, d), lambda i, j: (0, 0)),
                pl.BlockSpec((d, tn), lambda i, j: (0, j))]
    out_specs = [pl.BlockSpec((tm, tn), lambda i, j: (i, j))]
    out_shape = [jax.ShapeDtypeStruct((m, n_out), BF16)]
    kv_tiles = None
    if cache_seq is not None:
        assert tm % cache_seq == 0 and tn % HEAD_DIM == 0 and attn % tn == 0
        k0, v0, v1 = attn // tn, 2 * attn // tn, 3 * attn // tn
        kv_tiles = (k0, v0, v1)
        blk = (tm // cache_seq, tn // HEAD_DIM, cache_seq, HEAD_DIM)
        out_specs += [pl.BlockSpec(blk, lambda i, j: (i, jnp.clip(j - k0, 0, v0 - k0 - 1), 0, 0)),
                      pl.BlockSpec(blk, lambda i, j: (i, jnp.clip(j - v0, 0, v1 - v0 - 1), 0, 0))]
        out_shape += [jax.ShapeDtypeStruct((m // cache_seq, N_HEADS, cache_seq, HEAD_DIM), F32)] * 2
    return pl.pallas_call(
        functools.partial(_in_proj_kernel, kv_tiles=kv_tiles),
        grid=(m // tm, n_out // tn),
        in_specs=in_specs, out_specs=out_specs, out_shape=out_shape,
        scratch_shapes=[pltpu.VMEM((tm, d), BF16)],
        compiler_params=_cparams(2),
        name="in_proj_ctx" if cache_seq is not None else "in_proj_lat",
    )(x, mods_l, g.reshape(1, d), w)


def _ctx_attn_kernel(q_ref, k_ref, v_ref, o_ref, *, scale):
    for h in range(N_HEADS):
        sl = slice(h * HEAD_DIM, (h + 1) * HEAD_DIM)
        s = lax.dot_general(q_ref[:, sl], k_ref[:, sl], (((1,), (1,)), ((), ())),
                            preferred_element_type=F32) * scale
        e = jnp.exp(s - jnp.max(s, axis=-1, keepdims=True))
        p = e * (1.0 / jnp.sum(e, axis=-1, keepdims=True))
        o_ref[:, sl] = jnp.dot(p.astype(BF16), v_ref[:, sl], preferred_element_type=F32).astype(o_ref.dtype)


def _ctx_attention(proj, n_seq, seq):
    attn = N_HEADS * HEAD_DIM
    return pl.pallas_call(
        functools.partial(_ctx_attn_kernel, scale=HEAD_DIM ** -0.5),
        grid=(n_seq,),
        in_specs=[pl.BlockSpec((seq, attn), lambda b: (b, 0)),
                  pl.BlockSpec((seq, attn), lambda b: (b, 1)),
                  pl.BlockSpec((seq, attn), lambda b: (b, 2))],
        out_specs=pl.BlockSpec((seq, attn), lambda b: (b, 0)),
        out_shape=jax.ShapeDtypeStruct((n_seq * seq, attn), BF16),
        compiler_params=_cparams(1),
        name="ctx_attention",
    )(proj, proj, proj)


def _nbr_layout(rows):
    r_grp, win = NBR_ROWS_PER_GROUP, NA_KH + NBR_ROWS_PER_GROUP
    types, type_of_group = [], []
    for g in range(rows // r_grp):
        w0 = min(max(r_grp * g - NA_KH // 2, 0), rows - win)
        desc = []
        for qi in range(r_grp):
            r = r_grp * g + qi
            r0 = min(max(r - NA_KH // 2, 0), rows - NA_KH)
            desc.append(tuple((w0 + kj - r) if r0 <= w0 + kj < r0 + NA_KH else None for kj in range(win)))
        desc = tuple(desc)
        if desc not in types:
            types.append(desc)
        type_of_group.append(types.index(desc))
    return types, type_of_group


def _nbr_attn_kernel(rpb_ref, q_ref, k_ref, v_ref, kc_ref, vc_ref, o_ref, bias_ref, pair_ref, *,
                     rows, types, type_of_group, scale):
    w = GRID_W
    r_grp, win = NBR_ROWS_PER_GROUP, NA_KH + NBR_ROWS_PER_GROUP
    gq, wk = r_grp * w, win * w
    n_dr, n_dc = 2 * NA_KH - 1, 2 * NA_KW - 1
    neg_inf = float("-inf")
    h = pl.program_id(0)

    pairs = sorted({(d[qi][2 * m], d[qi][2 * m + 1]) for d in types for qi in range(r_grp)
                    for m in range(win // 2)} - {(None, None)},
                   key=lambda p: tuple(-99 if v is None else v for v in p))

    @pl.when(pl.program_id(1) == 0)
    def _build_bias_tables():
        cq = lax.broadcasted_iota(jnp.int32, (w, V7X_LANES), 0)
        lane = lax.broadcasted_iota(jnp.int32, (w, V7X_LANES), 1)
        ck = lane & (w - 1)
        c0 = jnp.clip(cq - NA_KW // 2, 0, w - NA_KW)
        in_win = (ck >= c0) & (ck < c0 + NA_KW)
        dc_idx = jnp.where(in_win, jnp.clip(ck - cq, -(NA_KW - 1), NA_KW - 1) + (NA_KW - 1), -1)
        left = lane < w
        for p in range(len(pairs)):
            pair_ref[p] = jnp.full((w, V7X_LANES), neg_inf, F32)

        def per_dc(dc, carry):
            hit = dc_idx == dc
            for p, (da, db) in enumerate(pairs):
                va = neg_inf if da is None else rpb_ref[(h * n_dr + da + NA_KH - 1) * n_dc + dc]
                vb = neg_inf if db is None else rpb_ref[(h * n_dr + db + NA_KH - 1) * n_dc + dc]
                pair_ref[p] = jnp.where(hit, jnp.where(left, va, vb), pair_ref[p])
            return carry

        lax.fori_loop(0, n_dc, per_dc, 0)
        for t, desc in enumerate(types):
            for qi in range(r_grp):
                for m in range(win // 2):
                    pr = (desc[qi][2 * m], desc[qi][2 * m + 1])
                    tile = (jnp.full((w, V7X_LANES), neg_inf, F32) if pr == (None, None)
                            else pair_ref[pairs.index(pr)])
                    bias_ref[t, qi * w:(qi + 1) * w, m * V7X_LANES:(m + 1) * V7X_LANES] = tile

    kc = kc_ref[...].astype(BF16)
    vc = vc_ref[...].astype(BF16)
    nt = (((1,), (1,)), ((), ()))
    default_type = max(set(type_of_group), key=type_of_group.count)

    def per_group(g, carry):
        w0 = jnp.clip(r_grp * g - NA_KH // 2, 0, rows - win)
        q0 = pl.multiple_of(g * gq, gq)
        k0 = pl.multiple_of(w0 * w, w)
        t = jnp.int32(default_type)
        for gi, ti in enumerate(type_of_group):
            if ti != default_type:
                t = jnp.where(g == gi, ti, t)
        q = q_ref[pl.ds(q0, gq), :]
        s = lax.dot_general(q, k_ref[pl.ds(k0, wk), :], nt, preferred_element_type=F32) * scale + bias_ref[t]
        sc = lax.dot_general(q, kc, nt, preferred_element_type=F32) * scale
        mx = jnp.maximum(jnp.max(s, axis=-1, keepdims=True), jnp.max(sc, axis=-1, keepdims=True))
        e = jnp.exp(s - mx)
        ec = jnp.exp(sc - mx)
        inv = 1.0 / (jnp.sum(e, axis=-1, keepdims=True) + jnp.sum(ec, axis=-1, keepdims=True))
        o = (jnp.dot((e * inv).astype(BF16), v_ref[pl.ds(k0, wk), :], preferred_element_type=F32)
             + jnp.dot((ec * inv).astype(BF16), vc, preferred_element_type=F32))
        o_ref[pl.ds(q0, gq), :] = o.astype(o_ref.dtype)
        return carry

    lax.fori_loop(0, rows // r_grp, per_group, 0)


def _nbr_attention(proj, cache_k, cache_v, rpb_l, layer, n_batch, n_tok):
    assert 2 * GRID_W == V7X_LANES and n_tok % GRID_W == 0
    rows = n_tok // GRID_W
    assert rows % NBR_ROWS_PER_GROUP == 0 and rows >= NA_KH + NBR_ROWS_PER_GROUP and GRID_W >= NA_KW
    types, type_of_group = _nbr_layout(rows)
    r_grp, win = NBR_ROWS_PER_GROUP, NA_KH + NBR_ROWS_PER_GROUP
    n_pairs = len({(d[qi][2 * m], d[qi][2 * m + 1]) for d in types for qi in range(r_grp)
                   for m in range(win // 2)} - {(None, None)})
    t_ctx = cache_k.shape[3]
    blk = lambda col0: pl.BlockSpec((n_tok, HEAD_DIM), lambda h, b: (b, col0 + h))
    ctx = pl.BlockSpec((None, None, None, t_ctx, HEAD_DIM), lambda h, b: (b, layer, h, 0, 0))
    return pl.pallas_call(
        functools.partial(_nbr_attn_kernel, rows=rows, types=types, type_of_group=type_of_group,
                          scale=HEAD_DIM ** -0.5),
        grid=(N_HEADS, n_batch),
        in_specs=[pl.BlockSpec(memory_space=pltpu.SMEM),
                  blk(0), blk(N_HEADS), blk(2 * N_HEADS), ctx, ctx],
        out_specs=pl.BlockSpec((n_tok, HEAD_DIM), lambda h, b: (b, h)),
        out_shape=jax.ShapeDtypeStruct((n_batch * n_tok, N_HEADS * HEAD_DIM), BF16),
        scratch_shapes=[pltpu.VMEM((len(types), r_grp * GRID_W, win * GRID_W), F32),
                        pltpu.VMEM((n_pairs, GRID_W, V7X_LANES), F32)],
        compiler_params=_cparams(2),
        name="nbr_attention",
    )(rpb_l.reshape(-1), proj, proj, proj, cache_k, cache_v)


@functools.lru_cache(maxsize=None)
def _dft_tables(n):
    jk = (np.arange(n, dtype=np.int64)[:, None] * np.arange(n, dtype=np.int64)[None, :]) % n
    ang = 2.0 * np.pi * jk.astype(np.float64) / n
    scale = 1.0 / math.sqrt(n)
    return ((np.cos(ang) * scale).astype(BF16), (np.sin(ang) * scale).astype(BF16))


def _fourier_kernel(f_ref, cs_ref, wc_ref, ws_ref, o_ref, zc_ref, zs_ref, *, row_chunk):
    n, fd = f_ref.shape
    gd = fd // F_GROUPS

    @pl.when(pl.program_id(1) == 0)
    def _channel_stage():
        for r in range(n // row_chunk):
            rs = slice(r * row_chunk, (r + 1) * row_chunk)
            for g in range(F_GROUPS):
                sl = slice(g * gd, (g + 1) * gd)
                z = jnp.dot(f_ref[rs, sl], cs_ref[...], preferred_element_type=F32)
                zc_ref[rs, sl] = z[:, :gd].astype(BF16)
                zs_ref[rs, sl] = z[:, gd:].astype(BF16)

    o_ref[...] = (jnp.dot(wc_ref[...], zc_ref[...], preferred_element_type=F32)
                  + jnp.dot(ws_ref[...], zs_ref[...], preferred_element_type=F32)).astype(o_ref.dtype)


def _fourier(proj, n_seq, seq):
    attn = N_HEADS * HEAD_DIM
    fd = attn
    gd = fd // F_GROUPS
    cc, sc = _dft_tables(gd)
    cp, sp = _dft_tables(seq)
    cs = jnp.asarray(np.concatenate([cc, sc], axis=1))
    tm = min(TM_FOURIER, seq)
    assert seq % tm == 0 and proj.shape[1] // fd > 3
    return pl.pallas_call(
        functools.partial(_fourier_kernel, row_chunk=min(512, seq)),
        grid=(n_seq, seq // tm),
        in_specs=[pl.BlockSpec((seq, fd), lambda b, i: (b, 3)),
                  pl.BlockSpec((gd, 2 * gd), lambda b, i: (0, 0)),
                  pl.BlockSpec((tm, seq), lambda b, i: (i, 0)),
                  pl.BlockSpec((tm, seq), lambda b, i: (i, 0))],
        out_specs=pl.BlockSpec((tm, fd), lambda b, i: (b * (seq // tm) + i, 0)),
        out_shape=jax.ShapeDtypeStruct((n_seq * seq, fd), BF16),
        scratch_shapes=[pltpu.VMEM((seq, fd), BF16), pltpu.VMEM((seq, fd), BF16)],
        compiler_params=_cparams(2),
        name="fourier_%d" % seq,
    )(proj, cs, jnp.asarray(cp), jnp.asarray(-sp))


def _merge_out_kernel(x_ref, a_ref, fo_ref, ga_ref, gb_ref, mod_ref, wpa_ref, wpb_ref, wout_ref,
                      o_ref, y_ref, *, tc):
    d = x_ref.shape[1]
    for c in range(d // tc):
        sl = slice(c * tc, (c + 1) * tc)
        ya = jnp.dot(a_ref[...], wpa_ref[:, sl], preferred_element_type=F32)
        yb = jnp.dot(fo_ref[...], wpb_ref[:, sl], preferred_element_type=F32)
        y = (jax.nn.sigmoid(ga_ref[:, sl].astype(F32)) * ya
             + jax.nn.sigmoid(gb_ref[:, sl].astype(F32)) * yb)
        y_ref[:, sl] = y.astype(BF16)
    for c in range(d // tc):
        sl = slice(c * tc, (c + 1) * tc)
        o = jnp.dot(y_ref[...], wout_ref[:, sl], preferred_element_type=F32)
        o_ref[:, sl] = x_ref[:, sl] + mod_ref[2:3, sl] * o


def _merge_out(x, a, fo, proj, mods_l, w_pa, w_pb, w_out, *, rows_per_mod, mod_row0):
    m, d = x.shape
    attn = a.shape[1]
    tm = min(TM_MERGE, m)
    assert m % tm == 0 and (3 * attn + fo.shape[1]) % d == 0 and rows_per_mod % tm == 0
    gate_blk0 = (3 * attn + fo.shape[1]) // d
    whole = lambda arr: pl.BlockSpec(arr.shape, lambda i: (0, 0), pipeline_mode=pl.Buffered(1))
    return pl.pallas_call(
        functools.partial(_merge_out_kernel, tc=min(TC_MERGE, d)),
        grid=(m // tm,),
        in_specs=[pl.BlockSpec((tm, d), lambda i: (i, 0)),
                  pl.BlockSpec((tm, attn), lambda i: (i, 0)),
                  pl.BlockSpec((tm, fo.shape[1]), lambda i: (i, 0)),
                  pl.BlockSpec((tm, d), lambda i: (i, gate_blk0)),
                  pl.BlockSpec((tm, d), lambda i: (i, gate_blk0 + 1)),
                  pl.BlockSpec((None, N_MOD, d), lambda i: (mod_row0 + (i * tm) // rows_per_mod, 0, 0)),
                  whole(w_pa), whole(w_pb), whole(w_out)],
        out_specs=pl.BlockSpec((tm, d), lambda i: (i, 0)),
        out_shape=jax.ShapeDtypeStruct((m, d), F32),
        scratch_shapes=[pltpu.VMEM((tm, d), BF16)],
        compiler_params=_cparams(1),
        name="merge_out",
    )(x, a, fo, proj, proj, mods_l, w_pa, w_pb, w_out)


def _ffn_kernel(x_ref, xp_ref, xn_ref, mod_ref, g_ref, wa_ref, wg_ref, cwa_ref, cwg_ref, cba_ref, cbg_ref,
                wd_ref, o_ref, h_ref, u_ref, acc_ref, *, seq):
    tm = x_ref.shape[0]
    halo = xp_ref.shape[0]
    i, f = pl.program_id(0), pl.program_id(1)

    @pl.when(f == 0)
    def _():
        g, scale, shift = g_ref[...], mod_ref[4:5, :], mod_ref[3:4, :]
        h_ref[0:halo, :] = _norm_mod(xp_ref[...], g, scale, shift).astype(BF16)
        h_ref[halo:halo + tm, :] = _norm_mod(x_ref[...], g, scale, shift).astype(BF16)
        h_ref[halo + tm:, :] = _norm_mod(xn_ref[...], g, scale, shift).astype(BF16)
        acc_ref[...] = jnp.zeros_like(acc_ref)

    h = h_ref[...]
    u_ref[0] = jnp.dot(h, wa_ref[...], preferred_element_type=F32)
    u_ref[1] = jnp.dot(h, wg_ref[...], preferred_element_type=F32)

    pos = (i * tm + lax.broadcasted_iota(jnp.int32, (tm, 1), 0)) & (seq - 1)
    first, last = pos == 0, pos == seq - 1

    def conv(idx, cw_ref, cb_ref):
        prev = jnp.where(first, 0.0, u_ref[idx, pl.ds(halo - 1, tm), :])
        nxt = jnp.where(last, 0.0, u_ref[idx, pl.ds(halo + 1, tm), :])
        return (prev * cw_ref[0:1, :] + u_ref[idx, pl.ds(halo, tm), :] * cw_ref[1:2, :]
                + nxt * cw_ref[2:3, :] + cb_ref[...])

    ca = conv(0, cwa_ref, cba_ref)
    cg = conv(1, cwg_ref, cbg_ref)
    act = ((cg * jax.nn.sigmoid(cg)) * ca).astype(BF16)
    acc_ref[...] += jnp.dot(act, wd_ref[...], preferred_element_type=F32)

    @pl.when(f == pl.num_programs(1) - 1)
    def _():
        o_ref[...] = x_ref[...] + mod_ref[5:6, :] * acc_ref[...]


def _conv_ffn(x, mods_l, g, w_up, conv_w, conv_b, w_down, *, seq, rows_per_mod, mod_row0):
    m, d = x.shape
    ffn = w_down.shape[0]
    tm, tf = min(TM_FFN, m), min(TF_FFN, ffn)
    halo = V7X_BF16_SUBLANES
    assert seq & (seq - 1) == 0 and (tm % seq == 0 or seq % tm == 0) and ffn % tf == 0 and tm % halo == 0
    assert m % tm == 0 and rows_per_mod % tm == 0
    nf, nhb = ffn // tf, m // halo
    return pl.pallas_call(
        functools.partial(_ffn_kernel, seq=seq),
        grid=(m // tm, nf),
        in_specs=[pl.BlockSpec((tm, d), lambda i, f: (i, 0)),
                  pl.BlockSpec((halo, d), lambda i, f: (jnp.maximum(i * (tm // halo) - 1, 0), 0)),
                  pl.BlockSpec((halo, d), lambda i, f: (jnp.minimum((i + 1) * (tm // halo), nhb - 1), 0)),
                  pl.BlockSpec((None, N_MOD, d), lambda i, f: (mod_row0 + (i * tm) // rows_per_mod, 0, 0)),
                  pl.BlockSpec((1, d), lambda i, f: (0, 0)),
                  pl.BlockSpec((d, tf), lambda i, f: (0, f)),
                  pl.BlockSpec((d, tf), lambda i, f: (0, nf + f)),
                  pl.BlockSpec((3, tf), lambda i, f: (0, f)),
                  pl.BlockSpec((3, tf), lambda i, f: (0, nf + f)),
                  pl.BlockSpec((1, tf), lambda i, f: (0, f)),
                  pl.BlockSpec((1, tf), lambda i, f: (0, nf + f)),
                  pl.BlockSpec((tf, d), lambda i, f: (f, 0))],
        out_specs=pl.BlockSpec((tm, d), lambda i, f: (i, 0)),
        out_shape=jax.ShapeDtypeStruct((m, d), F32),
        scratch_shapes=[pltpu.VMEM((tm + 2 * halo, d), BF16),
                        pltpu.VMEM((2, tm + 2 * halo, tf), F32),
                        pltpu.VMEM((tm, d), F32)],
        compiler_params=_cparams(2),
        name="conv_ffn",
    )(x, x, x, mods_l, g.reshape(1, d), w_up, w_up, conv_w, conv_w,
      conv_b.reshape(1, -1), conv_b.reshape(1, -1), w_down)


def _final_norm_kernel(x_ref, g_ref, o_ref):
    x = x_ref[...]
    r = lax.rsqrt(jnp.mean(x * x, axis=-1, keepdims=True) + RMS_EPS)
    o_ref[...] = (x * r) * g_ref[...]


def _final_norm(x, g):
    m, d = x.shape
    tm = min(TM_NORM, m)
    assert m % tm == 0
    return pl.pallas_call(
        _final_norm_kernel,
        grid=(m // tm,),
        in_specs=[pl.BlockSpec((tm, d), lambda i: (i, 0)), pl.BlockSpec((1, d), lambda i: (0, 0))],
        out_specs=pl.BlockSpec((tm, d), lambda i: (i, 0)),
        out_shape=jax.ShapeDtypeStruct((m, d), F32),
        compiler_params=_cparams(1),
        name="final_norm",
    )(x, g.reshape(1, d))


def kernel(x_prompt, x_sample, cache_k, cache_v, c, c_ctx, w_mod, b_mod, norm1_g, w_in, rpb, w_pa, w_pb,
           w_out, norm2_g, w_up, conv_w, conv_b, w_down, norm_f_g):
    n_ctx, seq, d = x_prompt.shape
    n_lat, n_tok, _ = x_sample.shape
    n_layers = w_mod.shape[0]
    assert n_lat < MOD_ROWS

    c_all = jnp.zeros((MOD_ROWS, d), F32).at[:n_lat].set(c).at[n_lat].set(c_ctx)
    mods = _modulation(c_all, w_mod, b_mod).reshape(n_layers, MOD_ROWS, N_MOD, d)

    xp = x_prompt.reshape(n_ctx * seq, d)
    xs = x_sample.reshape(n_lat * n_tok, d)
    ctx_mod = dict(rows_per_mod=n_ctx * seq, mod_row0=n_lat)
    lat_mod = dict(rows_per_mod=n_tok, mod_row0=0)
    new_k, new_v = [], []
    for l in range(n_layers):
        w_in_l, w_pa_l, w_pb_l, w_out_l = (w[l].astype(BF16) for w in (w_in, w_pa, w_pb, w_out))
        w_up_l, w_down_l = w_up[l].astype(BF16), w_down[l].astype(BF16)
        mods_l = mods[l]

        proj, k_l, v_l = _in_proj(xp, mods_l, norm1_g[l], w_in_l, cache_seq=seq, **ctx_mod)
        new_k.append(k_l)
        new_v.append(v_l)
        a = _ctx_attention(proj, n_ctx, seq)
        fo = _fourier(proj, n_ctx, seq)
        xp = _merge_out(xp, a, fo, proj, mods_l, w_pa_l, w_pb_l, w_out_l, **ctx_mod)
        xp = _conv_ffn(xp, mods_l, norm2_g[l], w_up_l, conv_w[l], conv_b[l], w_down_l, seq=seq, **ctx_mod)

        (proj,) = _in_proj(xs, mods_l, norm1_g[l], w_in_l, **lat_mod)
        a = _nbr_attention(proj, cache_k, cache_v, rpb[l], l, n_lat, n_tok)
        fo = _fourier(proj, n_lat, n_tok)
        xs = _merge_out(xs, a, fo, proj, mods_l, w_pa_l, w_pb_l, w_out_l, **lat_mod)
        xs = _conv_ffn(xs, mods_l, norm2_g[l], w_up_l, conv_w[l], conv_b[l], w_down_l, seq=n_tok, **lat_mod)

    y_prompt = _final_norm(xp, norm_f_g).reshape(n_ctx, seq, d)
    y_sample = _final_norm(xs, norm_f_g).reshape(n_lat, n_tok, d)
    return (y_prompt, y_sample, jnp.stack(new_k, axis=1), jnp.stack(new_v, axis=1))
```

```python
import functools
import math

import numpy as np
import jax
import jax.numpy as jnp
from jax import lax
from jax.experimental import pallas as pl
from jax.experimental.pallas import tpu as pltpu

F32 = jnp.float32
BF16 = jnp.bfloat16

RMS_EPS = 1e-6
LOG2_E = math.log2(math.e)
N_MOD = 6
N_HEADS = 8
HEAD_DIM = 128
GRID_W = 64
NA_KH = 8
NA_KW = 16
F_GROUPS = 4
NBR_ROWS_PER_GROUP = 4
MOD_ROWS = 8
V7X_LANES = 128
V7X_BF16_SUBLANES = 16
V7X_VMEM_BYTES = 64 * 1024 * 1024
VMEM_LIMIT = V7X_VMEM_BYTES - 8 * 1024 * 1024

TM_IN, TN_IN = 1024, 1024
TN_MOD = 1024
TM_MERGE, TC_MERGE = 512, 512
TM_FFN, TF_FFN, TC_FFN = 512, 512, 256
TM_FOURIER = 256
TM_NORM = 512
NORM_ROW_CHUNK = 16
NORM_UNROLL = 4


def _cparams(n_axes):
    return pltpu.CompilerParams(dimension_semantics=("arbitrary",) * n_axes,
                                vmem_limit_bytes=VMEM_LIMIT)


def _layer_arg(layer):
    return jnp.full((1,), layer, jnp.int32)


def _norm_mod_store(dst_ref, dst_row0, src_ref, g, scale, shift, keep=None, src_row0=0, n=None):
    n = src_ref.shape[0] if n is None else n
    chunk = min(NORM_ROW_CHUNK, n)
    assert n % chunk == 0 and src_row0 % chunk == 0 and dst_row0 % chunk == 0
    gs = g * (1.0 + scale)

    def rows(r, carry):
        r0 = pl.multiple_of(r * chunk, chunk)
        x = src_ref[pl.ds(src_row0 + r0, chunk), :]
        inv = lax.rsqrt(jnp.mean(x * x, axis=-1, keepdims=True) + RMS_EPS)
        h = (x * inv) * gs + shift
        if keep is not None:
            h = jnp.where(keep, h, 0.0)
        dst_ref[pl.ds(dst_row0 + r0, chunk), :] = h.astype(BF16)
        return carry

    lax.fori_loop(0, n // chunk, rows, 0, unroll=min(NORM_UNROLL, n // chunk))


def _mod_kernel(c_ref, w_ref, b_ref, o_ref):
    cv = c_ref[...]
    s = (cv * jax.nn.sigmoid(cv)).astype(BF16)
    o_ref[...] = jnp.dot(s, w_ref[...].astype(BF16), preferred_element_type=F32) + b_ref[...]


def _modulation(c_all, w_mod, b_mod):
    n_layers, d, nd = w_mod.shape
    tn = min(TN_MOD, nd)
    assert nd % tn == 0
    return pl.pallas_call(
        _mod_kernel,
        grid=(n_layers, nd // tn),
        in_specs=[pl.BlockSpec((MOD_ROWS, d), lambda l, j: (0, 0)),
                  pl.BlockSpec((None, d, tn), lambda l, j: (l, 0, j)),
                  pl.BlockSpec((None, 1, tn), lambda l, j: (l, 0, j))],
        out_specs=pl.BlockSpec((None, MOD_ROWS, tn), lambda l, j: (l, 0, j)),
        out_shape=jax.ShapeDtypeStruct((n_layers, MOD_ROWS, nd), F32),
        compiler_params=_cparams(2),
        name="modulation",
    )(c_all, w_mod, b_mod.reshape(n_layers, 1, nd))


def _in_proj_kernel(l_ref, x_ref, mod_ref, g_ref, w_ref, *rest, kv_tiles):
    h_ref = rest[-1]
    j = pl.program_id(1)

    @pl.when(j == 0)
    def _():
        _norm_mod_store(h_ref, 0, x_ref, g_ref[...], mod_ref[1:2, :], mod_ref[0:1, :])

    res = jnp.dot(h_ref[...], w_ref[...], preferred_element_type=F32)
    if kv_tiles is None:
        o_ref = rest[0]
        o_ref[...] = res.astype(o_ref.dtype)
    else:
        o_ref, kc_ref, vc_ref = rest[2], rest[3], rest[4]
        o_ref[...] = res.astype(o_ref.dtype)
        k0, v0, v1 = kv_tiles
        nb, nh, s, dh = kc_ref.shape

        def scatter(dst_ref):
            for bb in range(nb):
                for hh in range(nh):
                    dst_ref[bb, hh] = res[bb * s:(bb + 1) * s, hh * dh:(hh + 1) * dh]

        @pl.when((j >= k0) & (j < v0))
        def _():
            scatter(kc_ref)

        @pl.when((j >= v0) & (j < v1))
        def _():
            scatter(vc_ref)


def _in_proj(layer, x, mods, g, w, *, rows_per_mod, mod_row0, caches=None):
    m, d = x.shape
    n_out = w.shape[2]
    tm, tn = min(TM_IN, m), min(TN_IN, n_out)
    assert m % tm == 0 and n_out % tn == 0 and (rows_per_mod % tm == 0 or tm % rows_per_mod == 0)
    attn = N_HEADS * HEAD_DIM
    in_specs = [pl.BlockSpec((tm, d), lambda i, j, l: (i, 0)),
                pl.BlockSpec((None, None, N_MOD, d),
                             lambda i, j, l: (l[0], mod_row0 + (i * tm) // rows_per_mod, 0, 0)),
                pl.BlockSpec((None, 1, d), lambda i, j, l: (l[0], 0, 0)),
                pl.BlockSpec((None, d, tn), lambda i, j, l: (l[0], 0, j))]
    out_specs = [pl.BlockSpec((tm, tn), lambda i, j, l: (i, j))]
    out_shape = [jax.ShapeDtypeStruct((m, n_out), BF16)]
    args = [_layer_arg(layer), x, mods, g.reshape(g.shape[0], 1, d), w]
    kv_tiles, aliases = None, {}
    if caches is not None:
        seq = caches[0].shape[3]
        assert tm % seq == 0 and tn % HEAD_DIM == 0 and attn % tn == 0
        k0, v0, v1 = attn // tn, 2 * attn // tn, 3 * attn // tn
        kv_tiles = (k0, v0, v1)
        blk = (tm // seq, None, tn // HEAD_DIM, seq, HEAD_DIM)
        out_specs += [pl.BlockSpec(blk, lambda i, j, l: (i, l[0], jnp.clip(j - k0, 0, v0 - k0 - 1), 0, 0)),
                      pl.BlockSpec(blk, lambda i, j, l: (i, l[0], jnp.clip(j - v0, 0, v1 - v0 - 1), 0, 0))]
        out_shape += [jax.ShapeDtypeStruct(c.shape, c.dtype) for c in caches]
        in_specs += [pl.BlockSpec(memory_space=pl.ANY)] * 2
        aliases = {len(args): 1, len(args) + 1: 2}
        args += list(caches)
    return pl.pallas_call(
        functools.partial(_in_proj_kernel, kv_tiles=kv_tiles),
        grid_spec=pltpu.PrefetchScalarGridSpec(
            num_scalar_prefetch=1, grid=(m // tm, n_out // tn), in_specs=in_specs, out_specs=out_specs,
            scratch_shapes=[pltpu.VMEM((tm, d), BF16)]),
        out_shape=out_shape,
        input_output_aliases=aliases,
        compiler_params=_cparams(2),
        name="in_proj_ctx" if caches is not None else "in_proj_lat",
    )(*args)


def _ctx_attn_kernel(q_ref, k_ref, v_ref, o_ref, *, scale):
    for h in range(N_HEADS):
        sl = slice(h * HEAD_DIM, (h + 1) * HEAD_DIM)
        s = lax.dot_general(q_ref[:, sl], k_ref[:, sl], (((1,), (1,)), ((), ())),
                            preferred_element_type=F32) * scale
        e = jnp.exp(s - jnp.max(s, axis=-1, keepdims=True))
        p = e * (1.0 / jnp.sum(e, axis=-1, keepdims=True))
        o_ref[:, sl] = jnp.dot(p.astype(BF16), v_ref[:, sl], preferred_element_type=F32).astype(o_ref.dtype)


def _ctx_attention(proj, n_seq, seq):
    attn = N_HEADS * HEAD_DIM
    return pl.pallas_call(
        functools.partial(_ctx_attn_kernel, scale=HEAD_DIM ** -0.5),
        grid=(n_seq,),
        in_specs=[pl.BlockSpec((seq, attn), lambda b: (b, 0)),
                  pl.BlockSpec((seq, attn), lambda b: (b, 1)),
                  pl.BlockSpec((seq, attn), lambda b: (b, 2))],
        out_specs=pl.BlockSpec((seq, attn), lambda b: (b, 0)),
        out_shape=jax.ShapeDtypeStruct((n_seq * seq, attn), BF16),
        compiler_params=_cparams(1),
        name="ctx_attention",
    )(proj, proj, proj)


def _nbr_layout(rows):
    r_grp, win = NBR_ROWS_PER_GROUP, NA_KH + NBR_ROWS_PER_GROUP
    types, type_of_group = [], []
    for g in range(rows // r_grp):
        w0 = min(max(r_grp * g - NA_KH // 2, 0), rows - win)
        desc = []
        for qi in range(r_grp):
            r = r_grp * g + qi
            r0 = min(max(r - NA_KH // 2, 0), rows - NA_KH)
            desc.append(tuple((w0 + kj - r) if r0 <= w0 + kj < r0 + NA_KH else None for kj in range(win)))
        desc = tuple(desc)
        if desc not in types:
            types.append(desc)
        type_of_group.append(types.index(desc))
    pairs = sorted({(d[qi][2 * m], d[qi][2 * m + 1]) for d in types for qi in range(r_grp)
                    for m in range(win // 2)} - {(None, None)},
                   key=lambda p: tuple(-99 if v is None else v for v in p))
    return types, type_of_group, pairs


def _nbr_attn_kernel(l_ref, rpb_ref, q_ref, k_ref, v_ref, kc_ref, vc_ref, o_ref, bias_ref, pair_ref, *,
                     rows, layout, scale):
    types, type_of_group, pairs = layout
    w = GRID_W
    r_grp, win = NBR_ROWS_PER_GROUP, NA_KH + NBR_ROWS_PER_GROUP
    gq, wk = r_grp * w, win * w
    n_dr, n_dc = 2 * NA_KH - 1, 2 * NA_KW - 1
    neg_inf = float("-inf")
    h = pl.program_id(0)
    layer = l_ref[0]

    @pl.when(pl.program_id(1) == 0)
    def _build_bias_tables():
        cq = lax.broadcasted_iota(jnp.int32, (w, V7X_LANES), 0)
        lane = lax.broadcasted_iota(jnp.int32, (w, V7X_LANES), 1)
        ck = lane & (w - 1)
        c0 = jnp.clip(cq - NA_KW // 2, 0, w - NA_KW)
        in_win = (ck >= c0) & (ck < c0 + NA_KW)
        dc_idx = jnp.where(in_win, jnp.clip(ck - cq, -(NA_KW - 1), NA_KW - 1) + (NA_KW - 1), -1)
        left = lane < w
        for p in range(len(pairs)):
            pair_ref[p] = jnp.full((w, V7X_LANES), neg_inf, F32)

        def per_dc(dc, carry):
            hit = dc_idx == dc
            for p, (da, db) in enumerate(pairs):
                va = neg_inf if da is None else rpb_ref[layer, (h * n_dr + da + NA_KH - 1) * n_dc + dc]
                vb = neg_inf if db is None else rpb_ref[layer, (h * n_dr + db + NA_KH - 1) * n_dc + dc]
                pair_ref[p] = jnp.where(hit, jnp.where(left, va, vb) * LOG2_E, pair_ref[p])
            return carry

        lax.fori_loop(0, n_dc, per_dc, 0)
        for t, desc in enumerate(types):
            for qi in range(r_grp):
                for m in range(win // 2):
                    pr = (desc[qi][2 * m], desc[qi][2 * m + 1])
                    tile = (jnp.full((w, V7X_LANES), neg_inf, F32) if pr == (None, None)
                            else pair_ref[pairs.index(pr)])
                    bias_ref[t, qi * w:(qi + 1) * w, m * V7X_LANES:(m + 1) * V7X_LANES] = tile

    kc = kc_ref[...].astype(BF16)
    vc = vc_ref[...].astype(BF16)
    nt = (((1,), (1,)), ((), ()))
    scale2 = scale * LOG2_E
    default_type = max(set(type_of_group), key=type_of_group.count)

    def scores(g):
        w0 = jnp.clip(r_grp * g - NA_KH // 2, 0, rows - win)
        q0 = pl.multiple_of(g * gq, gq)
        k0 = pl.multiple_of(w0 * w, w)
        t = jnp.int32(default_type)
        for gi, ti in enumerate(type_of_group):
            if ti != default_type:
                t = jnp.where(g == gi, ti, t)
        q = q_ref[pl.ds(q0, gq), :]
        s = lax.dot_general(q, k_ref[pl.ds(k0, wk), :], nt, preferred_element_type=F32) * scale2 + bias_ref[t]
        sc = lax.dot_general(q, kc, nt, preferred_element_type=F32) * scale2
        return q0, k0, s, sc

    def attend(q0, k0, s, sc):
        mx = jnp.maximum(jnp.max(s, axis=-1, keepdims=True), jnp.max(sc, axis=-1, keepdims=True))
        e = jnp.exp2(s - mx)
        ec = jnp.exp2(sc - mx)
        inv = 1.0 / (jnp.sum(e, axis=-1, keepdims=True) + jnp.sum(ec, axis=-1, keepdims=True))
        o = (jnp.dot(e.astype(BF16), v_ref[pl.ds(k0, wk), :], preferred_element_type=F32)
             + jnp.dot(ec.astype(BF16), vc, preferred_element_type=F32))
        o_ref[pl.ds(q0, gq), :] = (o * inv).astype(o_ref.dtype)

    def per_pair(i, carry):
        first, second = scores(2 * i), scores(2 * i + 1)
        attend(*first)
        attend(*second)
        return carry

    lax.fori_loop(0, rows // (2 * r_grp), per_pair, 0)


def _nbr_attention(layer, proj, cache_k, cache_v, rpb, n_batch, n_tok):
    assert 2 * GRID_W == V7X_LANES and n_tok % GRID_W == 0
    rows = n_tok // GRID_W
    assert rows % (2 * NBR_ROWS_PER_GROUP) == 0 and rows >= NA_KH + NBR_ROWS_PER_GROUP and GRID_W >= NA_KW
    layout = _nbr_layout(rows)
    r_grp, win = NBR_ROWS_PER_GROUP, NA_KH + NBR_ROWS_PER_GROUP
    t_ctx = cache_k.shape[3]
    blk = lambda col0: pl.BlockSpec((n_tok, HEAD_DIM), lambda h, b, l: (b, col0 + h))
    ctx = pl.BlockSpec((None, None, None, t_ctx, HEAD_DIM), lambda h, b, l: (b, l[0], h, 0, 0))
    return pl.pallas_call(
        functools.partial(_nbr_attn_kernel, rows=rows, layout=layout, scale=HEAD_DIM ** -0.5),
        grid_spec=pltpu.PrefetchScalarGridSpec(
            num_scalar_prefetch=1, grid=(N_HEADS, n_batch),
            in_specs=[pl.BlockSpec(memory_space=pltpu.SMEM),
                      blk(0), blk(N_HEADS), blk(2 * N_HEADS), ctx, ctx],
            out_specs=pl.BlockSpec((n_tok, HEAD_DIM), lambda h, b, l: (b, h)),
            scratch_shapes=[pltpu.VMEM((len(layout[0]), r_grp * GRID_W, win * GRID_W), F32),
                            pltpu.VMEM((len(layout[2]), GRID_W, V7X_LANES), F32)]),
        out_shape=jax.ShapeDtypeStruct((n_batch * n_tok, N_HEADS * HEAD_DIM), BF16),
        compiler_params=_cparams(2),
        name="nbr_attention",
    )(_layer_arg(layer), rpb.reshape(rpb.shape[0], -1), proj, proj, proj, cache_k, cache_v)


@functools.lru_cache(maxsize=None)
def _dft_tables(n):
    jk = (np.arange(n, dtype=np.int64)[:, None] * np.arange(n, dtype=np.int64)[None, :]) % n
    ang = 2.0 * np.pi * jk.astype(np.float64) / n
    scale = 1.0 / math.sqrt(n)
    return ((np.cos(ang) * scale).astype(BF16), (np.sin(ang) * scale).astype(BF16))


def _fourier_kernel(f_ref, cs_ref, wc_ref, ws_ref, o_ref, zc_ref, zs_ref, *, row_chunk):
    n, fd = f_ref.shape
    gd = fd // F_GROUPS

    @pl.when(pl.program_id(1) == 0)
    def _channel_stage():
        for r in range(n // row_chunk):
            rs = slice(r * row_chunk, (r + 1) * row_chunk)
            for g in range(F_GROUPS):
                sl = slice(g * gd, (g + 1) * gd)
                z = jnp.dot(f_ref[rs, sl], cs_ref[...], preferred_element_type=F32)
                zc_ref[rs, sl] = z[:, :gd].astype(BF16)
                zs_ref[rs, sl] = z[:, gd:].astype(BF16)

    o_ref[...] = (jnp.dot(wc_ref[...], zc_ref[...], preferred_element_type=F32)
                  + jnp.dot(ws_ref[...], zs_ref[...], preferred_element_type=F32)).astype(o_ref.dtype)


def _fourier(proj, n_seq, seq):
    attn = N_HEADS * HEAD_DIM
    fd = attn
    gd = fd // F_GROUPS
    cc, sc = _dft_tables(gd)
    cp, sp = _dft_tables(seq)
    cs = jnp.asarray(np.concatenate([cc, sc], axis=1))
    tm = min(TM_FOURIER, seq)
    assert seq % tm == 0 and proj.shape[1] // fd > 3
    return pl.pallas_call(
        functools.partial(_fourier_kernel, row_chunk=min(512, seq)),
        grid=(n_seq, seq // tm),
        in_specs=[pl.BlockSpec((seq, fd), lambda b, i: (b, 3)),
                  pl.BlockSpec((gd, 2 * gd), lambda b, i: (0, 0)),
                  pl.BlockSpec((tm, seq), lambda b, i: (i, 0)),
                  pl.BlockSpec((tm, seq), lambda b, i: (i, 0))],
        out_specs=pl.BlockSpec((tm, fd), lambda b, i: (b * (seq // tm) + i, 0)),
        out_shape=jax.ShapeDtypeStruct((n_seq * seq, fd), BF16),
        scratch_shapes=[pltpu.VMEM((seq, fd), BF16), pltpu.VMEM((seq, fd), BF16)],
        compiler_params=_cparams(2),
        name="fourier_%d" % seq,
    )(proj, cs, jnp.asarray(cp), jnp.asarray(-sp))


def _merge_out_kernel(l_ref, x_ref, a_ref, fo_ref, ga_ref, gb_ref, mod_ref, wpa_ref, wpb_ref, wout_ref,
                      o_ref, y_ref, *, tc):
    d = x_ref.shape[1]
    for c in range(d // tc):
        sl = slice(c * tc, (c + 1) * tc)
        ya = jnp.dot(a_ref[...], wpa_ref[:, sl], preferred_element_type=F32)
        yb = jnp.dot(fo_ref[...], wpb_ref[:, sl], preferred_element_type=F32)
        y = (jax.nn.sigmoid(ga_ref[:, sl].astype(F32)) * ya
             + jax.nn.sigmoid(gb_ref[:, sl].astype(F32)) * yb)
        y_ref[:, sl] = y.astype(BF16)
    for c in range(d // tc):
        sl = slice(c * tc, (c + 1) * tc)
        o = jnp.dot(y_ref[...], wout_ref[:, sl], preferred_element_type=F32)
        o_ref[:, sl] = x_ref[:, sl] + mod_ref[2:3, sl] * o


def _merge_out(layer, x, a, fo, proj, mods, w_pa, w_pb, w_out, *, rows_per_mod, mod_row0):
    m, d = x.shape
    attn = a.shape[1]
    tm = min(TM_MERGE, m)
    assert m % tm == 0 and (3 * attn + fo.shape[1]) % d == 0 and rows_per_mod % tm == 0
    gate_blk0 = (3 * attn + fo.shape[1]) // d
    whole = lambda arr: pl.BlockSpec((None,) + arr.shape[1:], lambda i, l: (l[0], 0, 0),
                                     pipeline_mode=pl.Buffered(1))
    return pl.pallas_call(
        functools.partial(_merge_out_kernel, tc=min(TC_MERGE, d)),
        grid_spec=pltpu.PrefetchScalarGridSpec(
            num_scalar_prefetch=1, grid=(m // tm,),
            in_specs=[pl.BlockSpec((tm, d), lambda i, l: (i, 0)),
                      pl.BlockSpec((tm, attn), lambda i, l: (i, 0)),
                      pl.BlockSpec((tm, fo.shape[1]), lambda i, l: (i, 0)),
                      pl.BlockSpec((tm, d), lambda i, l: (i, gate_blk0)),
                      pl.BlockSpec((tm, d), lambda i, l: (i, gate_blk0 + 1)),
                      pl.BlockSpec((None, None, N_MOD, d),
                                   lambda i, l: (l[0], mod_row0 + (i * tm) // rows_per_mod, 0, 0)),
                      whole(w_pa), whole(w_pb), whole(w_out)],
            out_specs=pl.BlockSpec((tm, d), lambda i, l: (i, 0)),
            scratch_shapes=[pltpu.VMEM((tm, d), BF16)]),
        out_shape=jax.ShapeDtypeStruct((m, d), F32),
        compiler_params=_cparams(1),
        name="merge_out",
    )(_layer_arg(layer), x, a, fo, proj, proj, mods, w_pa, w_pb, w_out)


def _ffn_kernel(l_ref, x_ref, xp_ref, xn_ref, mod_ref, g_ref, wa_ref, wg_ref, cwa_ref, cwg_ref, cba_ref,
                cbg_ref, wd_ref, o_ref, h_ref, u_ref, acc_ref, *, seq, tc):
    tm = x_ref.shape[0]
    halo = xp_ref.shape[0]
    tf = wa_ref.shape[1]
    seg = min(seq, tm)
    n_seg, stride = tm // seg, seg + halo
    n_rows = acc_ref.shape[0]
    i, f = pl.program_id(0), pl.program_id(1)

    @pl.when(f == 0)
    def _():
        g, scale, shift = g_ref[...], mod_ref[4:5, :], mod_ref[3:4, :]
        tok0 = i * tm
        _norm_mod_store(h_ref, 0, xp_ref, g, scale, shift, keep=(tok0 & (seq - 1)) != 0)
        for s in range(n_seg):
            _norm_mod_store(h_ref, halo + s * stride, x_ref, g, scale, shift, src_row0=s * seg, n=seg)
            if s < n_seg - 1:
                h_ref[(s + 1) * stride:(s + 1) * stride + halo, :] = jnp.zeros((halo, h_ref.shape[1]), BF16)
        _norm_mod_store(h_ref, n_seg * stride, xn_ref, g, scale, shift, keep=((tok0 + tm) & (seq - 1)) != 0)
        acc_ref[...] = jnp.zeros_like(acc_ref)

    h = h_ref[...]

    def conv(c, idx, cw_ref, cb_ref, cs):
        return (u_ref[c, idx, pl.ds(halo - 1, n_rows), :] * cw_ref[0:1, cs]
                + u_ref[c, idx, pl.ds(halo, n_rows), :] * cw_ref[1:2, cs]
                + u_ref[c, idx, pl.ds(halo + 1, n_rows), :] * cw_ref[2:3, cs] + cb_ref[:, cs])

    n_chunks = tf // tc
    for c in range(n_chunks):
        cs = slice(c * tc, (c + 1) * tc)
        u_ref[c, 0] = jnp.dot(h, wa_ref[:, cs], preferred_element_type=F32)
        u_ref[c, 1] = jnp.dot(h, wg_ref[:, cs], preferred_element_type=F32)
    for c in range(n_chunks):
        cs = slice(c * tc, (c + 1) * tc)
        ca = conv(c, 0, cwa_ref, cba_ref, cs)
        cg = conv(c, 1, cwg_ref, cbg_ref, cs)
        act = ((cg * jax.nn.sigmoid(cg)) * ca).astype(BF16)
        acc_ref[...] += jnp.dot(act, wd_ref[cs, :], preferred_element_type=F32)

    @pl.when(f == pl.num_programs(1) - 1)
    def _():
        for s in range(n_seg):
            rows = slice(s * seg, (s + 1) * seg)
            o_ref[rows, :] = x_ref[rows, :] + mod_ref[5:6, :] * acc_ref[s * stride:s * stride + seg, :]


def _conv_ffn(layer, x, mods, g, w_up, conv_w, conv_b, w_down, *, seq, rows_per_mod, mod_row0):
    m, d = x.shape
    ffn = w_down.shape[1]
    tm, tf = min(TM_FFN, m), min(TF_FFN, ffn)
    tc = min(TC_FFN, tf)
    halo = V7X_BF16_SUBLANES
    assert seq & (seq - 1) == 0 and (seq % tm == 0 or tm % seq == 0) and m % seq == 0
    assert m % tm == 0 and ffn % tf == 0 and tf % tc == 0 and min(seq, tm) % halo == 0 and rows_per_mod % tm == 0
    nf, nhb = ffn // tf, m // halo
    work_rows = tm + (tm // min(seq, tm) + 1) * halo
    mat = lambda rows, cols, imap: pl.BlockSpec((None, rows, cols), imap)
    return pl.pallas_call(
        functools.partial(_ffn_kernel, seq=seq, tc=tc),
        grid_spec=pltpu.PrefetchScalarGridSpec(
            num_scalar_prefetch=1, grid=(m // tm, nf),
            in_specs=[pl.BlockSpec((tm, d), lambda i, f, l: (i, 0)),
                      pl.BlockSpec((halo, d), lambda i, f, l: (jnp.maximum(i * (tm // halo) - 1, 0), 0)),
                      pl.BlockSpec((halo, d),
                                   lambda i, f, l: (jnp.minimum((i + 1) * (tm // halo), nhb - 1), 0)),
                      pl.BlockSpec((None, None, N_MOD, d),
                                   lambda i, f, l: (l[0], mod_row0 + (i * tm) // rows_per_mod, 0, 0)),
                      mat(1, d, lambda i, f, l: (l[0], 0, 0)),
                      mat(d, tf, lambda i, f, l: (l[0], 0, f)),
                      mat(d, tf, lambda i, f, l: (l[0], 0, nf + f)),
                      mat(3, tf, lambda i, f, l: (l[0], 0, f)),
                      mat(3, tf, lambda i, f, l: (l[0], 0, nf + f)),
                      mat(1, tf, lambda i, f, l: (l[0], 0, f)),
                      mat(1, tf, lambda i, f, l: (l[0], 0, nf + f)),
                      mat(tf, d, lambda i, f, l: (l[0], f, 0))],
            out_specs=pl.BlockSpec((tm, d), lambda i, f, l: (i, 0)),
            scratch_shapes=[pltpu.VMEM((work_rows, d), BF16),
                            pltpu.VMEM((tf // tc, 2, work_rows, tc), F32),
                            pltpu.VMEM((work_rows - 2 * halo, d), F32)]),
        out_shape=jax.ShapeDtypeStruct((m, d), F32),
        compiler_params=_cparams(2),
        name="conv_ffn",
    )(_layer_arg(layer), x, x, x, mods, g.reshape(g.shape[0], 1, d), w_up, w_up, conv_w, conv_w,
      conv_b.reshape(conv_b.shape[0], 1, -1), conv_b.reshape(conv_b.shape[0], 1, -1), w_down)


def _final_norm_kernel(x_ref, g_ref, o_ref):
    x = x_ref[...]
    r = lax.rsqrt(jnp.mean(x * x, axis=-1, keepdims=True) + RMS_EPS)
    o_ref[...] = (x * r) * g_ref[...]


def _final_norm(x, g):
    m, d = x.shape
    tm = min(TM_NORM, m)
    assert m % tm == 0
    return pl.pallas_call(
        _final_norm_kernel,
        grid=(m // tm,),
        in_specs=[pl.BlockSpec((tm, d), lambda i: (i, 0)), pl.BlockSpec((1, d), lambda i: (0, 0))],
        out_specs=pl.BlockSpec((tm, d), lambda i: (i, 0)),
        out_shape=jax.ShapeDtypeStruct((m, d), F32),
        compiler_params=_cparams(1),
        name="final_norm",
    )(x, g.reshape(1, d))


def kernel(x_prompt, x_sample, cache_k, cache_v, c, c_ctx, w_mod, b_mod, norm1_g, w_in, rpb, w_pa, w_pb,
           w_out, norm2_g, w_up, conv_w, conv_b, w_down, norm_f_g):
    n_ctx, seq, d = x_prompt.shape
    n_lat, n_tok, _ = x_sample.shape
    n_layers = w_mod.shape[0]
    assert n_lat < MOD_ROWS

    c_all = jnp.zeros((MOD_ROWS, d), F32).at[:n_lat].set(c).at[n_lat].set(c_ctx)
    mods = _modulation(c_all, w_mod, b_mod).reshape(n_layers, MOD_ROWS, N_MOD, d)
    w_in, w_pa, w_pb, w_out, w_up, w_down = (w.astype(BF16) for w in (w_in, w_pa, w_pb, w_out, w_up, w_down))

    xp = x_prompt.reshape(n_ctx * seq, d)
    xs = x_sample.reshape(n_lat * n_tok, d)
    ctx_mod = dict(rows_per_mod=n_ctx * seq, mod_row0=n_lat)
    lat_mod = dict(rows_per_mod=n_tok, mod_row0=0)
    caches = (jnp.zeros((n_ctx, n_layers, N_HEADS, seq, HEAD_DIM), F32),) * 2
    for l in range(n_layers):
        proj, *caches = _in_proj(l, xp, mods, norm1_g, w_in, caches=caches, **ctx_mod)
        a = _ctx_attention(proj, n_ctx, seq)
        fo = _fourier(proj, n_ctx, seq)
        xp = _merge_out(l, xp, a, fo, proj, mods, w_pa, w_pb, w_out, **ctx_mod)
        xp = _conv_ffn(l, xp, mods, norm2_g, w_up, conv_w, conv_b, w_down, seq=seq, **ctx_mod)

        (proj,) = _in_proj(l, xs, mods, norm1_g, w_in, **lat_mod)
        a = _nbr_attention(l, proj, cache_k, cache_v, rpb, n_lat, n_tok)
        fo = _fourier(proj, n_lat, n_tok)
        xs = _merge_out(l, xs, a, fo, proj, mods, w_pa, w_pb, w_out, **lat_mod)
        xs = _conv_ffn(l, xs, mods, norm2_g, w_up, conv_w, conv_b, w_down, seq=n_tok, **lat_mod)

    y_prompt = _final_norm(xp, norm_f_g).reshape(n_ctx, seq, d)
    y_sample = _final_norm(xs, norm_f_g).reshape(n_lat, n_tok, d)
    return (y_prompt, y_sample, caches[0], caches[1])
```

```python
import functools
import math

import numpy as np
import jax
import jax.numpy as jnp
from jax import lax
from jax.experimental import pallas as pl
from jax.experimental.pallas import tpu as pltpu

F32 = jnp.float32
BF16 = jnp.bfloat16

RMS_EPS = 1e-6
LOG2_E = math.log2(math.e)
N_MOD = 6
N_HEADS = 8
HEAD_DIM = 128
GRID_W = 64
NA_KH = 8
NA_KW = 16
F_GROUPS = 4
NBR_ROWS_PER_GROUP = 4
MOD_ROWS = 8
V7X_LANES = 128
V7X_BF16_SUBLANES = 16
V7X_VMEM_BYTES = 64 * 1024 * 1024
VMEM_LIMIT = V7X_VMEM_BYTES - 8 * 1024 * 1024

TM_IN, TN_IN = 1024, 1024
TN_MOD = 1024
TM_MERGE, TC_MERGE = 512, 512
TM_FFN, TF_FFN, TC_FFN = 512, 512, 256
TM_FOURIER = 256
NORM_ROW_CHUNK = 16
NORM_UNROLL = 4


def _cparams(n_axes):
    return pltpu.CompilerParams(dimension_semantics=("arbitrary",) * n_axes,
                                vmem_limit_bytes=VMEM_LIMIT)


def _layer_arg(layer):
    return jnp.full((1,), layer, jnp.int32)


def _norm_mod_store(dst_ref, dst_row0, src_ref, g, scale, shift, keep=None, src_row0=0, n=None):
    n = src_ref.shape[0] if n is None else n
    chunk = min(NORM_ROW_CHUNK, n)
    assert n % chunk == 0 and src_row0 % chunk == 0 and dst_row0 % chunk == 0
    gs = g * (1.0 + scale)

    def rows(r, carry):
        r0 = pl.multiple_of(r * chunk, chunk)
        x = src_ref[pl.ds(src_row0 + r0, chunk), :]
        inv = lax.rsqrt(jnp.mean(x * x, axis=-1, keepdims=True) + RMS_EPS)
        h = (x * inv) * gs + shift
        if keep is not None:
            h = jnp.where(keep, h, 0.0)
        dst_ref[pl.ds(dst_row0 + r0, chunk), :] = h.astype(BF16)
        return carry

    lax.fori_loop(0, n // chunk, rows, 0, unroll=min(NORM_UNROLL, n // chunk))


def _mod_kernel(c_ref, w_ref, b_ref, o_ref):
    cv = c_ref[...]
    s = (cv * jax.nn.sigmoid(cv)).astype(BF16)
    o_ref[...] = jnp.dot(s, w_ref[...].astype(BF16), preferred_element_type=F32) + b_ref[...]


def _modulation(c_all, w_mod, b_mod):
    n_layers, d, nd = w_mod.shape
    tn = min(TN_MOD, nd)
    assert nd % tn == 0
    return pl.pallas_call(
        _mod_kernel,
        grid=(n_layers, nd // tn),
        in_specs=[pl.BlockSpec((MOD_ROWS, d), lambda l, j: (0, 0)),
                  pl.BlockSpec((None, d, tn), lambda l, j: (l, 0, j)),
                  pl.BlockSpec((None, 1, tn), lambda l, j: (l, 0, j))],
        out_specs=pl.BlockSpec((None, MOD_ROWS, tn), lambda l, j: (l, 0, j)),
        out_shape=jax.ShapeDtypeStruct((n_layers, MOD_ROWS, nd), F32),
        compiler_params=_cparams(2),
        name="modulation",
    )(c_all, w_mod, b_mod.reshape(n_layers, 1, nd))


def _in_proj_kernel(l_ref, x_ref, mod_ref, g_ref, w_ref, *rest, kv_tiles):
    h_ref = rest[-1]
    j = pl.program_id(1)

    @pl.when(j == 0)
    def _():
        _norm_mod_store(h_ref, 0, x_ref, g_ref[...], mod_ref[1:2, :], mod_ref[0:1, :])

    res = jnp.dot(h_ref[...], w_ref[...], preferred_element_type=F32)
    if kv_tiles is None:
        o_ref = rest[0]
        o_ref[...] = res.astype(o_ref.dtype)
    else:
        o_ref, kc_ref, vc_ref = rest[2], rest[3], rest[4]
        o_ref[...] = res.astype(o_ref.dtype)
        k0, v0, v1 = kv_tiles
        nb, nh, s, dh = kc_ref.shape

        def scatter(dst_ref):
            for bb in range(nb):
                for hh in range(nh):
                    dst_ref[bb, hh] = res[bb * s:(bb + 1) * s, hh * dh:(hh + 1) * dh]

        @pl.when((j >= k0) & (j < v0))
        def _():
            scatter(kc_ref)

        @pl.when((j >= v0) & (j < v1))
        def _():
            scatter(vc_ref)


def _in_proj(layer, x, mods, g, w, *, rows_per_mod, mod_row0, caches=None):
    m, d = x.shape
    n_out = w.shape[2]
    tm, tn = min(TM_IN, m), min(TN_IN, n_out)
    assert m % tm == 0 and n_out % tn == 0 and (rows_per_mod % tm == 0 or tm % rows_per_mod == 0)
    attn = N_HEADS * HEAD_DIM
    in_specs = [pl.BlockSpec((tm, d), lambda i, j, l: (i, 0)),
                pl.BlockSpec((None, None, N_MOD, d),
                             lambda i, j, l: (l[0], mod_row0 + (i * tm) // rows_per_mod, 0, 0)),
                pl.BlockSpec((None, 1, d), lambda i, j, l: (l[0], 0, 0)),
                pl.BlockSpec((None, d, tn), lambda i, j, l: (l[0], 0, j))]
    out_specs = [pl.BlockSpec((tm, tn), lambda i, j, l: (i, j))]
    out_shape = [jax.ShapeDtypeStruct((m, n_out), BF16)]
    args = [_layer_arg(layer), x, mods, g.reshape(g.shape[0], 1, d), w]
    kv_tiles, aliases = None, {}
    if caches is not None:
        seq = caches[0].shape[3]
        assert tm % seq == 0 and tn % HEAD_DIM == 0 and attn % tn == 0
        k0, v0, v1 = attn // tn, 2 * attn // tn, 3 * attn // tn
        kv_tiles = (k0, v0, v1)
        blk = (tm // seq, None, tn // HEAD_DIM, seq, HEAD_DIM)
        out_specs += [pl.BlockSpec(blk, lambda i, j, l: (i, l[0], jnp.clip(j - k0, 0, v0 - k0 - 1), 0, 0)),
                      pl.BlockSpec(blk, lambda i, j, l: (i, l[0], jnp.clip(j - v0, 0, v1 - v0 - 1), 0, 0))]
        out_shape += [jax.ShapeDtypeStruct(c.shape, c.dtype) for c in caches]
        in_specs += [pl.BlockSpec(memory_space=pl.ANY)] * 2
        aliases = {len(args): 1, len(args) + 1: 2}
        args += list(caches)
    return pl.pallas_call(
        functools.partial(_in_proj_kernel, kv_tiles=kv_tiles),
        grid_spec=pltpu.PrefetchScalarGridSpec(
            num_scalar_prefetch=1, grid=(m // tm, n_out // tn), in_specs=in_specs, out_specs=out_specs,
            scratch_shapes=[pltpu.VMEM((tm, d), BF16)]),
        out_shape=out_shape,
        input_output_aliases=aliases,
        compiler_params=_cparams(2),
        name="in_proj_ctx" if caches is not None else "in_proj_lat",
    )(*args)


def _ctx_attn_kernel(q_ref, k_ref, v_ref, o_ref, *, scale):
    for h in range(N_HEADS):
        sl = slice(h * HEAD_DIM, (h + 1) * HEAD_DIM)
        s = lax.dot_general(q_ref[:, sl], k_ref[:, sl], (((1,), (1,)), ((), ())),
                            preferred_element_type=F32) * scale
        e = jnp.exp(s - jnp.max(s, axis=-1, keepdims=True))
        p = e * (1.0 / jnp.sum(e, axis=-1, keepdims=True))
        o_ref[:, sl] = jnp.dot(p.astype(BF16), v_ref[:, sl], preferred_element_type=F32).astype(o_ref.dtype)


def _ctx_attention(proj, n_seq, seq):
    attn = N_HEADS * HEAD_DIM
    return pl.pallas_call(
        functools.partial(_ctx_attn_kernel, scale=HEAD_DIM ** -0.5),
        grid=(n_seq,),
        in_specs=[pl.BlockSpec((seq, attn), lambda b: (b, 0)),
                  pl.BlockSpec((seq, attn), lambda b: (b, 1)),
                  pl.BlockSpec((seq, attn), lambda b: (b, 2))],
        out_specs=pl.BlockSpec((seq, attn), lambda b: (b, 0)),
        out_shape=jax.ShapeDtypeStruct((n_seq * seq, attn), BF16),
        compiler_params=_cparams(1),
        name="ctx_attention",
    )(proj, proj, proj)


def _nbr_layout(rows):
    r_grp, win = NBR_ROWS_PER_GROUP, NA_KH + NBR_ROWS_PER_GROUP
    types, type_of_group = [], []
    for g in range(rows // r_grp):
        w0 = min(max(r_grp * g - NA_KH // 2, 0), rows - win)
        desc = []
        for qi in range(r_grp):
            r = r_grp * g + qi
            r0 = min(max(r - NA_KH // 2, 0), rows - NA_KH)
            desc.append(tuple((w0 + kj - r) if r0 <= w0 + kj < r0 + NA_KH else None for kj in range(win)))
        desc = tuple(desc)
        if desc not in types:
            types.append(desc)
        type_of_group.append(types.index(desc))
    pairs = sorted({(d[qi][2 * m], d[qi][2 * m + 1]) for d in types for qi in range(r_grp)
                    for m in range(win // 2)} - {(None, None)},
                   key=lambda p: tuple(-99 if v is None else v for v in p))
    return types, type_of_group, pairs


def _nbr_attn_kernel(l_ref, rpb_ref, q_ref, k_ref, v_ref, kc_ref, vc_ref, o_ref, bias_ref, pair_ref, *,
                     rows, layout, scale):
    types, type_of_group, pairs = layout
    w = GRID_W
    r_grp, win = NBR_ROWS_PER_GROUP, NA_KH + NBR_ROWS_PER_GROUP
    gq, wk = r_grp * w, win * w
    n_dr, n_dc = 2 * NA_KH - 1, 2 * NA_KW - 1
    neg_inf = float("-inf")
    h = pl.program_id(0)
    layer = l_ref[0]

    @pl.when(pl.program_id(1) == 0)
    def _build_bias_tables():
        cq = lax.broadcasted_iota(jnp.int32, (w, V7X_LANES), 0)
        lane = lax.broadcasted_iota(jnp.int32, (w, V7X_LANES), 1)
        ck = lane & (w - 1)
        c0 = jnp.clip(cq - NA_KW // 2, 0, w - NA_KW)
        in_win = (ck >= c0) & (ck < c0 + NA_KW)
        dc_idx = jnp.where(in_win, jnp.clip(ck - cq, -(NA_KW - 1), NA_KW - 1) + (NA_KW - 1), -1)
        left = lane < w
        for p in range(len(pairs)):
            pair_ref[p] = jnp.full((w, V7X_LANES), neg_inf, F32)

        def per_dc(dc, carry):
            hit = dc_idx == dc
            for p, (da, db) in enumerate(pairs):
                va = neg_inf if da is None else rpb_ref[layer, (h * n_dr + da + NA_KH - 1) * n_dc + dc]
                vb = neg_inf if db is None else rpb_ref[layer, (h * n_dr + db + NA_KH - 1) * n_dc + dc]
                pair_ref[p] = jnp.where(hit, jnp.where(left, va, vb) * LOG2_E, pair_ref[p])
            return carry

        lax.fori_loop(0, n_dc, per_dc, 0)
        for t, desc in enumerate(types):
            for qi in range(r_grp):
                for m in range(win // 2):
                    pr = (desc[qi][2 * m], desc[qi][2 * m + 1])
                    tile = (jnp.full((w, V7X_LANES), neg_inf, F32) if pr == (None, None)
                            else pair_ref[pairs.index(pr)])
                    bias_ref[t, qi * w:(qi + 1) * w, m * V7X_LANES:(m + 1) * V7X_LANES] = tile

    kc = kc_ref[...].astype(BF16)
    vc = vc_ref[...].astype(BF16)
    nt = (((1,), (1,)), ((), ()))
    scale2 = scale * LOG2_E
    default_type = max(set(type_of_group), key=type_of_group.count)

    def scores(g):
        w0 = jnp.clip(r_grp * g - NA_KH // 2, 0, rows - win)
        q0 = pl.multiple_of(g * gq, gq)
        k0 = pl.multiple_of(w0 * w, w)
        t = jnp.int32(default_type)
        for gi, ti in enumerate(type_of_group):
            if ti != default_type:
                t = jnp.where(g == gi, ti, t)
        q = q_ref[pl.ds(q0, gq), :]
        s = lax.dot_general(q, k_ref[pl.ds(k0, wk), :], nt, preferred_element_type=F32) * scale2 + bias_ref[t]
        sc = lax.dot_general(q, kc, nt, preferred_element_type=F32) * scale2
        return q0, k0, s, sc

    def attend(q0, k0, s, sc):
        mx = jnp.maximum(jnp.max(s, axis=-1, keepdims=True), jnp.max(sc, axis=-1, keepdims=True))
        e = jnp.exp2(s - mx)
        ec = jnp.exp2(sc - mx)
        inv = 1.0 / (jnp.sum(e, axis=-1, keepdims=True) + jnp.sum(ec, axis=-1, keepdims=True))
        o = (jnp.dot(e.astype(BF16), v_ref[pl.ds(k0, wk), :], preferred_element_type=F32)
             + jnp.dot(ec.astype(BF16), vc, preferred_element_type=F32))
        o_ref[pl.ds(q0, gq), :] = (o * inv).astype(o_ref.dtype)

    def per_pair(i, carry):
        first, second = scores(2 * i), scores(2 * i + 1)
        attend(*first)
        attend(*second)
        return carry

    lax.fori_loop(0, rows // (2 * r_grp), per_pair, 0)


def _nbr_attention(layer, proj, cache_k, cache_v, rpb, n_batch, n_tok):
    assert 2 * GRID_W == V7X_LANES and n_tok % GRID_W == 0
    rows = n_tok // GRID_W
    assert rows % (2 * NBR_ROWS_PER_GROUP) == 0 and rows >= NA_KH + NBR_ROWS_PER_GROUP and GRID_W >= NA_KW
    layout = _nbr_layout(rows)
    r_grp, win = NBR_ROWS_PER_GROUP, NA_KH + NBR_ROWS_PER_GROUP
    t_ctx = cache_k.shape[3]
    blk = lambda col0: pl.BlockSpec((n_tok, HEAD_DIM), lambda h, b, l: (b, col0 + h))
    ctx = pl.BlockSpec((None, None, None, t_ctx, HEAD_DIM), lambda h, b, l: (b, l[0], h, 0, 0))
    return pl.pallas_call(
        functools.partial(_nbr_attn_kernel, rows=rows, layout=layout, scale=HEAD_DIM ** -0.5),
        grid_spec=pltpu.PrefetchScalarGridSpec(
            num_scalar_prefetch=1, grid=(N_HEADS, n_batch),
            in_specs=[pl.BlockSpec(memory_space=pltpu.SMEM),
                      blk(0), blk(N_HEADS), blk(2 * N_HEADS), ctx, ctx],
            out_specs=pl.BlockSpec((n_tok, HEAD_DIM), lambda h, b, l: (b, h)),
            scratch_shapes=[pltpu.VMEM((len(layout[0]), r_grp * GRID_W, win * GRID_W), F32),
                            pltpu.VMEM((len(layout[2]), GRID_W, V7X_LANES), F32)]),
        out_shape=jax.ShapeDtypeStruct((n_batch * n_tok, N_HEADS * HEAD_DIM), BF16),
        compiler_params=_cparams(2),
        name="nbr_attention",
    )(_layer_arg(layer), rpb.reshape(rpb.shape[0], -1), proj, proj, proj, cache_k, cache_v)


@functools.lru_cache(maxsize=None)
def _dft_tables(n):
    jk = (np.arange(n, dtype=np.int64)[:, None] * np.arange(n, dtype=np.int64)[None, :]) % n
    ang = 2.0 * np.pi * jk.astype(np.float64) / n
    scale = 1.0 / math.sqrt(n)
    return ((np.cos(ang) * scale).astype(BF16), (np.sin(ang) * scale).astype(BF16))


@functools.lru_cache(maxsize=None)
def _mirror_matrix(t):
    r = np.zeros((t, t), np.float32)
    q = np.arange(1, t)
    r[q, t - q] = 1.0
    return r.astype(BF16)


def _fourier_kernel(f_ref, cc_ref, sc_ref, rev_ref, wc_ref, ws_ref, o_ref, zc_ref, zs_ref, edge_ref):
    n, fd = f_ref.shape
    gd = fd // F_GROUPS
    half = n // 2
    tile = rev_ref.shape[0]
    n_tiles = half // tile

    @pl.when(pl.program_id(1) == 0)
    def _channel_stage():
        for t in range(n_tiles):
            rows = slice(t * tile, (t + 1) * tile)
            lo = f_ref[rows, :].astype(F32)
            src = (2 * n_tiles - 1 - t) * tile
            mir = jnp.dot(rev_ref[...], f_ref[src:src + tile, :], preferred_element_type=F32)
            if t > 0:
                first = lax.broadcasted_iota(jnp.int32, (tile, 1), 0) == 0
                mir = jnp.where(first, f_ref[src + tile:src + tile + 1, :].astype(F32), mir)
            x_even = (lo + mir).astype(BF16)
            x_odd = (lo - mir).astype(BF16)
            for g in range(F_GROUPS):
                sl = slice(g * gd, (g + 1) * gd)
                zc_ref[rows, sl] = jnp.dot(x_even[:, sl], cc_ref[...], preferred_element_type=F32).astype(BF16)
                zs_ref[rows, sl] = jnp.dot(x_odd[:, sl], sc_ref[...], preferred_element_type=F32).astype(BF16)
        sign = 1.0 - 2.0 * (lax.broadcasted_iota(jnp.int32, (8, 1), 0) & 1).astype(F32)
        x_half = f_ref[half:half + V7X_BF16_SUBLANES, :]
        for g in range(F_GROUPS):
            sl = slice(g * gd, (g + 1) * gd)
            z_half = jnp.dot(x_half[:, sl], cc_ref[...], preferred_element_type=F32)[0:1, :]
            edge_ref[:, sl] = sign * (z_half * (1.0 / math.sqrt(n)))

    y = (jnp.dot(wc_ref[...], zc_ref[...], preferred_element_type=F32)
         + jnp.dot(ws_ref[...], zs_ref[...], preferred_element_type=F32))
    tm = y.shape[0]
    y = y.reshape(tm // 8, 8, fd) + edge_ref[...][None]
    o_ref[...] = y.reshape(tm, fd).astype(o_ref.dtype)


def _fourier(proj, n_seq, seq):
    attn = N_HEADS * HEAD_DIM
    fd = attn
    gd = fd // F_GROUPS
    cc, sc = _dft_tables(gd)
    cp, sp = _dft_tables(seq)
    half = seq // 2
    tm, tile = min(TM_FOURIER, seq), min(TM_FOURIER, half)
    assert seq % tm == 0 and half % tile == 0 and tile % V7X_BF16_SUBLANES == 0 and proj.shape[1] // fd > 3
    return pl.pallas_call(
        _fourier_kernel,
        grid=(n_seq, seq // tm),
        in_specs=[pl.BlockSpec((seq, fd), lambda b, i: (b, 3)),
                  pl.BlockSpec((gd, gd), lambda b, i: (0, 0)),
                  pl.BlockSpec((gd, gd), lambda b, i: (0, 0)),
                  pl.BlockSpec((tile, tile), lambda b, i: (0, 0)),
                  pl.BlockSpec((tm, half), lambda b, i: (i, 0)),
                  pl.BlockSpec((tm, half), lambda b, i: (i, 0))],
        out_specs=pl.BlockSpec((tm, fd), lambda b, i: (b * (seq // tm) + i, 0)),
        out_shape=jax.ShapeDtypeStruct((n_seq * seq, fd), BF16),
        scratch_shapes=[pltpu.VMEM((half, fd), BF16), pltpu.VMEM((half, fd), BF16), pltpu.VMEM((8, fd), F32)],
        compiler_params=_cparams(2),
        name="fourier_%d" % seq,
    )(proj, jnp.asarray(cc), jnp.asarray(sc), jnp.asarray(_mirror_matrix(tile)),
      jnp.asarray(cp[:, :half]), jnp.asarray(-sp[:, :half]))


def _merge_out_kernel(l_ref, x_ref, a_ref, fo_ref, ga_ref, gb_ref, mod_ref, wpa_ref, wpb_ref, wout_ref,
                      o_ref, y_ref, *, tc):
    d = x_ref.shape[1]
    for c in range(d // tc):
        sl = slice(c * tc, (c + 1) * tc)
        ya = jnp.dot(a_ref[...], wpa_ref[:, sl], preferred_element_type=F32)
        yb = jnp.dot(fo_ref[...], wpb_ref[:, sl], preferred_element_type=F32)
        y = (jax.nn.sigmoid(ga_ref[:, sl].astype(F32)) * ya
             + jax.nn.sigmoid(gb_ref[:, sl].astype(F32)) * yb)
        y_ref[:, sl] = y.astype(BF16)
    for c in range(d // tc):
        sl = slice(c * tc, (c + 1) * tc)
        o = jnp.dot(y_ref[...], wout_ref[:, sl], preferred_element_type=F32)
        o_ref[:, sl] = x_ref[:, sl] + mod_ref[2:3, sl] * o


def _merge_out(layer, x, a, fo, proj, mods, w_pa, w_pb, w_out, *, rows_per_mod, mod_row0):
    m, d = x.shape
    attn = a.shape[1]
    tm = min(TM_MERGE, m)
    assert m % tm == 0 and (3 * attn + fo.shape[1]) % d == 0 and rows_per_mod % tm == 0
    gate_blk0 = (3 * attn + fo.shape[1]) // d
    whole = lambda arr: pl.BlockSpec((None,) + arr.shape[1:], lambda i, l: (l[0], 0, 0),
                                     pipeline_mode=pl.Buffered(1))
    return pl.pallas_call(
        functools.partial(_merge_out_kernel, tc=min(TC_MERGE, d)),
        grid_spec=pltpu.PrefetchScalarGridSpec(
            num_scalar_prefetch=1, grid=(m // tm,),
            in_specs=[pl.BlockSpec((tm, d), lambda i, l: (i, 0)),
                      pl.BlockSpec((tm, attn), lambda i, l: (i, 0)),
                      pl.BlockSpec((tm, fo.shape[1]), lambda i, l: (i, 0)),
                      pl.BlockSpec((tm, d), lambda i, l: (i, gate_blk0)),
                      pl.BlockSpec((tm, d), lambda i, l: (i, gate_blk0 + 1)),
                      pl.BlockSpec((None, None, N_MOD, d),
                                   lambda i, l: (l[0], mod_row0 + (i * tm) // rows_per_mod, 0, 0)),
                      whole(w_pa), whole(w_pb), whole(w_out)],
            out_specs=pl.BlockSpec((tm, d), lambda i, l: (i, 0)),
            scratch_shapes=[pltpu.VMEM((tm, d), BF16)]),
        out_shape=jax.ShapeDtypeStruct((m, d), F32),
        compiler_params=_cparams(1),
        name="merge_out",
    )(_layer_arg(layer), x, a, fo, proj, proj, mods, w_pa, w_pb, w_out)


def _ffn_kernel(l_ref, x_ref, xp_ref, xn_ref, mod_ref, g_ref, wa_ref, wg_ref, cwa_ref, cwg_ref, cba_ref,
                cbg_ref, wd_ref, gf_ref, o_ref, h_ref, u_ref, acc_ref, *, seq, tc, final_norm):
    tm = x_ref.shape[0]
    halo = xp_ref.shape[0]
    tf = wa_ref.shape[1]
    seg = min(seq, tm)
    n_seg, stride = tm // seg, seg + halo
    n_rows = acc_ref.shape[0]
    i, f = pl.program_id(0), pl.program_id(1)

    @pl.when(f == 0)
    def _():
        g, scale, shift = g_ref[...], mod_ref[4:5, :], mod_ref[3:4, :]
        tok0 = i * tm
        _norm_mod_store(h_ref, 0, xp_ref, g, scale, shift, keep=(tok0 & (seq - 1)) != 0)
        for s in range(n_seg):
            _norm_mod_store(h_ref, halo + s * stride, x_ref, g, scale, shift, src_row0=s * seg, n=seg)
            if s < n_seg - 1:
                h_ref[(s + 1) * stride:(s + 1) * stride + halo, :] = jnp.zeros((halo, h_ref.shape[1]), BF16)
        _norm_mod_store(h_ref, n_seg * stride, xn_ref, g, scale, shift, keep=((tok0 + tm) & (seq - 1)) != 0)
        acc_ref[...] = jnp.zeros_like(acc_ref)

    h = h_ref[...]

    def conv(c, idx, cw_ref, cb_ref, cs):
        return (u_ref[c, idx, pl.ds(halo - 1, n_rows), :] * cw_ref[0:1, cs]
                + u_ref[c, idx, pl.ds(halo, n_rows), :] * cw_ref[1:2, cs]
                + u_ref[c, idx, pl.ds(halo + 1, n_rows), :] * cw_ref[2:3, cs] + cb_ref[:, cs])

    n_chunks = tf // tc
    for c in range(n_chunks):
        cs = slice(c * tc, (c + 1) * tc)
        u_ref[c, 0] = jnp.dot(h, wa_ref[:, cs], preferred_element_type=F32)
        u_ref[c, 1] = jnp.dot(h, wg_ref[:, cs], preferred_element_type=F32)
    for c in range(n_chunks):
        cs = slice(c * tc, (c + 1) * tc)
        ca = conv(c, 0, cwa_ref, cba_ref, cs)
        cg = conv(c, 1, cwg_ref, cbg_ref, cs)
        act = ((cg * jax.nn.sigmoid(cg)) * ca).astype(BF16)
        acc_ref[...] += jnp.dot(act, wd_ref[cs, :], preferred_element_type=F32)

    @pl.when(f == pl.num_programs(1) - 1)
    def _():
        for s in range(n_seg):
            rows = slice(s * seg, (s + 1) * seg)
            y = x_ref[rows, :] + mod_ref[5:6, :] * acc_ref[s * stride:s * stride + seg, :]
            if final_norm:
                y = (y * lax.rsqrt(jnp.mean(y * y, axis=-1, keepdims=True) + RMS_EPS)) * gf_ref[...]
            o_ref[rows, :] = y


def _conv_ffn(layer, x, mods, g, w_up, conv_w, conv_b, w_down, final_gain, *, seq, rows_per_mod, mod_row0,
              final_norm):
    m, d = x.shape
    ffn = w_down.shape[1]
    tm, tf = min(TM_FFN, m), min(TF_FFN, ffn)
    tc = min(TC_FFN, tf)
    halo = V7X_BF16_SUBLANES
    assert seq & (seq - 1) == 0 and (seq % tm == 0 or tm % seq == 0) and m % seq == 0
    assert m % tm == 0 and ffn % tf == 0 and tf % tc == 0 and min(seq, tm) % halo == 0 and rows_per_mod % tm == 0
    nf, nhb = ffn // tf, m // halo
    work_rows = tm + (tm // min(seq, tm) + 1) * halo
    mat = lambda rows, cols, imap: pl.BlockSpec((None, rows, cols), imap)
    return pl.pallas_call(
        functools.partial(_ffn_kernel, seq=seq, tc=tc, final_norm=final_norm),
        grid_spec=pltpu.PrefetchScalarGridSpec(
            num_scalar_prefetch=1, grid=(m // tm, nf),
            in_specs=[pl.BlockSpec((tm, d), lambda i, f, l: (i, 0)),
                      pl.BlockSpec((halo, d), lambda i, f, l: (jnp.maximum(i * (tm // halo) - 1, 0), 0)),
                      pl.BlockSpec((halo, d),
                                   lambda i, f, l: (jnp.minimum((i + 1) * (tm // halo), nhb - 1), 0)),
                      pl.BlockSpec((None, None, N_MOD, d),
                                   lambda i, f, l: (l[0], mod_row0 + (i * tm) // rows_per_mod, 0, 0)),
                      mat(1, d, lambda i, f, l: (l[0], 0, 0)),
                      mat(d, tf, lambda i, f, l: (l[0], 0, f)),
                      mat(d, tf, lambda i, f, l: (l[0], 0, nf + f)),
                      mat(3, tf, lambda i, f, l: (l[0], 0, f)),
                      mat(3, tf, lambda i, f, l: (l[0], 0, nf + f)),
                      mat(1, tf, lambda i, f, l: (l[0], 0, f)),
                      mat(1, tf, lambda i, f, l: (l[0], 0, nf + f)),
                      mat(tf, d, lambda i, f, l: (l[0], f, 0)),
                      pl.BlockSpec((1, d), lambda i, f, l: (0, 0))],
            out_specs=pl.BlockSpec((tm, d), lambda i, f, l: (i, 0)),
            scratch_shapes=[pltpu.VMEM((work_rows, d), BF16),
                            pltpu.VMEM((tf // tc, 2, work_rows, tc), F32),
                            pltpu.VMEM((work_rows - 2 * halo, d), F32)]),
        out_shape=jax.ShapeDtypeStruct((m, d), F32),
        compiler_params=_cparams(2),
        name="conv_ffn_final" if final_norm else "conv_ffn",
    )(_layer_arg(layer), x, x, x, mods, g.reshape(g.shape[0], 1, d), w_up, w_up, conv_w, conv_w,
      conv_b.reshape(conv_b.shape[0], 1, -1), conv_b.reshape(conv_b.shape[0], 1, -1), w_down,
      final_gain.reshape(1, d))


def kernel(x_prompt, x_sample, cache_k, cache_v, c, c_ctx, w_mod, b_mod, norm1_g, w_in, rpb, w_pa, w_pb,
           w_out, norm2_g, w_up, conv_w, conv_b, w_down, norm_f_g):
    n_ctx, seq, d = x_prompt.shape
    n_lat, n_tok, _ = x_sample.shape
    n_layers = w_mod.shape[0]
    assert n_lat < MOD_ROWS and n_layers >= 1

    c_all = jnp.zeros((MOD_ROWS, d), F32).at[:n_lat].set(c).at[n_lat].set(c_ctx)
    mods = _modulation(c_all, w_mod, b_mod).reshape(n_layers, MOD_ROWS, N_MOD, d)
    w_in, w_pa, w_pb, w_out, w_up, w_down = (w.astype(BF16) for w in (w_in, w_pa, w_pb, w_out, w_up, w_down))

    xp = x_prompt.reshape(n_ctx * seq, d)
    xs = x_sample.reshape(n_lat * n_tok, d)
    ctx_mod = dict(rows_per_mod=n_ctx * seq, mod_row0=n_lat)
    lat_mod = dict(rows_per_mod=n_tok, mod_row0=0)
    caches = (jnp.zeros((n_ctx, n_layers, N_HEADS, seq, HEAD_DIM), F32),) * 2
    for l in range(n_layers):
        last = l == n_layers - 1
        ffn_w = (mods, norm2_g, w_up, conv_w, conv_b, w_down, norm_f_g)
        proj, *caches = _in_proj(l, xp, mods, norm1_g, w_in, caches=caches, **ctx_mod)
        a = _ctx_attention(proj, n_ctx, seq)
        fo = _fourier(proj, n_ctx, seq)
        xp = _merge_out(l, xp, a, fo, proj, mods, w_pa, w_pb, w_out, **ctx_mod)
        xp = _conv_ffn(l, xp, *ffn_w, seq=seq, final_norm=last, **ctx_mod)

        (proj,) = _in_proj(l, xs, mods, norm1_g, w_in, **lat_mod)
        a = _nbr_attention(l, proj, cache_k, cache_v, rpb, n_lat, n_tok)
        fo = _fourier(proj, n_lat, n_tok)
        xs = _merge_out(l, xs, a, fo, proj, mods, w_pa, w_pb, w_out, **lat_mod)
        xs = _conv_ffn(l, xs, *ffn_w, seq=n_tok, final_norm=last, **lat_mod)

    return (xp.reshape(n_ctx, seq, d), xs.reshape(n_lat, n_tok, d), caches[0], caches[1])
```

```python
import functools
import math

import numpy as np
import jax
import jax.numpy as jnp
from jax import lax
from jax.experimental import pallas as pl
from jax.experimental.pallas import tpu as pltpu

F32 = jnp.float32
BF16 = jnp.bfloat16

RMS_EPS = 1e-6
LOG2_E = math.log2(math.e)
N_MOD = 6
N_HEADS = 8
HEAD_DIM = 128
GRID_W = 64
NA_KH = 8
NA_KW = 16
F_GROUPS = 4
NBR_ROWS_PER_GROUP = 4
MOD_ROWS = 8
V7X_LANES = 128
V7X_BF16_SUBLANES = 16
V7X_VMEM_BYTES = 64 * 1024 * 1024
VMEM_LIMIT = V7X_VMEM_BYTES - 8 * 1024 * 1024

TM_IN, TN_IN, TN_IN_CTX = 1024, 1024, 512
TN_MOD = 1024
TM_MERGE, TC_MERGE = 512, 512
TM_FFN, TF_FFN, TC_FFN = 512, 512, 256
TM_FOURIER = 256
NORM_ROW_CHUNK = 16
NORM_UNROLL = 8


def _cparams(n_axes):
    return pltpu.CompilerParams(dimension_semantics=("arbitrary",) * n_axes,
                                vmem_limit_bytes=VMEM_LIMIT)


def _layer_arg(layer):
    return jnp.full((1,), layer, jnp.int32)


def _norm_mod_store(dst_ref, dst_row0, src_ref, g, scale, shift, keep=None, src_row0=0, n=None):
    n = src_ref.shape[0] if n is None else n
    chunk = min(NORM_ROW_CHUNK, n)
    assert n % chunk == 0 and src_row0 % chunk == 0 and dst_row0 % chunk == 0
    gs = g * (1.0 + scale)

    def rows(r, carry):
        r0 = pl.multiple_of(r * chunk, chunk)
        x = src_ref[pl.ds(src_row0 + r0, chunk), :]
        inv = lax.rsqrt(jnp.mean(x * x, axis=-1, keepdims=True) + RMS_EPS)
        h = (x * inv) * gs + shift
        if keep is not None:
            h = jnp.where(keep, h, 0.0)
        dst_ref[pl.ds(dst_row0 + r0, chunk), :] = h.astype(BF16)
        return carry

    lax.fori_loop(0, n // chunk, rows, 0, unroll=min(NORM_UNROLL, n // chunk))


def _mod_kernel(c_ref, w_ref, b_ref, o_ref):
    cv = c_ref[...]
    s = (cv * jax.nn.sigmoid(cv)).astype(BF16)
    o_ref[...] = jnp.dot(s, w_ref[...].astype(BF16), preferred_element_type=F32) + b_ref[...]


def _modulation(c_all, w_mod, b_mod):
    n_layers, d, nd = w_mod.shape
    tn = min(TN_MOD, nd)
    assert nd % tn == 0
    return pl.pallas_call(
        _mod_kernel,
        grid=(n_layers, nd // tn),
        in_specs=[pl.BlockSpec((MOD_ROWS, d), lambda l, j: (0, 0)),
                  pl.BlockSpec((None, d, tn), lambda l, j: (l, 0, j)),
                  pl.BlockSpec((None, 1, tn), lambda l, j: (l, 0, j))],
        out_specs=pl.BlockSpec((None, MOD_ROWS, tn), lambda l, j: (l, 0, j)),
        out_shape=jax.ShapeDtypeStruct((n_layers, MOD_ROWS, nd), F32),
        compiler_params=_cparams(2),
        name="modulation",
    )(c_all, w_mod, b_mod.reshape(n_layers, 1, nd))


def _in_proj_kernel(l_ref, x_ref, mod_ref, g_ref, w_ref, *rest, kv_tiles):
    h_ref = rest[-1]
    j = pl.program_id(1)

    @pl.when(j == 0)
    def _():
        _norm_mod_store(h_ref, 0, x_ref, g_ref[...], mod_ref[1:2, :], mod_ref[0:1, :])

    res = jnp.dot(h_ref[...], w_ref[...].astype(BF16), preferred_element_type=F32)
    if kv_tiles is None:
        o_ref = rest[0]
        o_ref[...] = res.astype(o_ref.dtype)
    else:
        o_ref, kc_ref, vc_ref = rest[-4], rest[-3], rest[-2]
        o_ref[...] = res.astype(o_ref.dtype)
        k0, v0, v1 = kv_tiles
        nb, nh, s, dh = kc_ref.shape

        def scatter(dst_ref):
            for bb in range(nb):
                for hh in range(nh):
                    dst_ref[bb, hh] = res[bb * s:(bb + 1) * s, hh * dh:(hh + 1) * dh]

        @pl.when((j >= k0) & (j < v0))
        def _():
            scatter(kc_ref)

        @pl.when((j >= v0) & (j < v1))
        def _():
            scatter(vc_ref)


def _in_proj(layer, x, mods, g, w, *, rows_per_mod, mod_row0, caches=None):
    m, d = x.shape
    n_out = w.shape[2]
    tm, tn = min(TM_IN, m), min(TN_IN if caches is None else TN_IN_CTX, n_out)
    assert m % tm == 0 and n_out % tn == 0 and (rows_per_mod % tm == 0 or tm % rows_per_mod == 0)
    attn = N_HEADS * HEAD_DIM
    in_specs = [pl.BlockSpec((tm, d), lambda i, j, l: (i, 0)),
                pl.BlockSpec((None, None, N_MOD, d),
                             lambda i, j, l: (l[0], mod_row0 + (i * tm) // rows_per_mod, 0, 0)),
                pl.BlockSpec((None, 1, d), lambda i, j, l: (l[0], 0, 0)),
                pl.BlockSpec((None, d, tn), lambda i, j, l: (l[0], 0, j))]
    out_specs = [pl.BlockSpec((tm, tn), lambda i, j, l: (i, j))]
    out_shape = [jax.ShapeDtypeStruct((m, n_out), BF16)]
    args = [_layer_arg(layer), x, mods, g.reshape(g.shape[0], 1, d), w]
    kv_tiles, aliases = None, {}
    if caches is not None:
        seq = caches[0].shape[3]
        assert tm % seq == 0 and tn % HEAD_DIM == 0 and attn % tn == 0
        k0, v0, v1 = attn // tn, 2 * attn // tn, 3 * attn // tn
        kv_tiles = (k0, v0, v1)
        blk = (tm // seq, None, tn // HEAD_DIM, seq, HEAD_DIM)
        out_specs += [pl.BlockSpec(blk, lambda i, j, l: (i, l[0], jnp.clip(j - k0, 0, v0 - k0 - 1), 0, 0)),
                      pl.BlockSpec(blk, lambda i, j, l: (i, l[0], jnp.clip(j - v0, 0, v1 - v0 - 1), 0, 0))]
        out_shape += [jax.ShapeDtypeStruct(c.shape, c.dtype) for c in caches]
        in_specs += [pl.BlockSpec(memory_space=pl.ANY)] * 2
        aliases = {len(args): 1, len(args) + 1: 2}
        args += list(caches)
    return pl.pallas_call(
        functools.partial(_in_proj_kernel, kv_tiles=kv_tiles),
        grid_spec=pltpu.PrefetchScalarGridSpec(
            num_scalar_prefetch=1, grid=(m // tm, n_out // tn), in_specs=in_specs, out_specs=out_specs,
            scratch_shapes=[pltpu.VMEM((tm, d), BF16)]),
        out_shape=out_shape,
        input_output_aliases=aliases,
        compiler_params=_cparams(2),
        name="in_proj_ctx" if caches is not None else "in_proj_lat",
    )(*args)


def _ctx_attn_kernel(q_ref, k_ref, v_ref, o_ref, *, scale):
    for h in range(N_HEADS):
        sl = slice(h * HEAD_DIM, (h + 1) * HEAD_DIM)
        s = lax.dot_general(q_ref[:, sl], k_ref[:, sl], (((1,), (1,)), ((), ())),
                            preferred_element_type=F32) * scale
        e = jnp.exp(s - jnp.max(s, axis=-1, keepdims=True))
        p = e * (1.0 / jnp.sum(e, axis=-1, keepdims=True))
        o_ref[:, sl] = jnp.dot(p.astype(BF16), v_ref[:, sl], preferred_element_type=F32).astype(o_ref.dtype)


def _ctx_attention(proj, n_seq, seq):
    attn = N_HEADS * HEAD_DIM
    return pl.pallas_call(
        functools.partial(_ctx_attn_kernel, scale=HEAD_DIM ** -0.5),
        grid=(n_seq,),
        in_specs=[pl.BlockSpec((seq, attn), lambda b: (b, 0)),
                  pl.BlockSpec((seq, attn), lambda b: (b, 1)),
                  pl.BlockSpec((seq, attn), lambda b: (b, 2))],
        out_specs=pl.BlockSpec((seq, attn), lambda b: (b, 0)),
        out_shape=jax.ShapeDtypeStruct((n_seq * seq, attn), BF16),
        compiler_params=_cparams(1),
        name="ctx_attention",
    )(proj, proj, proj)


def _nbr_layout(rows):
    r_grp, win = NBR_ROWS_PER_GROUP, NA_KH + NBR_ROWS_PER_GROUP
    types, type_of_group = [], []
    for g in range(rows // r_grp):
        w0 = min(max(r_grp * g - NA_KH // 2, 0), rows - win)
        desc = []
        for qi in range(r_grp):
            r = r_grp * g + qi
            r0 = min(max(r - NA_KH // 2, 0), rows - NA_KH)
            desc.append(tuple((w0 + kj - r) if r0 <= w0 + kj < r0 + NA_KH else None for kj in range(win)))
        desc = tuple(desc)
        if desc not in types:
            types.append(desc)
        type_of_group.append(types.index(desc))
    pairs = sorted({(d[qi][2 * m], d[qi][2 * m + 1]) for d in types for qi in range(r_grp)
                    for m in range(win // 2)} - {(None, None)},
                   key=lambda p: tuple(-99 if v is None else v for v in p))
    return types, type_of_group, pairs


def _nbr_attn_kernel(l_ref, rpb_ref, q_ref, k_ref, v_ref, kc_ref, vc_ref, o_ref, bias_ref, pair_ref, *,
                     rows, layout, scale):
    types, type_of_group, pairs = layout
    w = GRID_W
    r_grp, win = NBR_ROWS_PER_GROUP, NA_KH + NBR_ROWS_PER_GROUP
    gq, wk = r_grp * w, win * w
    n_dr, n_dc = 2 * NA_KH - 1, 2 * NA_KW - 1
    neg_inf = float("-inf")
    h = pl.program_id(0)
    layer = l_ref[0]

    @pl.when(pl.program_id(1) == 0)
    def _build_bias_tables():
        cq = lax.broadcasted_iota(jnp.int32, (w, V7X_LANES), 0)
        lane = lax.broadcasted_iota(jnp.int32, (w, V7X_LANES), 1)
        ck = lane & (w - 1)
        c0 = jnp.clip(cq - NA_KW // 2, 0, w - NA_KW)
        in_win = (ck >= c0) & (ck < c0 + NA_KW)
        dc_idx = jnp.where(in_win, jnp.clip(ck - cq, -(NA_KW - 1), NA_KW - 1) + (NA_KW - 1), -1)
        left = lane < w
        for p in range(len(pairs)):
            pair_ref[p] = jnp.full((w, V7X_LANES), neg_inf, F32)

        def per_dc(dc, carry):
            hit = dc_idx == dc
            for p, (da, db) in enumerate(pairs):
                va = neg_inf if da is None else rpb_ref[layer, (h * n_dr + da + NA_KH - 1) * n_dc + dc]
                vb = neg_inf if db is None else rpb_ref[layer, (h * n_dr + db + NA_KH - 1) * n_dc + dc]
                pair_ref[p] = jnp.where(hit, jnp.where(left, va, vb) * LOG2_E, pair_ref[p])
            return carry

        lax.fori_loop(0, n_dc, per_dc, 0)
        for t, desc in enumerate(types):
            for qi in range(r_grp):
                for m in range(win // 2):
                    pr = (desc[qi][2 * m], desc[qi][2 * m + 1])
                    tile = (jnp.full((w, V7X_LANES), neg_inf, F32) if pr == (None, None)
                            else pair_ref[pairs.index(pr)])
                    bias_ref[t, qi * w:(qi + 1) * w, m * V7X_LANES:(m + 1) * V7X_LANES] = tile

    kc = kc_ref[...].astype(BF16)
    vc = vc_ref[...].astype(BF16)
    nt = (((1,), (1,)), ((), ()))
    scale2 = scale * LOG2_E
    default_type = max(set(type_of_group), key=type_of_group.count)

    def scores(g):
        w0 = jnp.clip(r_grp * g - NA_KH // 2, 0, rows - win)
        q0 = pl.multiple_of(g * gq, gq)
        k0 = pl.multiple_of(w0 * w, w)
        t = jnp.int32(default_type)
        for gi, ti in enumerate(type_of_group):
            if ti != default_type:
                t = jnp.where(g == gi, ti, t)
        q = q_ref[pl.ds(q0, gq), :]
        s = lax.dot_general(q, k_ref[pl.ds(k0, wk), :], nt, preferred_element_type=F32) * scale2 + bias_ref[t]
        sc = lax.dot_general(q, kc, nt, preferred_element_type=F32) * scale2
        return q0, k0, s, sc

    def attend(q0, k0, s, sc):
        mx = jnp.maximum(jnp.max(s, axis=-1, keepdims=True), jnp.max(sc, axis=-1, keepdims=True))
        e = jnp.exp2(s - mx)
        ec = jnp.exp2(sc - mx)
        inv = 1.0 / (jnp.sum(e, axis=-1, keepdims=True) + jnp.sum(ec, axis=-1, keepdims=True))
        o = (jnp.dot(e.astype(BF16), v_ref[pl.ds(k0, wk), :], preferred_element_type=F32)
             + jnp.dot(ec.astype(BF16), vc, preferred_element_type=F32))
        o_ref[pl.ds(q0, gq), :] = (o * inv).astype(o_ref.dtype)

    def per_pair(i, carry):
        first, second = scores(2 * i), scores(2 * i + 1)
        attend(*first)
        attend(*second)
        return carry

    lax.fori_loop(0, rows // (2 * r_grp), per_pair, 0)


def _nbr_attention(layer, proj, cache_k, cache_v, rpb, n_batch, n_tok):
    assert 2 * GRID_W == V7X_LANES and n_tok % GRID_W == 0
    rows = n_tok // GRID_W
    assert rows % (2 * NBR_ROWS_PER_GROUP) == 0 and rows >= NA_KH + NBR_ROWS_PER_GROUP and GRID_W >= NA_KW
    layout = _nbr_layout(rows)
    r_grp, win = NBR_ROWS_PER_GROUP, NA_KH + NBR_ROWS_PER_GROUP
    t_ctx = cache_k.shape[3]
    blk = lambda col0: pl.BlockSpec((n_tok, HEAD_DIM), lambda h, b, l: (b, col0 + h))
    ctx = pl.BlockSpec((None, None, None, t_ctx, HEAD_DIM), lambda h, b, l: (b, l[0], h, 0, 0))
    return pl.pallas_call(
        functools.partial(_nbr_attn_kernel, rows=rows, layout=layout, scale=HEAD_DIM ** -0.5),
        grid_spec=pltpu.PrefetchScalarGridSpec(
            num_scalar_prefetch=1, grid=(N_HEADS, n_batch),
            in_specs=[pl.BlockSpec(memory_space=pltpu.SMEM),
                      blk(0), blk(N_HEADS), blk(2 * N_HEADS), ctx, ctx],
            out_specs=pl.BlockSpec((n_tok, HEAD_DIM), lambda h, b, l: (b, h)),
            scratch_shapes=[pltpu.VMEM((len(layout[0]), r_grp * GRID_W, win * GRID_W), F32),
                            pltpu.VMEM((len(layout[2]), GRID_W, V7X_LANES), F32)]),
        out_shape=jax.ShapeDtypeStruct((n_batch * n_tok, N_HEADS * HEAD_DIM), BF16),
        compiler_params=_cparams(2),
        name="nbr_attention",
    )(_layer_arg(layer), rpb.reshape(rpb.shape[0], -1), proj, proj, proj, cache_k, cache_v)


@functools.lru_cache(maxsize=None)
def _dft_tables(n):
    jk = (np.arange(n, dtype=np.int64)[:, None] * np.arange(n, dtype=np.int64)[None, :]) % n
    ang = 2.0 * np.pi * jk.astype(np.float64) / n
    scale = 1.0 / math.sqrt(n)
    return ((np.cos(ang) * scale).astype(BF16), (np.sin(ang) * scale).astype(BF16))


@functools.lru_cache(maxsize=None)
def _mirror_matrix(t):
    r = np.zeros((t, t), np.float32)
    q = np.arange(1, t)
    r[q, t - q] = 1.0
    return r.astype(BF16)


def _fourier_kernel(f_ref, cc_ref, sc_ref, rev_ref, wc_ref, ws_ref, o_ref, zc_ref, zs_ref, edge_ref):
    n, fd = f_ref.shape
    gd = fd // F_GROUPS
    half = n // 2
    tile = rev_ref.shape[0]
    n_tiles = half // tile

    @pl.when(pl.program_id(1) == 0)
    def _channel_stage():
        for t in range(n_tiles):
            rows = slice(t * tile, (t + 1) * tile)
            lo = f_ref[rows, :].astype(F32)
            src = (2 * n_tiles - 1 - t) * tile
            mir = jnp.dot(rev_ref[...], f_ref[src:src + tile, :], preferred_element_type=F32)
            if t > 0:
                first = lax.broadcasted_iota(jnp.int32, (tile, 1), 0) == 0
                mir = jnp.where(first, f_ref[src + tile:src + tile + 1, :].astype(F32), mir)
            x_even = (lo + mir).astype(BF16)
            x_odd = (lo - mir).astype(BF16)
            for g in range(F_GROUPS):
                sl = slice(g * gd, (g + 1) * gd)
                zc_ref[rows, sl] = jnp.dot(x_even[:, sl], cc_ref[...], preferred_element_type=F32).astype(BF16)
                zs_ref[rows, sl] = jnp.dot(x_odd[:, sl], sc_ref[...], preferred_element_type=F32).astype(BF16)
        sign = 1.0 - 2.0 * (lax.broadcasted_iota(jnp.int32, (8, 1), 0) & 1).astype(F32)
        x_half = f_ref[half:half + V7X_BF16_SUBLANES, :]
        for g in range(F_GROUPS):
            sl = slice(g * gd, (g + 1) * gd)
            z_half = jnp.dot(x_half[:, sl], cc_ref[...], preferred_element_type=F32)[0:1, :]
            edge_ref[:, sl] = sign * (z_half * (1.0 / math.sqrt(n)))

    y = (jnp.dot(wc_ref[...], zc_ref[...], preferred_element_type=F32)
         + jnp.dot(ws_ref[...], zs_ref[...], preferred_element_type=F32))
    tm = y.shape[0]
    y = y.reshape(tm // 8, 8, fd) + edge_ref[...][None]
    o_ref[...] = y.reshape(tm, fd).astype(o_ref.dtype)


def _fourier(proj, n_seq, seq):
    attn = N_HEADS * HEAD_DIM
    fd = attn
    gd = fd // F_GROUPS
    cc, sc = _dft_tables(gd)
    cp, sp = _dft_tables(seq)
    half = seq // 2
    tm, tile = min(TM_FOURIER, seq), min(TM_FOURIER, half)
    assert seq % tm == 0 and half % tile == 0 and tile % V7X_BF16_SUBLANES == 0 and proj.shape[1] // fd > 3
    return pl.pallas_call(
        _fourier_kernel,
        grid=(n_seq, seq // tm),
        in_specs=[pl.BlockSpec((seq, fd), lambda b, i: (b, 3)),
                  pl.BlockSpec((gd, gd), lambda b, i: (0, 0)),
                  pl.BlockSpec((gd, gd), lambda b, i: (0, 0)),
                  pl.BlockSpec((tile, tile), lambda b, i: (0, 0)),
                  pl.BlockSpec((tm, half), lambda b, i: (i, 0)),
                  pl.BlockSpec((tm, half), lambda b, i: (i, 0))],
        out_specs=pl.BlockSpec((tm, fd), lambda b, i: (b * (seq // tm) + i, 0)),
        out_shape=jax.ShapeDtypeStruct((n_seq * seq, fd), BF16),
        scratch_shapes=[pltpu.VMEM((half, fd), BF16), pltpu.VMEM((half, fd), BF16), pltpu.VMEM((8, fd), F32)],
        compiler_params=_cparams(2),
        name="fourier_%d" % seq,
    )(proj, jnp.asarray(cc), jnp.asarray(sc), jnp.asarray(_mirror_matrix(tile)),
      jnp.asarray(cp[:, :half]), jnp.asarray(-sp[:, :half]))


def _merge_out_kernel(l_ref, x_ref, a_ref, fo_ref, ga_ref, gb_ref, mod_ref, wpa_ref, wpb_ref, wout_ref,
                      o_ref, y_ref, *, tc):
    d = x_ref.shape[1]
    for c in range(d // tc):
        sl = slice(c * tc, (c + 1) * tc)
        ya = jnp.dot(a_ref[...], wpa_ref[:, sl], preferred_element_type=F32)
        yb = jnp.dot(fo_ref[...], wpb_ref[:, sl], preferred_element_type=F32)
        y = (jax.nn.sigmoid(ga_ref[:, sl].astype(F32)) * ya
             + jax.nn.sigmoid(gb_ref[:, sl].astype(F32)) * yb)
        y_ref[:, sl] = y.astype(BF16)
    for c in range(d // tc):
        sl = slice(c * tc, (c + 1) * tc)
        o = jnp.dot(y_ref[...], wout_ref[:, sl], preferred_element_type=F32)
        o_ref[:, sl] = x_ref[:, sl] + mod_ref[2:3, sl] * o


def _merge_out(layer, x, a, fo, proj, mods, w_pa, w_pb, w_out, *, rows_per_mod, mod_row0):
    m, d = x.shape
    attn = a.shape[1]
    tm = min(TM_MERGE, m)
    assert m % tm == 0 and (3 * attn + fo.shape[1]) % d == 0 and rows_per_mod % tm == 0
    gate_blk0 = (3 * attn + fo.shape[1]) // d
    whole = lambda arr: pl.BlockSpec((None,) + arr.shape[1:], lambda i, l: (l[0], 0, 0),
                                     pipeline_mode=pl.Buffered(1))
    return pl.pallas_call(
        functools.partial(_merge_out_kernel, tc=min(TC_MERGE, d)),
        grid_spec=pltpu.PrefetchScalarGridSpec(
            num_scalar_prefetch=1, grid=(m // tm,),
            in_specs=[pl.BlockSpec((tm, d), lambda i, l: (i, 0)),
                      pl.BlockSpec((tm, attn), lambda i, l: (i, 0)),
                      pl.BlockSpec((tm, fo.shape[1]), lambda i, l: (i, 0)),
                      pl.BlockSpec((tm, d), lambda i, l: (i, gate_blk0)),
                      pl.BlockSpec((tm, d), lambda i, l: (i, gate_blk0 + 1)),
                      pl.BlockSpec((None, None, N_MOD, d),
                                   lambda i, l: (l[0], mod_row0 + (i * tm) // rows_per_mod, 0, 0)),
                      whole(w_pa), whole(w_pb), whole(w_out)],
            out_specs=pl.BlockSpec((tm, d), lambda i, l: (i, 0)),
            scratch_shapes=[pltpu.VMEM((tm, d), BF16)]),
        out_shape=jax.ShapeDtypeStruct((m, d), F32),
        compiler_params=_cparams(1),
        name="merge_out",
    )(_layer_arg(layer), x, a, fo, proj, proj, mods, w_pa, w_pb, w_out)


def _ffn_kernel(l_ref, x_ref, xp_ref, xn_ref, mod_ref, g_ref, wa_ref, wg_ref, cwa_ref, cwg_ref, cba_ref,
                cbg_ref, wd_ref, gf_ref, o_ref, h_ref, u_ref, *maybe_acc, seq, tc, final_norm):
    tm = x_ref.shape[0]
    halo = xp_ref.shape[0]
    tf = wa_ref.shape[1]
    seg = min(seq, tm)
    n_seg, stride = tm // seg, seg + halo
    acc_ref = maybe_acc[0] if maybe_acc else o_ref
    n_rows = acc_ref.shape[0]
    i, f = pl.program_id(0), pl.program_id(1)

    @pl.when(f == 0)
    def _():
        g, scale, shift = g_ref[...], mod_ref[4:5, :], mod_ref[3:4, :]
        tok0 = i * tm
        _norm_mod_store(h_ref, 0, xp_ref, g, scale, shift, keep=(tok0 & (seq - 1)) != 0)
        for s in range(n_seg):
            _norm_mod_store(h_ref, halo + s * stride, x_ref, g, scale, shift, src_row0=s * seg, n=seg)
            if s < n_seg - 1:
                h_ref[(s + 1) * stride:(s + 1) * stride + halo, :] = jnp.zeros((halo, h_ref.shape[1]), BF16)
        _norm_mod_store(h_ref, n_seg * stride, xn_ref, g, scale, shift, keep=((tok0 + tm) & (seq - 1)) != 0)
        acc_ref[...] = jnp.zeros_like(acc_ref)

    h = h_ref[...]

    def conv(c, idx, cw_ref, cb_ref, cs):
        return (u_ref[c, idx, pl.ds(halo - 1, n_rows), :] * cw_ref[0:1, cs]
                + u_ref[c, idx, pl.ds(halo, n_rows), :] * cw_ref[1:2, cs]
                + u_ref[c, idx, pl.ds(halo + 1, n_rows), :] * cw_ref[2:3, cs] + cb_ref[:, cs])

    n_chunks = tf // tc
    for c in range(n_chunks):
        cs = slice(c * tc, (c + 1) * tc)
        u_ref[c, 0] = jnp.dot(h, wa_ref[:, cs], preferred_element_type=F32)
        u_ref[c, 1] = jnp.dot(h, wg_ref[:, cs], preferred_element_type=F32)
    for c in range(n_chunks):
        cs = slice(c * tc, (c + 1) * tc)
        ca = conv(c, 0, cwa_ref, cba_ref, cs)
        cg = conv(c, 1, cwg_ref, cbg_ref, cs)
        act = ((cg * jax.nn.sigmoid(cg)) * ca).astype(BF16)
        acc_ref[...] += jnp.dot(act, wd_ref[cs, :], preferred_element_type=F32)

    @pl.when(f == pl.num_programs(1) - 1)
    def _():
        for s in range(n_seg):
            rows = slice(s * seg, (s + 1) * seg)
            y = x_ref[rows, :] + mod_ref[5:6, :] * acc_ref[s * stride:s * stride + seg, :]
            if final_norm:
                y = (y * lax.rsqrt(jnp.mean(y * y, axis=-1, keepdims=True) + RMS_EPS)) * gf_ref[...]
            o_ref[rows, :] = y


def _conv_ffn(layer, x, mods, g, w_up, conv_w, conv_b, w_down, final_gain, *, seq, rows_per_mod, mod_row0,
              final_norm):
    m, d = x.shape
    ffn = w_down.shape[1]
    tm, tf = min(TM_FFN, m), min(TF_FFN, ffn)
    tc = min(TC_FFN, tf)
    halo = V7X_BF16_SUBLANES
    assert seq & (seq - 1) == 0 and (seq % tm == 0 or tm % seq == 0) and m % seq == 0
    assert m % tm == 0 and ffn % tf == 0 and tf % tc == 0 and min(seq, tm) % halo == 0 and rows_per_mod % tm == 0
    nf, nhb = ffn // tf, m // halo
    n_seg = tm // min(seq, tm)
    work_rows = tm + (n_seg + 1) * halo
    acc = [pltpu.VMEM((work_rows - 2 * halo, d), F32)] if n_seg > 1 else []
    mat = lambda rows, cols, imap: pl.BlockSpec((None, rows, cols), imap)
    return pl.pallas_call(
        functools.partial(_ffn_kernel, seq=seq, tc=tc, final_norm=final_norm),
        grid_spec=pltpu.PrefetchScalarGridSpec(
            num_scalar_prefetch=1, grid=(m // tm, nf),
            in_specs=[pl.BlockSpec((tm, d), lambda i, f, l: (i, 0)),
                      pl.BlockSpec((halo, d), lambda i, f, l: (jnp.maximum(i * (tm // halo) - 1, 0), 0)),
                      pl.BlockSpec((halo, d),
                                   lambda i, f, l: (jnp.minimum((i + 1) * (tm // halo), nhb - 1), 0)),
                      pl.BlockSpec((None, None, N_MOD, d),
                                   lambda i, f, l: (l[0], mod_row0 + (i * tm) // rows_per_mod, 0, 0)),
                      mat(1, d, lambda i, f, l: (l[0], 0, 0)),
                      mat(d, tf, lambda i, f, l: (l[0], 0, f)),
                      mat(d, tf, lambda i, f, l: (l[0], 0, nf + f)),
                      mat(3, tf, lambda i, f, l: (l[0], 0, f)),
                      mat(3, tf, lambda i, f, l: (l[0], 0, nf + f)),
                      mat(1, tf, lambda i, f, l: (l[0], 0, f)),
                      mat(1, tf, lambda i, f, l: (l[0], 0, nf + f)),
                      mat(tf, d, lambda i, f, l: (l[0], f, 0)),
                      pl.BlockSpec((1, d), lambda i, f, l: (0, 0))],
            out_specs=pl.BlockSpec((tm, d), lambda i, f, l: (i, 0)),
            scratch_shapes=[pltpu.VMEM((work_rows, d), BF16),
                            pltpu.VMEM((tf // tc, 2, work_rows, tc), F32)] + acc),
        out_shape=jax.ShapeDtypeStruct((m, d), F32),
        compiler_params=_cparams(2),
        name="conv_ffn_final" if final_norm else "conv_ffn",
    )(_layer_arg(layer), x, x, x, mods, g.reshape(g.shape[0], 1, d), w_up, w_up, conv_w, conv_w,
      conv_b.reshape(conv_b.shape[0], 1, -1), conv_b.reshape(conv_b.shape[0], 1, -1), w_down,
      final_gain.reshape(1, d))


def kernel(x_prompt, x_sample, cache_k, cache_v, c, c_ctx, w_mod, b_mod, norm1_g, w_in, rpb, w_pa, w_pb,
           w_out, norm2_g, w_up, conv_w, conv_b, w_down, norm_f_g):
    n_ctx, seq, d = x_prompt.shape
    n_lat, n_tok, _ = x_sample.shape
    n_layers = w_mod.shape[0]
    assert n_lat < MOD_ROWS and n_layers >= 1

    c_all = jnp.zeros((MOD_ROWS, d), F32).at[:n_lat].set(c).at[n_lat].set(c_ctx)
    mods = _modulation(c_all, w_mod, b_mod).reshape(n_layers, MOD_ROWS, N_MOD, d)
    w_pa, w_pb, w_out, w_up, w_down = (w.astype(BF16) for w in (w_pa, w_pb, w_out, w_up, w_down))

    xp = x_prompt.reshape(n_ctx * seq, d)
    xs = x_sample.reshape(n_lat * n_tok, d)
    ctx_mod = dict(rows_per_mod=n_ctx * seq, mod_row0=n_lat)
    lat_mod = dict(rows_per_mod=n_tok, mod_row0=0)
    caches = (jnp.zeros((n_ctx, n_layers, N_HEADS, seq, HEAD_DIM), F32),) * 2
    for l in range(n_layers):
        last = l == n_layers - 1
        ffn_w = (mods, norm2_g, w_up, conv_w, conv_b, w_down, norm_f_g)
        proj, *caches = _in_proj(l, xp, mods, norm1_g, w_in, caches=caches, **ctx_mod)
        a = _ctx_attention(proj, n_ctx, seq)
        fo = _fourier(proj, n_ctx, seq)
        xp = _merge_out(l, xp, a, fo, proj, mods, w_pa, w_pb, w_out, **ctx_mod)
        xp = _conv_ffn(l, xp, *ffn_w, seq=seq, final_norm=last, **ctx_mod)

        (proj,) = _in_proj(l, xs, mods, norm1_g, w_in, **lat_mod)
        a = _nbr_attention(l, proj, cache_k, cache_v, rpb, n_lat, n_tok)
        fo = _fourier(proj, n_lat, n_tok)
        xs = _merge_out(l, xs, a, fo, proj, mods, w_pa, w_pb, w_out, **lat_mod)
        xs = _conv_ffn(l, xs, *ffn_w, seq=n_tok, final_norm=last, **lat_mod)

    return (xp.reshape(n_ctx, seq, d), xs.reshape(n_lat, n_tok, d), caches[0], caches[1])
```

```python
import functools
import math

import numpy as np
import jax
import jax.numpy as jnp
from jax import lax
from jax.experimental import pallas as pl
from jax.experimental.pallas import tpu as pltpu

F32 = jnp.float32
BF16 = jnp.bfloat16

RMS_EPS = 1e-6
LOG2_E = math.log2(math.e)
N_MOD = 6
N_HEADS = 8
HEAD_DIM = 128
GRID_W = 64
NA_KH = 8
NA_KW = 16
F_GROUPS = 4
NBR_ROWS_PER_GROUP = 4
MOD_ROWS = 8
V7X_LANES = 128
V7X_BF16_SUBLANES = 16
V7X_VMEM_BYTES = 64 * 1024 * 1024
VMEM_LIMIT = V7X_VMEM_BYTES - 8 * 1024 * 1024

TM_IN, TN_IN = 1024, 1024
TN_MOD = 1024
TM_MERGE, TC_MERGE = 512, 512
TM_FFN, TF_FFN, TC_FFN = 512, 512, 256
TM_FOURIER = 256
NORM_ROW_CHUNK = 16
NORM_UNROLL = 8


def _cparams(n_axes):
    return pltpu.CompilerParams(dimension_semantics=("arbitrary",) * n_axes,
                                vmem_limit_bytes=VMEM_LIMIT)


def _layer_arg(layer):
    return jnp.full((1,), layer, jnp.int32)


def _norm_mod_store(dst_ref, dst_row0, src_ref, g, scale, shift, keep=None, src_row0=0, n=None):
    n = src_ref.shape[0] if n is None else n
    chunk = min(NORM_ROW_CHUNK, n)
    assert n % chunk == 0 and src_row0 % chunk == 0 and dst_row0 % chunk == 0
    gs = g * (1.0 + scale)

    def rows(r, carry):
        r0 = pl.multiple_of(r * chunk, chunk)
        x = src_ref[pl.ds(src_row0 + r0, chunk), :]
        inv = lax.rsqrt(jnp.mean(x * x, axis=-1, keepdims=True) + RMS_EPS)
        h = (x * inv) * gs + shift
        if keep is not None:
            h = jnp.where(keep, h, 0.0)
        dst_ref[pl.ds(dst_row0 + r0, chunk), :] = h.astype(BF16)
        return carry

    lax.fori_loop(0, n // chunk, rows, 0, unroll=min(NORM_UNROLL, n // chunk))


def _mod_kernel(c_ref, w_ref, b_ref, o_ref):
    cv = c_ref[...]
    s = (cv * jax.nn.sigmoid(cv)).astype(BF16)
    o_ref[...] = jnp.dot(s, w_ref[...].astype(BF16), preferred_element_type=F32) + b_ref[...]


def _modulation(c_all, w_mod, b_mod):
    n_layers, d, nd = w_mod.shape
    tn = min(TN_MOD, nd)
    assert nd % tn == 0
    return pl.pallas_call(
        _mod_kernel,
        grid=(n_layers, nd // tn),
        in_specs=[pl.BlockSpec((MOD_ROWS, d), lambda l, j: (0, 0)),
                  pl.BlockSpec((None, d, tn), lambda l, j: (l, 0, j)),
                  pl.BlockSpec((None, 1, tn), lambda l, j: (l, 0, j))],
        out_specs=pl.BlockSpec((None, MOD_ROWS, tn), lambda l, j: (l, 0, j)),
        out_shape=jax.ShapeDtypeStruct((n_layers, MOD_ROWS, nd), F32),
        compiler_params=_cparams(2),
        name="modulation",
    )(c_all, w_mod, b_mod.reshape(n_layers, 1, nd))


def _in_proj_kernel(l_ref, x_ref, mod_ref, g_ref, w_ref, *rest, kv_tiles):
    h_ref = rest[-1]
    j = pl.program_id(1)

    @pl.when(j == 0)
    def _():
        _norm_mod_store(h_ref, 0, x_ref, g_ref[...], mod_ref[1:2, :], mod_ref[0:1, :])

    res = jnp.dot(h_ref[...], w_ref[...], preferred_element_type=F32)
    if kv_tiles is None:
        o_ref = rest[0]
        o_ref[...] = res.astype(o_ref.dtype)
    else:
        o_ref, kc_ref, vc_ref = rest[-4], rest[-3], rest[-2]
        o_ref[...] = res.astype(o_ref.dtype)
        k0, v0, v1 = kv_tiles
        nb, nh, s, dh = kc_ref.shape

        def scatter(dst_ref):
            for bb in range(nb):
                for hh in range(nh):
                    dst_ref[bb, hh] = res[bb * s:(bb + 1) * s, hh * dh:(hh + 1) * dh]

        @pl.when((j >= k0) & (j < v0))
        def _():
            scatter(kc_ref)

        @pl.when((j >= v0) & (j < v1))
        def _():
            scatter(vc_ref)


def _in_proj(layer, x, mods, g, w, *, rows_per_mod, mod_row0, caches=None):
    m, d = x.shape
    n_out = w.shape[2]
    tm, tn = min(TM_IN, m), min(TN_IN, n_out)
    assert m % tm == 0 and n_out % tn == 0 and (rows_per_mod % tm == 0 or tm % rows_per_mod == 0)
    attn = N_HEADS * HEAD_DIM
    in_specs = [pl.BlockSpec((tm, d), lambda i, j, l: (i, 0)),
                pl.BlockSpec((None, None, N_MOD, d),
                             lambda i, j, l: (l[0], mod_row0 + (i * tm) // rows_per_mod, 0, 0)),
                pl.BlockSpec((None, 1, d), lambda i, j, l: (l[0], 0, 0)),
                pl.BlockSpec((None, d, tn), lambda i, j, l: (l[0], 0, j))]
    out_specs = [pl.BlockSpec((tm, tn), lambda i, j, l: (i, j))]
    out_shape = [jax.ShapeDtypeStruct((m, n_out), BF16)]
    args = [_layer_arg(layer), x, mods, g.reshape(g.shape[0], 1, d), w]
    kv_tiles, aliases = None, {}
    if caches is not None:
        seq = caches[0].shape[3]
        assert tm % seq == 0 and tn % HEAD_DIM == 0 and attn % tn == 0
        k0, v0, v1 = attn // tn, 2 * attn // tn, 3 * attn // tn
        kv_tiles = (k0, v0, v1)
        blk = (tm // seq, None, tn // HEAD_DIM, seq, HEAD_DIM)
        out_specs += [pl.BlockSpec(blk, lambda i, j, l: (i, l[0], jnp.clip(j - k0, 0, v0 - k0 - 1), 0, 0)),
                      pl.BlockSpec(blk, lambda i, j, l: (i, l[0], jnp.clip(j - v0, 0, v1 - v0 - 1), 0, 0))]
        out_shape += [jax.ShapeDtypeStruct(c.shape, c.dtype) for c in caches]
        in_specs += [pl.BlockSpec(memory_space=pl.ANY)] * 2
        aliases = {len(args): 1, len(args) + 1: 2}
        args += list(caches)
    return pl.pallas_call(
        functools.partial(_in_proj_kernel, kv_tiles=kv_tiles),
        grid_spec=pltpu.PrefetchScalarGridSpec(
            num_scalar_prefetch=1, grid=(m // tm, n_out // tn), in_specs=in_specs, out_specs=out_specs,
            scratch_shapes=[pltpu.VMEM((tm, d), BF16)]),
        out_shape=out_shape,
        input_output_aliases=aliases,
        compiler_params=_cparams(2),
        name="in_proj_ctx" if caches is not None else "in_proj_lat",
    )(*args)


def _ctx_attn_kernel(q_ref, k_ref, v_ref, o_ref, *, scale):
    for h in range(N_HEADS):
        sl = slice(h * HEAD_DIM, (h + 1) * HEAD_DIM)
        s = lax.dot_general(q_ref[:, sl], k_ref[:, sl], (((1,), (1,)), ((), ())),
                            preferred_element_type=F32) * scale
        e = jnp.exp(s - jnp.max(s, axis=-1, keepdims=True))
        p = e * (1.0 / jnp.sum(e, axis=-1, keepdims=True))
        o_ref[:, sl] = jnp.dot(p.astype(BF16), v_ref[:, sl], preferred_element_type=F32).astype(o_ref.dtype)


def _ctx_attention(proj, n_seq, seq):
    attn = N_HEADS * HEAD_DIM
    return pl.pallas_call(
        functools.partial(_ctx_attn_kernel, scale=HEAD_DIM ** -0.5),
        grid=(n_seq,),
        in_specs=[pl.BlockSpec((seq, attn), lambda b: (b, 0)),
                  pl.BlockSpec((seq, attn), lambda b: (b, 1)),
                  pl.BlockSpec((seq, attn), lambda b: (b, 2))],
        out_specs=pl.BlockSpec((seq, attn), lambda b: (b, 0)),
        out_shape=jax.ShapeDtypeStruct((n_seq * seq, attn), BF16),
        compiler_params=_cparams(1),
        name="ctx_attention",
    )(proj, proj, proj)


def _nbr_layout(rows):
    r_grp, win = NBR_ROWS_PER_GROUP, NA_KH + NBR_ROWS_PER_GROUP
    types, type_of_group = [], []
    for g in range(rows // r_grp):
        w0 = min(max(r_grp * g - NA_KH // 2, 0), rows - win)
        desc = []
        for qi in range(r_grp):
            r = r_grp * g + qi
            r0 = min(max(r - NA_KH // 2, 0), rows - NA_KH)
            desc.append(tuple((w0 + kj - r) if r0 <= w0 + kj < r0 + NA_KH else None for kj in range(win)))
        desc = tuple(desc)
        if desc not in types:
            types.append(desc)
        type_of_group.append(types.index(desc))
    pairs = sorted({(d[qi][2 * m], d[qi][2 * m + 1]) for d in types for qi in range(r_grp)
                    for m in range(win // 2)} - {(None, None)},
                   key=lambda p: tuple(-99 if v is None else v for v in p))
    return types, type_of_group, pairs


def _nbr_attn_kernel(l_ref, rpb_ref, q_ref, k_ref, v_ref, kc_ref, vc_ref, o_ref, bias_ref, pair_ref, *,
                     rows, layout, scale):
    types, type_of_group, pairs = layout
    w = GRID_W
    r_grp, win = NBR_ROWS_PER_GROUP, NA_KH + NBR_ROWS_PER_GROUP
    gq, wk = r_grp * w, win * w
    n_dr, n_dc = 2 * NA_KH - 1, 2 * NA_KW - 1
    neg_inf = float("-inf")
    h = pl.program_id(0)
    layer = l_ref[0]

    @pl.when(pl.program_id(1) == 0)
    def _build_bias_tables():
        cq = lax.broadcasted_iota(jnp.int32, (w, V7X_LANES), 0)
        lane = lax.broadcasted_iota(jnp.int32, (w, V7X_LANES), 1)
        ck = lane & (w - 1)
        c0 = jnp.clip(cq - NA_KW // 2, 0, w - NA_KW)
        in_win = (ck >= c0) & (ck < c0 + NA_KW)
        dc_idx = jnp.where(in_win, jnp.clip(ck - cq, -(NA_KW - 1), NA_KW - 1) + (NA_KW - 1), -1)
        left = lane < w
        for p in range(len(pairs)):
            pair_ref[p] = jnp.full((w, V7X_LANES), neg_inf, F32)

        def per_dc(dc, carry):
            hit = dc_idx == dc
            for p, (da, db) in enumerate(pairs):
                va = neg_inf if da is None else rpb_ref[layer, (h * n_dr + da + NA_KH - 1) * n_dc + dc]
                vb = neg_inf if db is None else rpb_ref[layer, (h * n_dr + db + NA_KH - 1) * n_dc + dc]
                pair_ref[p] = jnp.where(hit, jnp.where(left, va, vb) * LOG2_E, pair_ref[p])
            return carry

        lax.fori_loop(0, n_dc, per_dc, 0)
        for t, desc in enumerate(types):
            for qi in range(r_grp):
                for m in range(win // 2):
                    pr = (desc[qi][2 * m], desc[qi][2 * m + 1])
                    tile = (jnp.full((w, V7X_LANES), neg_inf, F32) if pr == (None, None)
                            else pair_ref[pairs.index(pr)])
                    bias_ref[t, qi * w:(qi + 1) * w, m * V7X_LANES:(m + 1) * V7X_LANES] = tile

    kc = kc_ref[...].astype(BF16)
    vc = vc_ref[...].astype(BF16)
    nt = (((1,), (1,)), ((), ()))
    scale2 = scale * LOG2_E
    default_type = max(set(type_of_group), key=type_of_group.count)

    def scores(g):
        w0 = jnp.clip(r_grp * g - NA_KH // 2, 0, rows - win)
        q0 = pl.multiple_of(g * gq, gq)
        k0 = pl.multiple_of(w0 * w, w)
        t = jnp.int32(default_type)
        for gi, ti in enumerate(type_of_group):
            if ti != default_type:
                t = jnp.where(g == gi, ti, t)
        q = q_ref[pl.ds(q0, gq), :]
        s = lax.dot_general(q, k_ref[pl.ds(k0, wk), :], nt, preferred_element_type=F32) * scale2 + bias_ref[t]
        sc = lax.dot_general(q, kc, nt, preferred_element_type=F32) * scale2
        return q0, k0, s, sc

    def attend(q0, k0, s, sc):
        mx = jnp.maximum(jnp.max(s, axis=-1, keepdims=True), jnp.max(sc, axis=-1, keepdims=True))
        e = jnp.exp2(s - mx)
        ec = jnp.exp2(sc - mx)
        inv = 1.0 / (jnp.sum(e, axis=-1, keepdims=True) + jnp.sum(ec, axis=-1, keepdims=True))
        o = (jnp.dot(e.astype(BF16), v_ref[pl.ds(k0, wk), :], preferred_element_type=F32)
             + jnp.dot(ec.astype(BF16), vc, preferred_element_type=F32))
        o_ref[pl.ds(q0, gq), :] = (o * inv).astype(o_ref.dtype)

    def per_pair(i, carry):
        first, second = scores(2 * i), scores(2 * i + 1)
        attend(*first)
        attend(*second)
        return carry

    lax.fori_loop(0, rows // (2 * r_grp), per_pair, 0)


def _nbr_attention(layer, proj, cache_k, cache_v, rpb, n_batch, n_tok):
    assert 2 * GRID_W == V7X_LANES and n_tok % GRID_W == 0
    rows = n_tok // GRID_W
    assert rows % (2 * NBR_ROWS_PER_GROUP) == 0 and rows >= NA_KH + NBR_ROWS_PER_GROUP and GRID_W >= NA_KW
    layout = _nbr_layout(rows)
    r_grp, win = NBR_ROWS_PER_GROUP, NA_KH + NBR_ROWS_PER_GROUP
    t_ctx = cache_k.shape[3]
    blk = lambda col0: pl.BlockSpec((n_tok, HEAD_DIM), lambda h, b, l: (b, col0 + h))
    ctx = pl.BlockSpec((None, None, None, t_ctx, HEAD_DIM), lambda h, b, l: (b, l[0], h, 0, 0))
    return pl.pallas_call(
        functools.partial(_nbr_attn_kernel, rows=rows, layout=layout, scale=HEAD_DIM ** -0.5),
        grid_spec=pltpu.PrefetchScalarGridSpec(
            num_scalar_prefetch=1, grid=(N_HEADS, n_batch),
            in_specs=[pl.BlockSpec(memory_space=pltpu.SMEM),
                      blk(0), blk(N_HEADS), blk(2 * N_HEADS), ctx, ctx],
            out_specs=pl.BlockSpec((n_tok, HEAD_DIM), lambda h, b, l: (b, h)),
            scratch_shapes=[pltpu.VMEM((len(layout[0]), r_grp * GRID_W, win * GRID_W), F32),
                            pltpu.VMEM((len(layout[2]), GRID_W, V7X_LANES), F32)]),
        out_shape=jax.ShapeDtypeStruct((n_batch * n_tok, N_HEADS * HEAD_DIM), BF16),
        compiler_params=_cparams(2),
        name="nbr_attention",
    )(_layer_arg(layer), rpb.reshape(rpb.shape[0], -1), proj, proj, proj, cache_k, cache_v)


@functools.lru_cache(maxsize=None)
def _dft_tables(n):
    jk = (np.arange(n, dtype=np.int64)[:, None] * np.arange(n, dtype=np.int64)[None, :]) % n
    ang = 2.0 * np.pi * jk.astype(np.float64) / n
    scale = 1.0 / math.sqrt(n)
    return ((np.cos(ang) * scale).astype(BF16), (np.sin(ang) * scale).astype(BF16))


@functools.lru_cache(maxsize=None)
def _mirror_matrix(t):
    r = np.zeros((t, t), np.float32)
    q = np.arange(1, t)
    r[q, t - q] = 1.0
    return r.astype(BF16)


def _fourier_channel_stage(f_ref, cc_ref, sc_ref, rev_ref, zc_ref, zs_ref, edge_ref):
    n, fd = f_ref.shape
    gd = fd // F_GROUPS
    half = n // 2
    tile = rev_ref.shape[0]
    n_tiles = half // tile
    for t in range(n_tiles):
        rows = slice(t * tile, (t + 1) * tile)
        lo = f_ref[rows, :].astype(F32)
        src = (2 * n_tiles - 1 - t) * tile
        mir = jnp.dot(rev_ref[...], f_ref[src:src + tile, :], preferred_element_type=F32)
        if t > 0:
            first = lax.broadcasted_iota(jnp.int32, (tile, 1), 0) == 0
            mir = jnp.where(first, f_ref[src + tile:src + tile + 1, :].astype(F32), mir)
        x_even = (lo + mir).astype(BF16)
        x_odd = (lo - mir).astype(BF16)
        for g in range(F_GROUPS):
            sl = slice(g * gd, (g + 1) * gd)
            zc_ref[rows, sl] = jnp.dot(x_even[:, sl], cc_ref[...], preferred_element_type=F32).astype(BF16)
            zs_ref[rows, sl] = jnp.dot(x_odd[:, sl], sc_ref[...], preferred_element_type=F32).astype(BF16)
    sign = 1.0 - 2.0 * (lax.broadcasted_iota(jnp.int32, (8, 1), 0) & 1).astype(F32)
    x_half = f_ref[half:half + V7X_BF16_SUBLANES, :]
    for g in range(F_GROUPS):
        sl = slice(g * gd, (g + 1) * gd)
        z_half = jnp.dot(x_half[:, sl], cc_ref[...], preferred_element_type=F32)[0:1, :]
        edge_ref[:, sl] = sign * (z_half * (1.0 / math.sqrt(n)))


def _fourier_kernel(f_ref, cc_ref, sc_ref, rev_ref, wc_ref, ws_ref, o_ref, zc_ref, zs_ref, edge_ref):
    fd = f_ref.shape[1]

    @pl.when(pl.program_id(1) == 0)
    def _():
        _fourier_channel_stage(f_ref, cc_ref, sc_ref, rev_ref, zc_ref, zs_ref, edge_ref)

    y = (jnp.dot(wc_ref[...], zc_ref[...], preferred_element_type=F32)
         + jnp.dot(ws_ref[...], zs_ref[...], preferred_element_type=F32))
    tm = y.shape[0]
    y = y.reshape(tm // 8, 8, fd) + edge_ref[...][None]
    o_ref[...] = y.reshape(tm, fd).astype(o_ref.dtype)


def _fourier_fold_kernel(f_ref, cc_ref, sc_ref, rev_ref, wc_ref, ws_ref, wh_ref, o_ref, zc_ref, zs_ref,
                         edge_ref, next_ref):
    fd = f_ref.shape[1]
    t_rows = wc_ref.shape[0]
    sub = rev_ref.shape[0]
    n_sub = t_rows // sub

    @pl.when(pl.program_id(1) == 0)
    def _():
        _fourier_channel_stage(f_ref, cc_ref, sc_ref, rev_ref, zc_ref, zs_ref, edge_ref)
        y_half = jnp.dot(wh_ref[...], zc_ref[...], preferred_element_type=F32)[0:1, :] + edge_ref[0:1, :]
        next_ref[...] = jnp.broadcast_to(y_half, next_ref.shape)

    e = jnp.dot(wc_ref[...], zc_ref[...], preferred_element_type=F32)
    e = (e.reshape(t_rows // 8, 8, fd) + edge_ref[...][None]).reshape(t_rows, fd)
    o = -jnp.dot(ws_ref[...], zs_ref[...], preferred_element_type=F32)
    o_ref[0] = (e - o).astype(o_ref.dtype)
    upper = e + o
    upper_bf = upper.astype(BF16)
    first = lax.broadcasted_iota(jnp.int32, (sub, 1), 0) == 0
    for j in range(n_sub):
        src = n_sub - 1 - j
        blk = jnp.dot(rev_ref[...], upper_bf[src * sub:(src + 1) * sub, :], preferred_element_type=F32)
        row0 = next_ref[0:1, :] if j == 0 else upper[(n_sub - j) * sub:(n_sub - j) * sub + 1, :]
        o_ref[1, j * sub:(j + 1) * sub, :] = jnp.where(first, row0, blk).astype(o_ref.dtype)
    next_ref[...] = jnp.broadcast_to(upper[0:1, :], next_ref.shape)


def _fourier(proj, n_seq, seq):
    attn = N_HEADS * HEAD_DIM
    fd = attn
    gd = fd // F_GROUPS
    cc, sc = _dft_tables(gd)
    cp, sp = _dft_tables(seq)
    half = seq // 2
    tm, tile = min(TM_FOURIER, seq), min(TM_FOURIER, half)
    assert seq % tm == 0 and half % tile == 0 and tile % V7X_BF16_SUBLANES == 0 and proj.shape[1] // fd > 3
    consts = (jnp.asarray(cc), jnp.asarray(sc), jnp.asarray(_mirror_matrix(tile)),
              jnp.asarray(cp[:, :half]), jnp.asarray(-sp[:, :half]))
    const_specs = [pl.BlockSpec((gd, gd), lambda b, i: (0, 0)),
                   pl.BlockSpec((gd, gd), lambda b, i: (0, 0)),
                   pl.BlockSpec((tile, tile), lambda b, i: (0, 0))]
    scratch = [pltpu.VMEM((half, fd), BF16), pltpu.VMEM((half, fd), BF16), pltpu.VMEM((8, fd), F32)]
    if half % TM_MERGE == 0:
        t_rows, n_tiles = TM_MERGE, half // TM_MERGE
        w_half = np.zeros((V7X_BF16_SUBLANES, half), np.float32)
        w_half[0] = cp[half, :half].astype(np.float32)
        rows = lambda b, i: (n_tiles - 1 - i, 0)
        return pl.pallas_call(
            _fourier_fold_kernel,
            grid=(n_seq, n_tiles),
            in_specs=[pl.BlockSpec((seq, fd), lambda b, i: (b, 3))] + const_specs
            + [pl.BlockSpec((t_rows, half), rows), pl.BlockSpec((t_rows, half), rows),
               pl.BlockSpec((V7X_BF16_SUBLANES, half), lambda b, i: (0, 0))],
            out_specs=pl.BlockSpec((None, None, 2, t_rows, fd), lambda b, i: (b, n_tiles - 1 - i, 0, 0, 0)),
            out_shape=jax.ShapeDtypeStruct((n_seq, n_tiles, 2, t_rows, fd), BF16),
            scratch_shapes=scratch + [pltpu.VMEM((8, fd), F32)],
            compiler_params=_cparams(2),
            name="fourier_fold_%d" % seq,
        )(proj, *consts, jnp.asarray(w_half.astype(BF16)))
    return pl.pallas_call(
        _fourier_kernel,
        grid=(n_seq, seq // tm),
        in_specs=[pl.BlockSpec((seq, fd), lambda b, i: (b, 3))] + const_specs
        + [pl.BlockSpec((tm, half), lambda b, i: (i, 0)), pl.BlockSpec((tm, half), lambda b, i: (i, 0))],
        out_specs=pl.BlockSpec((tm, fd), lambda b, i: (b * (seq // tm) + i, 0)),
        out_shape=jax.ShapeDtypeStruct((n_seq * seq, fd), BF16),
        scratch_shapes=scratch,
        compiler_params=_cparams(2),
        name="fourier_%d" % seq,
    )(proj, *consts)


def _merge_out_kernel(l_ref, x_ref, a_ref, fo_ref, ga_ref, gb_ref, mod_ref, wpa_ref, wpb_ref, wout_ref,
                      o_ref, y_ref, *, tc):
    d = x_ref.shape[1]
    for c in range(d // tc):
        sl = slice(c * tc, (c + 1) * tc)
        ya = jnp.dot(a_ref[...], wpa_ref[:, sl], preferred_element_type=F32)
        yb = jnp.dot(fo_ref[...], wpb_ref[:, sl], preferred_element_type=F32)
        y = (jax.nn.sigmoid(ga_ref[:, sl].astype(F32)) * ya
             + jax.nn.sigmoid(gb_ref[:, sl].astype(F32)) * yb)
        y_ref[:, sl] = y.astype(BF16)
    for c in range(d // tc):
        sl = slice(c * tc, (c + 1) * tc)
        o = jnp.dot(y_ref[...], wout_ref[:, sl], preferred_element_type=F32)
        o_ref[:, sl] = x_ref[:, sl] + mod_ref[2:3, sl] * o


def _merge_out(layer, x, a, fo, proj, mods, w_pa, w_pb, w_out, *, rows_per_mod, mod_row0):
    m, d = x.shape
    attn, fd = a.shape[1], fo.shape[-1]
    tm = min(TM_MERGE, m)
    assert m % tm == 0 and (3 * attn + fd) % d == 0 and rows_per_mod % tm == 0
    gate_blk0 = (3 * attn + fd) // d
    whole = lambda arr: pl.BlockSpec((None,) + arr.shape[1:], lambda i, l: (l[0], 0, 0),
                                     pipeline_mode=pl.Buffered(1))
    if fo.ndim == 2:
        fo_spec = pl.BlockSpec((tm, fd), lambda i, l: (i, 0))
    else:
        n_tiles = fo.shape[1]
        assert fo.shape[2:4] == (2, tm) and fo.shape[0] * 2 * n_tiles * tm == m

        def folded(i, l):
            j = i % (2 * n_tiles)
            upper = j >= n_tiles
            return (i // (2 * n_tiles), jnp.where(upper, 2 * n_tiles - 1 - j, j), upper.astype(jnp.int32), 0, 0)

        fo_spec = pl.BlockSpec((None, None, None, tm, fd), folded)
    return pl.pallas_call(
        functools.partial(_merge_out_kernel, tc=min(TC_MERGE, d)),
        grid_spec=pltpu.PrefetchScalarGridSpec(
            num_scalar_prefetch=1, grid=(m // tm,),
            in_specs=[pl.BlockSpec((tm, d), lambda i, l: (i, 0)),
                      pl.BlockSpec((tm, attn), lambda i, l: (i, 0)),
                      fo_spec,
                      pl.BlockSpec((tm, d), lambda i, l: (i, gate_blk0)),
                      pl.BlockSpec((tm, d), lambda i, l: (i, gate_blk0 + 1)),
                      pl.BlockSpec((None, None, N_MOD, d),
                                   lambda i, l: (l[0], mod_row0 + (i * tm) // rows_per_mod, 0, 0)),
                      whole(w_pa), whole(w_pb), whole(w_out)],
            out_specs=pl.BlockSpec((tm, d), lambda i, l: (i, 0)),
            scratch_shapes=[pltpu.VMEM((tm, d), BF16)]),
        out_shape=jax.ShapeDtypeStruct((m, d), F32),
        compiler_params=_cparams(1),
        name="merge_out",
    )(_layer_arg(layer), x, a, fo, proj, proj, mods, w_pa, w_pb, w_out)


def _ffn_kernel(l_ref, x_ref, xp_ref, xn_ref, mod_ref, g_ref, wa_ref, wg_ref, cwa_ref, cwg_ref, cba_ref,
                cbg_ref, wd_ref, gf_ref, o_ref, h_ref, u_ref, *maybe_acc, seq, tc, final_norm):
    tm = x_ref.shape[0]
    halo = xp_ref.shape[0]
    tf = wa_ref.shape[1]
    seg = min(seq, tm)
    n_seg, stride = tm // seg, seg + halo
    acc_ref = maybe_acc[0] if maybe_acc else o_ref
    n_rows = acc_ref.shape[0]
    i, f = pl.program_id(0), pl.program_id(1)

    @pl.when(f == 0)
    def _():
        g, scale, shift = g_ref[...], mod_ref[4:5, :], mod_ref[3:4, :]
        tok0 = i * tm
        _norm_mod_store(h_ref, 0, xp_ref, g, scale, shift, keep=(tok0 & (seq - 1)) != 0)
        for s in range(n_seg):
            _norm_mod_store(h_ref, halo + s * stride, x_ref, g, scale, shift, src_row0=s * seg, n=seg)
            if s < n_seg - 1:
                h_ref[(s + 1) * stride:(s + 1) * stride + halo, :] = jnp.zeros((halo, h_ref.shape[1]), BF16)
        _norm_mod_store(h_ref, n_seg * stride, xn_ref, g, scale, shift, keep=((tok0 + tm) & (seq - 1)) != 0)
        acc_ref[...] = jnp.zeros_like(acc_ref)

    h = h_ref[...]

    def conv(c, idx, cw_ref, cb_ref, cs):
        return (u_ref[c, idx, pl.ds(halo - 1, n_rows), :] * cw_ref[0:1, cs]
                + u_ref[c, idx, pl.ds(halo, n_rows), :] * cw_ref[1:2, cs]
                + u_ref[c, idx, pl.ds(halo + 1, n_rows), :] * cw_ref[2:3, cs] + cb_ref[:, cs])

    n_chunks = tf // tc
    for c in range(n_chunks):
        cs = slice(c * tc, (c + 1) * tc)
        u_ref[c, 0] = jnp.dot(h, wa_ref[:, cs], preferred_element_type=F32)
        u_ref[c, 1] = jnp.dot(h, wg_ref[:, cs], preferred_element_type=F32)
    for c in range(n_chunks):
        cs = slice(c * tc, (c + 1) * tc)
        ca = conv(c, 0, cwa_ref, cba_ref, cs)
        cg = conv(c, 1, cwg_ref, cbg_ref, cs)
        act = ((cg * jax.nn.sigmoid(cg)) * ca).astype(BF16)
        acc_ref[...] += jnp.dot(act, wd_ref[cs, :], preferred_element_type=F32)

    @pl.when(f == pl.num_programs(1) - 1)
    def _():
        for s in range(n_seg):
            rows = slice(s * seg, (s + 1) * seg)
            y = x_ref[rows, :] + mod_ref[5:6, :] * acc_ref[s * stride:s * stride + seg, :]
            if final_norm:
                y = (y * lax.rsqrt(jnp.mean(y * y, axis=-1, keepdims=True) + RMS_EPS)) * gf_ref[...]
            o_ref[rows, :] = y


def _conv_ffn(layer, x, mods, g, w_up, conv_w, conv_b, w_down, final_gain, *, seq, rows_per_mod, mod_row0,
              final_norm):
    m, d = x.shape
    ffn = w_down.shape[1]
    tm, tf = min(TM_FFN, m), min(TF_FFN, ffn)
    tc = min(TC_FFN, tf)
    halo = V7X_BF16_SUBLANES
    assert seq & (seq - 1) == 0 and (seq % tm == 0 or tm % seq == 0) and m % seq == 0
    assert m % tm == 0 and ffn % tf == 0 and tf % tc == 0 and min(seq, tm) % halo == 0 and rows_per_mod % tm == 0
    nf, nhb = ffn // tf, m // halo
    n_seg = tm // min(seq, tm)
    work_rows = tm + (n_seg + 1) * halo
    acc = [pltpu.VMEM((work_rows - 2 * halo, d), F32)] if n_seg > 1 else []
    mat = lambda rows, cols, imap: pl.BlockSpec((None, rows, cols), imap)
    return pl.pallas_call(
        functools.partial(_ffn_kernel, seq=seq, tc=tc, final_norm=final_norm),
        grid_spec=pltpu.PrefetchScalarGridSpec(
            num_scalar_prefetch=1, grid=(m // tm, nf),
            in_specs=[pl.BlockSpec((tm, d), lambda i, f, l: (i, 0)),
                      pl.BlockSpec((halo, d), lambda i, f, l: (jnp.maximum(i * (tm // halo) - 1, 0), 0)),
                      pl.BlockSpec((halo, d),
                                   lambda i, f, l: (jnp.minimum((i + 1) * (tm // halo), nhb - 1), 0)),
                      pl.BlockSpec((None, None, N_MOD, d),
                                   lambda i, f, l: (l[0], mod_row0 + (i * tm) // rows_per_mod, 0, 0)),
                      mat(1, d, lambda i, f, l: (l[0], 0, 0)),
                      mat(d, tf, lambda i, f, l: (l[0], 0, f)),
                      mat(d, tf, lambda i, f, l: (l[0], 0, nf + f)),
                      mat(3, tf, lambda i, f, l: (l[0], 0, f)),
                      mat(3, tf, lambda i, f, l: (l[0], 0, nf + f)),
                      mat(1, tf, lambda i, f, l: (l[0], 0, f)),
                      mat(1, tf, lambda i, f, l: (l[0], 0, nf + f)),
                      mat(tf, d, lambda i, f, l: (l[0], f, 0)),
                      pl.BlockSpec((1, d), lambda i, f, l: (0, 0))],
            out_specs=pl.BlockSpec((tm, d), lambda i, f, l: (i, 0)),
            scratch_shapes=[pltpu.VMEM((work_rows, d), BF16),
                            pltpu.VMEM((tf // tc, 2, work_rows, tc), F32)] + acc),
        out_shape=jax.ShapeDtypeStruct((m, d), F32),
        compiler_params=_cparams(2),
        name="conv_ffn_final" if final_norm else "conv_ffn",
    )(_layer_arg(layer), x, x, x, mods, g.reshape(g.shape[0], 1, d), w_up, w_up, conv_w, conv_w,
      conv_b.reshape(conv_b.shape[0], 1, -1), conv_b.reshape(conv_b.shape[0], 1, -1), w_down,
      final_gain.reshape(1, d))


def kernel(x_prompt, x_sample, cache_k, cache_v, c, c_ctx, w_mod, b_mod, norm1_g, w_in, rpb, w_pa, w_pb,
           w_out, norm2_g, w_up, conv_w, conv_b, w_down, norm_f_g):
    n_ctx, seq, d = x_prompt.shape
    n_lat, n_tok, _ = x_sample.shape
    n_layers = w_mod.shape[0]
    assert n_lat < MOD_ROWS and n_layers >= 1

    c_all = jnp.zeros((MOD_ROWS, d), F32).at[:n_lat].set(c).at[n_lat].set(c_ctx)
    mods = _modulation(c_all, w_mod, b_mod).reshape(n_layers, MOD_ROWS, N_MOD, d)
    w_in, w_pa, w_pb, w_out, w_up, w_down = (w.astype(BF16) for w in (w_in, w_pa, w_pb, w_out, w_up, w_down))

    xp = x_prompt.reshape(n_ctx * seq, d)
    xs = x_sample.reshape(n_lat * n_tok, d)
    ctx_mod = dict(rows_per_mod=n_ctx * seq, mod_row0=n_lat)
    lat_mod = dict(rows_per_mod=n_tok, mod_row0=0)
    caches = (jnp.zeros((n_ctx, n_layers, N_HEADS, seq, HEAD_DIM), F32),) * 2
    for l in range(n_layers):
        last = l == n_layers - 1
        ffn_w = (mods, norm2_g, w_up, conv_w, conv_b, w_down, norm_f_g)
        proj, *caches = _in_proj(l, xp, mods, norm1_g, w_in, caches=caches, **ctx_mod)
        a = _ctx_attention(proj, n_ctx, seq)
        fo = _fourier(proj, n_ctx, seq)
        xp = _merge_out(l, xp, a, fo, proj, mods, w_pa, w_pb, w_out, **ctx_mod)
        xp = _conv_ffn(l, xp, *ffn_w, seq=seq, final_norm=last, **ctx_mod)

        (proj,) = _in_proj(l, xs, mods, norm1_g, w_in, **lat_mod)
        a = _nbr_attention(l, proj, cache_k, cache_v, rpb, n_lat, n_tok)
        fo = _fourier(proj, n_lat, n_tok)
        xs = _merge_out(l, xs, a, fo, proj, mods, w_pa, w_pb, w_out, **lat_mod)
        xs = _conv_ffn(l, xs, *ffn_w, seq=n_tok, final_norm=last, **lat_mod)

    return (xp.reshape(n_ctx, seq, d), xs.reshape(n_lat, n_tok, d), caches[0], caches[1])
```

```python
import functools
import math

import numpy as np
import jax
import jax.numpy as jnp
from jax import lax
from jax.experimental import pallas as pl
from jax.experimental.pallas import tpu as pltpu

F32 = jnp.float32
BF16 = jnp.bfloat16

RMS_EPS = 1e-6
LOG2_E = math.log2(math.e)
N_MOD = 6
N_HEADS = 8
HEAD_DIM = 128
GRID_W = 64
NA_KH = 8
NA_KW = 16
F_GROUPS = 4
NBR_ROWS_PER_GROUP = 4
MOD_ROWS = 8
V7X_LANES = 128
V7X_F32_SUBLANES = 8
V7X_BF16_SUBLANES = 16
V7X_VMEM_BYTES = 64 * 1024 * 1024
VMEM_LIMIT = V7X_VMEM_BYTES - 8 * 1024 * 1024

TM_IN, TN_IN, TN_IN_WIDE = 1024, 1024, 2048
TN_MOD = 1024
TM_MERGE, TC_MERGE = 512, 512
TM_FFN, TF_FFN, TC_FFN = 512, 512, 256
TM_FOURIER = 256
NORM_ROW_CHUNK = 16
NORM_UNROLL = 8


def _cparams(n_axes):
    return pltpu.CompilerParams(dimension_semantics=("arbitrary",) * n_axes,
                                vmem_limit_bytes=VMEM_LIMIT)


def _layer_arg(layer):
    return jnp.full((1,), layer, jnp.int32)


def _norm_mod_rows(x, gs, shift):
    return (x * lax.rsqrt(jnp.mean(x * x, axis=-1, keepdims=True) + RMS_EPS)) * gs + shift


def _norm_mod_store(dst_ref, dst_row0, src_ref, g, scale, shift, src_row0=0, n=None):
    n = src_ref.shape[0] if n is None else n
    chunk = min(NORM_ROW_CHUNK, n)
    assert n % chunk == 0 and src_row0 % chunk == 0 and dst_row0 % chunk == 0
    gs = g * (1.0 + scale)

    def rows(r, carry):
        r0 = pl.multiple_of(r * chunk, chunk)
        h = _norm_mod_rows(src_ref[pl.ds(src_row0 + r0, chunk), :], gs, shift)
        dst_ref[pl.ds(dst_row0 + r0, chunk), :] = h.astype(BF16)
        return carry

    lax.fori_loop(0, n // chunk, rows, 0, unroll=min(NORM_UNROLL, n // chunk))


def _mod_kernel(c_ref, w_ref, b_ref, o_ref):
    cv = c_ref[...]
    s = (cv * jax.nn.sigmoid(cv)).astype(BF16)
    o_ref[...] = jnp.dot(s, w_ref[...].astype(BF16), preferred_element_type=F32) + b_ref[...]


def _modulation(c_all, w_mod, b_mod):
    n_layers, d, nd = w_mod.shape
    tn = min(TN_MOD, nd)
    assert nd % tn == 0
    return pl.pallas_call(
        _mod_kernel,
        grid=(n_layers, nd // tn),
        in_specs=[pl.BlockSpec((MOD_ROWS, d), lambda l, j: (0, 0)),
                  pl.BlockSpec((None, d, tn), lambda l, j: (l, 0, j)),
                  pl.BlockSpec((None, 1, tn), lambda l, j: (l, 0, j))],
        out_specs=pl.BlockSpec((None, MOD_ROWS, tn), lambda l, j: (l, 0, j)),
        out_shape=jax.ShapeDtypeStruct((n_layers, MOD_ROWS, nd), F32),
        compiler_params=_cparams(2),
        name="modulation",
    )(c_all, w_mod, b_mod.reshape(n_layers, 1, nd))


def _in_proj_kernel(l_ref, x_ref, mod_ref, g_ref, w_ref, *rest, kv_tiles):
    h_ref = rest[-1]
    j = pl.program_id(1)

    @pl.when(j == 0)
    def _():
        _norm_mod_store(h_ref, 0, x_ref, g_ref[...], mod_ref[1:2, :], mod_ref[0:1, :])

    res = jnp.dot(h_ref[...], w_ref[...], preferred_element_type=F32)
    if kv_tiles is None:
        o_ref = rest[0]
        o_ref[...] = res.astype(o_ref.dtype)
    else:
        o_ref, kc_ref, vc_ref = rest[-4], rest[-3], rest[-2]
        o_ref[...] = res.astype(o_ref.dtype)
        k0, v0, v1 = kv_tiles
        nb, nh, s, dh = kc_ref.shape

        def scatter(dst_ref):
            for bb in range(nb):
                for hh in range(nh):
                    dst_ref[bb, hh] = res[bb * s:(bb + 1) * s, hh * dh:(hh + 1) * dh]

        @pl.when((j >= k0) & (j < v0))
        def _():
            scatter(kc_ref)

        @pl.when((j >= v0) & (j < v1))
        def _():
            scatter(vc_ref)


def _in_proj(layer, x, mods, g, w, *, rows_per_mod, mod_row0, caches=None):
    m, d = x.shape
    n_out = w.shape[2]
    wide = caches is None and n_out % TN_IN_WIDE == 0
    tm, tn = min(TM_IN, m), min(TN_IN_WIDE if wide else TN_IN, n_out)
    assert m % tm == 0 and n_out % tn == 0 and (rows_per_mod % tm == 0 or tm % rows_per_mod == 0)
    attn = N_HEADS * HEAD_DIM
    in_specs = [pl.BlockSpec((tm, d), lambda i, j, l: (i, 0)),
                pl.BlockSpec((None, None, N_MOD, d),
                             lambda i, j, l: (l[0], mod_row0 + (i * tm) // rows_per_mod, 0, 0)),
                pl.BlockSpec((None, 1, d), lambda i, j, l: (l[0], 0, 0)),
                pl.BlockSpec((None, d, tn), lambda i, j, l: (l[0], 0, j))]
    out_specs = [pl.BlockSpec((tm, tn), lambda i, j, l: (i, j))]
    out_shape = [jax.ShapeDtypeStruct((m, n_out), BF16)]
    args = [_layer_arg(layer), x, mods, g.reshape(g.shape[0], 1, d), w]
    kv_tiles, aliases = None, {}
    if caches is not None:
        seq = caches[0].shape[3]
        assert tm % seq == 0 and tn % HEAD_DIM == 0 and attn % tn == 0
        k0, v0, v1 = attn // tn, 2 * attn // tn, 3 * attn // tn
        kv_tiles = (k0, v0, v1)
        blk = (tm // seq, None, tn // HEAD_DIM, seq, HEAD_DIM)
        out_specs += [pl.BlockSpec(blk, lambda i, j, l: (i, l[0], jnp.clip(j - k0, 0, v0 - k0 - 1), 0, 0)),
                      pl.BlockSpec(blk, lambda i, j, l: (i, l[0], jnp.clip(j - v0, 0, v1 - v0 - 1), 0, 0))]
        out_shape += [jax.ShapeDtypeStruct(c.shape, c.dtype) for c in caches]
        in_specs += [pl.BlockSpec(memory_space=pl.ANY)] * 2
        aliases = {len(args): 1, len(args) + 1: 2}
        args += list(caches)
    return pl.pallas_call(
        functools.partial(_in_proj_kernel, kv_tiles=kv_tiles),
        grid_spec=pltpu.PrefetchScalarGridSpec(
            num_scalar_prefetch=1, grid=(m // tm, n_out // tn), in_specs=in_specs, out_specs=out_specs,
            scratch_shapes=[pltpu.VMEM((tm, d), BF16)]),
        out_shape=out_shape,
        input_output_aliases=aliases,
        compiler_params=_cparams(2),
        name="in_proj_ctx" if caches is not None else "in_proj_lat",
    )(*args)


def _ctx_attn_kernel(q_ref, k_ref, v_ref, o_ref, *, scale):
    for h in range(N_HEADS):
        sl = slice(h * HEAD_DIM, (h + 1) * HEAD_DIM)
        s = lax.dot_general(q_ref[:, sl], k_ref[:, sl], (((1,), (1,)), ((), ())),
                            preferred_element_type=F32) * scale
        e = jnp.exp(s - jnp.max(s, axis=-1, keepdims=True))
        p = e * (1.0 / jnp.sum(e, axis=-1, keepdims=True))
        o_ref[:, sl] = jnp.dot(p.astype(BF16), v_ref[:, sl], preferred_element_type=F32).astype(o_ref.dtype)


def _ctx_attention(proj, n_seq, seq):
    attn = N_HEADS * HEAD_DIM
    return pl.pallas_call(
        functools.partial(_ctx_attn_kernel, scale=HEAD_DIM ** -0.5),
        grid=(n_seq,),
        in_specs=[pl.BlockSpec((seq, attn), lambda b: (b, 0)),
                  pl.BlockSpec((seq, attn), lambda b: (b, 1)),
                  pl.BlockSpec((seq, attn), lambda b: (b, 2))],
        out_specs=pl.BlockSpec((seq, attn), lambda b: (b, 0)),
        out_shape=jax.ShapeDtypeStruct((n_seq * seq, attn), BF16),
        compiler_params=_cparams(1),
        name="ctx_attention",
    )(proj, proj, proj)


def _nbr_layout(rows):
    r_grp, win = NBR_ROWS_PER_GROUP, NA_KH + NBR_ROWS_PER_GROUP
    types, type_of_group = [], []
    for g in range(rows // r_grp):
        w0 = min(max(r_grp * g - NA_KH // 2, 0), rows - win)
        desc = []
        for qi in range(r_grp):
            r = r_grp * g + qi
            r0 = min(max(r - NA_KH // 2, 0), rows - NA_KH)
            desc.append(tuple((w0 + kj - r) if r0 <= w0 + kj < r0 + NA_KH else None for kj in range(win)))
        desc = tuple(desc)
        if desc not in types:
            types.append(desc)
        type_of_group.append(types.index(desc))
    pairs = sorted({(d[qi][2 * m], d[qi][2 * m + 1]) for d in types for qi in range(r_grp)
                    for m in range(win // 2)} - {(None, None)},
                   key=lambda p: tuple(-99 if v is None else v for v in p))
    return types, type_of_group, pairs


def _nbr_attn_kernel(l_ref, rpb_ref, q_ref, k_ref, v_ref, kc_ref, vc_ref, o_ref, bias_ref, pair_ref, *,
                     rows, layout, scale):
    types, type_of_group, pairs = layout
    w = GRID_W
    r_grp, win = NBR_ROWS_PER_GROUP, NA_KH + NBR_ROWS_PER_GROUP
    gq, wk = r_grp * w, win * w
    n_dr, n_dc = 2 * NA_KH - 1, 2 * NA_KW - 1
    neg_inf = float("-inf")
    h = pl.program_id(0)
    layer = l_ref[0]

    @pl.when(pl.program_id(1) == 0)
    def _build_bias_tables():
        cq = lax.broadcasted_iota(jnp.int32, (w, V7X_LANES), 0)
        lane = lax.broadcasted_iota(jnp.int32, (w, V7X_LANES), 1)
        ck = lane & (w - 1)
        c0 = jnp.clip(cq - NA_KW // 2, 0, w - NA_KW)
        in_win = (ck >= c0) & (ck < c0 + NA_KW)
        dc_idx = jnp.where(in_win, jnp.clip(ck - cq, -(NA_KW - 1), NA_KW - 1) + (NA_KW - 1), -1)
        left = lane < w
        for p in range(len(pairs)):
            pair_ref[p] = jnp.full((w, V7X_LANES), neg_inf, F32)

        def per_dc(dc, carry):
            hit = dc_idx == dc
            for p, (da, db) in enumerate(pairs):
                va = neg_inf if da is None else rpb_ref[layer, (h * n_dr + da + NA_KH - 1) * n_dc + dc]
                vb = neg_inf if db is None else rpb_ref[layer, (h * n_dr + db + NA_KH - 1) * n_dc + dc]
                pair_ref[p] = jnp.where(hit, jnp.where(left, va, vb) * LOG2_E, pair_ref[p])
            return carry

        lax.fori_loop(0, n_dc, per_dc, 0)
        for t, desc in enumerate(types):
            for qi in range(r_grp):
                for m in range(win // 2):
                    pr = (desc[qi][2 * m], desc[qi][2 * m + 1])
                    tile = (jnp.full((w, V7X_LANES), neg_inf, F32) if pr == (None, None)
                            else pair_ref[pairs.index(pr)])
                    bias_ref[t, qi * w:(qi + 1) * w, m * V7X_LANES:(m + 1) * V7X_LANES] = tile

    kc = kc_ref[...].astype(BF16)
    vc = vc_ref[...].astype(BF16)
    nt = (((1,), (1,)), ((), ()))
    scale2 = scale * LOG2_E
    default_type = max(set(type_of_group), key=type_of_group.count)

    def scores(g):
        w0 = jnp.clip(r_grp * g - NA_KH // 2, 0, rows - win)
        q0 = pl.multiple_of(g * gq, gq)
        k0 = pl.multiple_of(w0 * w, w)
        t = jnp.int32(default_type)
        for gi, ti in enumerate(type_of_group):
            if ti != default_type:
                t = jnp.where(g == gi, ti, t)
        q = q_ref[pl.ds(q0, gq), :]
        s = lax.dot_general(q, k_ref[pl.ds(k0, wk), :], nt, preferred_element_type=F32) * scale2 + bias_ref[t]
        sc = lax.dot_general(q, kc, nt, preferred_element_type=F32) * scale2
        return q0, k0, s, sc

    def attend(q0, k0, s, sc):
        mx = jnp.maximum(jnp.max(s, axis=-1, keepdims=True), jnp.max(sc, axis=-1, keepdims=True))
        e = jnp.exp2(s - mx)
        ec = jnp.exp2(sc - mx)
        inv = 1.0 / (jnp.sum(e, axis=-1, keepdims=True) + jnp.sum(ec, axis=-1, keepdims=True))
        o = (jnp.dot(e.astype(BF16), v_ref[pl.ds(k0, wk), :], preferred_element_type=F32)
             + jnp.dot(ec.astype(BF16), vc, preferred_element_type=F32))
        o_ref[pl.ds(q0, gq), :] = (o * inv).astype(o_ref.dtype)

    def per_pair(i, carry):
        first, second = scores(2 * i), scores(2 * i + 1)
        attend(*first)
        attend(*second)
        return carry

    lax.fori_loop(0, rows // (2 * r_grp), per_pair, 0, unroll=2)


def _nbr_attention(layer, proj, cache_k, cache_v, rpb, n_batch, n_tok):
    assert 2 * GRID_W == V7X_LANES and n_tok % GRID_W == 0
    rows = n_tok // GRID_W
    assert rows % (2 * NBR_ROWS_PER_GROUP) == 0 and rows >= NA_KH + NBR_ROWS_PER_GROUP and GRID_W >= NA_KW
    layout = _nbr_layout(rows)
    r_grp, win = NBR_ROWS_PER_GROUP, NA_KH + NBR_ROWS_PER_GROUP
    t_ctx = cache_k.shape[3]
    blk = lambda col0: pl.BlockSpec((n_tok, HEAD_DIM), lambda h, b, l: (b, col0 + h))
    ctx = pl.BlockSpec((None, None, None, t_ctx, HEAD_DIM), lambda h, b, l: (b, l[0], h, 0, 0))
    return pl.pallas_call(
        functools.partial(_nbr_attn_kernel, rows=rows, layout=layout, scale=HEAD_DIM ** -0.5),
        grid_spec=pltpu.PrefetchScalarGridSpec(
            num_scalar_prefetch=1, grid=(N_HEADS, n_batch),
            in_specs=[pl.BlockSpec(memory_space=pltpu.SMEM),
                      blk(0), blk(N_HEADS), blk(2 * N_HEADS), ctx, ctx],
            out_specs=pl.BlockSpec((n_tok, HEAD_DIM), lambda h, b, l: (b, h)),
            scratch_shapes=[pltpu.VMEM((len(layout[0]), r_grp * GRID_W, win * GRID_W), F32),
                            pltpu.VMEM((len(layout[2]), GRID_W, V7X_LANES), F32)]),
        out_shape=jax.ShapeDtypeStruct((n_batch * n_tok, N_HEADS * HEAD_DIM), BF16),
        compiler_params=_cparams(2),
        name="nbr_attention",
    )(_layer_arg(layer), rpb.reshape(rpb.shape[0], -1), proj, proj, proj, cache_k, cache_v)


@functools.lru_cache(maxsize=None)
def _dft_tables(n):
    jk = (np.arange(n, dtype=np.int64)[:, None] * np.arange(n, dtype=np.int64)[None, :]) % n
    ang = 2.0 * np.pi * jk.astype(np.float64) / n
    scale = 1.0 / math.sqrt(n)
    return ((np.cos(ang) * scale).astype(BF16), (np.sin(ang) * scale).astype(BF16))


@functools.lru_cache(maxsize=None)
def _mirror_matrix(t):
    r = np.zeros((t, t), np.float32)
    q = np.arange(1, t)
    r[q, t - q] = 1.0
    return r.astype(BF16)


def _fourier_channel_stage(f_ref, cc_ref, sc_ref, rev_ref, zc_ref, zs_ref, edge_ref):
    n, fd = f_ref.shape
    gd = fd // F_GROUPS
    half = n // 2
    tile = rev_ref.shape[0]
    n_tiles = half // tile
    for t in range(n_tiles):
        rows = slice(t * tile, (t + 1) * tile)
        lo = f_ref[rows, :].astype(F32)
        src = (2 * n_tiles - 1 - t) * tile
        mir = jnp.dot(rev_ref[...], f_ref[src:src + tile, :], preferred_element_type=F32)
        if t > 0:
            first = lax.broadcasted_iota(jnp.int32, (tile, 1), 0) == 0
            mir = jnp.where(first, f_ref[src + tile:src + tile + 1, :].astype(F32), mir)
        x_even = (lo + mir).astype(BF16)
        x_odd = (lo - mir).astype(BF16)
        for g in range(F_GROUPS):
            sl = slice(g * gd, (g + 1) * gd)
            zc_ref[rows, sl] = jnp.dot(x_even[:, sl], cc_ref[...], preferred_element_type=F32).astype(BF16)
            zs_ref[rows, sl] = jnp.dot(x_odd[:, sl], sc_ref[...], preferred_element_type=F32).astype(BF16)
    sign = 1.0 - 2.0 * (lax.broadcasted_iota(jnp.int32, (8, 1), 0) & 1).astype(F32)
    x_half = f_ref[half:half + V7X_BF16_SUBLANES, :]
    for g in range(F_GROUPS):
        sl = slice(g * gd, (g + 1) * gd)
        z_half = jnp.dot(x_half[:, sl], cc_ref[...], preferred_element_type=F32)[0:1, :]
        edge_ref[:, sl] = sign * (z_half * (1.0 / math.sqrt(n)))


def _fourier_kernel(f_ref, cc_ref, sc_ref, rev_ref, wc_ref, ws_ref, o_ref, zc_ref, zs_ref, edge_ref):
    fd = f_ref.shape[1]

    @pl.when(pl.program_id(1) == 0)
    def _():
        _fourier_channel_stage(f_ref, cc_ref, sc_ref, rev_ref, zc_ref, zs_ref, edge_ref)

    y = (jnp.dot(wc_ref[...], zc_ref[...], preferred_element_type=F32)
         + jnp.dot(ws_ref[...], zs_ref[...], preferred_element_type=F32))
    tm = y.shape[0]
    y = y.reshape(tm // 8, 8, fd) + edge_ref[...][None]
    o_ref[...] = y.reshape(tm, fd).astype(o_ref.dtype)


def _fourier_fold_kernel(f_ref, cc_ref, sc_ref, rev_ref, wc_ref, ws_ref, wh_ref, o_ref, zc_ref, zs_ref,
                         edge_ref, next_ref):
    fd = f_ref.shape[1]
    t_rows = wc_ref.shape[0]
    sub = rev_ref.shape[0]
    n_sub = t_rows // sub

    @pl.when(pl.program_id(1) == 0)
    def _():
        _fourier_channel_stage(f_ref, cc_ref, sc_ref, rev_ref, zc_ref, zs_ref, edge_ref)
        y_half = jnp.dot(wh_ref[...], zc_ref[...], preferred_element_type=F32)[0:1, :] + edge_ref[0:1, :]
        next_ref[...] = jnp.broadcast_to(y_half, next_ref.shape)

    e = jnp.dot(wc_ref[...], zc_ref[...], preferred_element_type=F32)
    e = (e.reshape(t_rows // 8, 8, fd) + edge_ref[...][None]).reshape(t_rows, fd)
    o = -jnp.dot(ws_ref[...], zs_ref[...], preferred_element_type=F32)
    o_ref[0] = (e - o).astype(o_ref.dtype)
    upper = e + o
    upper_bf = upper.astype(BF16)
    first = lax.broadcasted_iota(jnp.int32, (sub, 1), 0) == 0
    for j in range(n_sub):
        src = n_sub - 1 - j
        blk = jnp.dot(rev_ref[...], upper_bf[src * sub:(src + 1) * sub, :], preferred_element_type=F32)
        row0 = next_ref[0:1, :] if j == 0 else upper[(n_sub - j) * sub:(n_sub - j) * sub + 1, :]
        o_ref[1, j * sub:(j + 1) * sub, :] = jnp.where(first, row0, blk).astype(o_ref.dtype)
    next_ref[...] = jnp.broadcast_to(upper[0:1, :], next_ref.shape)


def _fourier(proj, n_seq, seq):
    attn = N_HEADS * HEAD_DIM
    fd = attn
    gd = fd // F_GROUPS
    cc, sc = _dft_tables(gd)
    cp, sp = _dft_tables(seq)
    half = seq // 2
    tm, tile = min(TM_FOURIER, seq), min(TM_FOURIER, half)
    assert seq % tm == 0 and half % tile == 0 and tile % V7X_BF16_SUBLANES == 0 and proj.shape[1] // fd > 3
    consts = (jnp.asarray(cc), jnp.asarray(sc), jnp.asarray(_mirror_matrix(tile)),
              jnp.asarray(cp[:, :half]), jnp.asarray(-sp[:, :half]))
    const_specs = [pl.BlockSpec((gd, gd), lambda b, i: (0, 0)),
                   pl.BlockSpec((gd, gd), lambda b, i: (0, 0)),
                   pl.BlockSpec((tile, tile), lambda b, i: (0, 0))]
    scratch = [pltpu.VMEM((half, fd), BF16), pltpu.VMEM((half, fd), BF16), pltpu.VMEM((8, fd), F32)]
    if half % TM_MERGE == 0:
        t_rows, n_tiles = TM_MERGE, half // TM_MERGE
        w_half = np.zeros((V7X_BF16_SUBLANES, half), np.float32)
        w_half[0] = cp[half, :half].astype(np.float32)
        rows = lambda b, i: (n_tiles - 1 - i, 0)
        return pl.pallas_call(
            _fourier_fold_kernel,
            grid=(n_seq, n_tiles),
            in_specs=[pl.BlockSpec((seq, fd), lambda b, i: (b, 3))] + const_specs
            + [pl.BlockSpec((t_rows, half), rows), pl.BlockSpec((t_rows, half), rows),
               pl.BlockSpec((V7X_BF16_SUBLANES, half), lambda b, i: (0, 0))],
            out_specs=pl.BlockSpec((None, None, 2, t_rows, fd), lambda b, i: (b, n_tiles - 1 - i, 0, 0, 0)),
            out_shape=jax.ShapeDtypeStruct((n_seq, n_tiles, 2, t_rows, fd), BF16),
            scratch_shapes=scratch + [pltpu.VMEM((8, fd), F32)],
            compiler_params=_cparams(2),
            name="fourier_fold_%d" % seq,
        )(proj, *consts, jnp.asarray(w_half.astype(BF16)))
    return pl.pallas_call(
        _fourier_kernel,
        grid=(n_seq, seq // tm),
        in_specs=[pl.BlockSpec((seq, fd), lambda b, i: (b, 3))] + const_specs
        + [pl.BlockSpec((tm, half), lambda b, i: (i, 0)), pl.BlockSpec((tm, half), lambda b, i: (i, 0))],
        out_specs=pl.BlockSpec((tm, fd), lambda b, i: (b * (seq // tm) + i, 0)),
        out_shape=jax.ShapeDtypeStruct((n_seq * seq, fd), BF16),
        scratch_shapes=scratch,
        compiler_params=_cparams(2),
        name="fourier_%d" % seq,
    )(proj, *consts)


def _merge_out_kernel(l_ref, x_ref, a_ref, fo_ref, ga_ref, gb_ref, mod_ref, wpa_ref, wpb_ref, wout_ref,
                      o_ref, y_ref, *, tc):
    d = x_ref.shape[1]
    for c in range(d // tc):
        sl = slice(c * tc, (c + 1) * tc)
        ya = jnp.dot(a_ref[...], wpa_ref[:, sl], preferred_element_type=F32)
        yb = jnp.dot(fo_ref[...], wpb_ref[:, sl], preferred_element_type=F32)
        y = (jax.nn.sigmoid(ga_ref[:, sl].astype(F32)) * ya
             + jax.nn.sigmoid(gb_ref[:, sl].astype(F32)) * yb)
        y_ref[:, sl] = y.astype(BF16)
    for c in range(d // tc):
        sl = slice(c * tc, (c + 1) * tc)
        o = jnp.dot(y_ref[...], wout_ref[:, sl], preferred_element_type=F32)
        o_ref[:, sl] = x_ref[:, sl] + mod_ref[2:3, sl] * o


def _merge_out(layer, x, a, fo, proj, mods, w_pa, w_pb, w_out, *, rows_per_mod, mod_row0):
    m, d = x.shape
    attn, fd = a.shape[1], fo.shape[-1]
    tm = min(TM_MERGE, m)
    assert m % tm == 0 and (3 * attn + fd) % d == 0 and rows_per_mod % tm == 0
    gate_blk0 = (3 * attn + fd) // d
    whole = lambda arr: pl.BlockSpec((None,) + arr.shape[1:], lambda i, l: (l[0], 0, 0),
                                     pipeline_mode=pl.Buffered(1))
    if fo.ndim == 2:
        fo_spec = pl.BlockSpec((tm, fd), lambda i, l: (i, 0))
    else:
        n_tiles = fo.shape[1]
        assert fo.shape[2:4] == (2, tm) and fo.shape[0] * 2 * n_tiles * tm == m

        def folded(i, l):
            j = i % (2 * n_tiles)
            upper = j >= n_tiles
            return (i // (2 * n_tiles), jnp.where(upper, 2 * n_tiles - 1 - j, j), upper.astype(jnp.int32), 0, 0)

        fo_spec = pl.BlockSpec((None, None, None, tm, fd), folded)
    return pl.pallas_call(
        functools.partial(_merge_out_kernel, tc=min(TC_MERGE, d)),
        grid_spec=pltpu.PrefetchScalarGridSpec(
            num_scalar_prefetch=1, grid=(m // tm,),
            in_specs=[pl.BlockSpec((tm, d), lambda i, l: (i, 0)),
                      pl.BlockSpec((tm, attn), lambda i, l: (i, 0)),
                      fo_spec,
                      pl.BlockSpec((tm, d), lambda i, l: (i, gate_blk0)),
                      pl.BlockSpec((tm, d), lambda i, l: (i, gate_blk0 + 1)),
                      pl.BlockSpec((None, None, N_MOD, d),
                                   lambda i, l: (l[0], mod_row0 + (i * tm) // rows_per_mod, 0, 0)),
                      whole(w_pa), whole(w_pb), whole(w_out)],
            out_specs=pl.BlockSpec((tm, d), lambda i, l: (i, 0)),
            scratch_shapes=[pltpu.VMEM((tm, d), BF16)]),
        out_shape=jax.ShapeDtypeStruct((m, d), F32),
        compiler_params=_cparams(1),
        name="merge_out",
    )(_layer_arg(layer), x, a, fo, proj, proj, mods, w_pa, w_pb, w_out)


def _ffn_kernel(l_ref, x_ref, xp_ref, xn_ref, mod_ref, g_ref, wa_ref, wg_ref, cwa_ref, cwg_ref, cba_ref,
                cbg_ref, wd_ref, gf_ref, o_ref, h_ref, u_ref, *maybe_acc, seq, tc, final_norm):
    tm = x_ref.shape[0]
    halo = xp_ref.shape[0]
    tf = wa_ref.shape[1]
    seg = min(seq, tm)
    n_seg, stride = tm // seg, seg + halo
    work_rows = h_ref.shape[0]
    pad = u_ref.shape[2] - work_rows
    last_halo = work_rows - halo
    acc_ref = maybe_acc[0] if maybe_acc else o_ref
    n_rows = acc_ref.shape[0]
    i, f = pl.program_id(0), pl.program_id(1)

    @pl.when(f == 0)
    def _():
        g, scale, shift = g_ref[...], mod_ref[4:5, :], mod_ref[3:4, :]
        gs = g * (1.0 + scale)
        tok0 = i * tm
        for s in range(n_seg):
            _norm_mod_store(h_ref, s * stride, x_ref, g, scale, shift, src_row0=s * seg, n=seg)
            if s < n_seg - 1:
                h_ref[s * stride + seg:(s + 1) * stride, :] = jnp.zeros((halo, h_ref.shape[1]), BF16)
        after = jnp.where(((tok0 + tm) & (seq - 1)) != 0, _norm_mod_rows(xn_ref[...], gs, shift)[0:1, :], 0.0)
        before = jnp.where((tok0 & (seq - 1)) != 0, _norm_mod_rows(xp_ref[...], gs, shift)[halo - 1:halo, :], 0.0)
        r = lax.broadcasted_iota(jnp.int32, (halo, 1), 0)
        h_ref[last_halo:, :] = jnp.where(r == 0, after, jnp.where(r == 1, before, 0.0)).astype(BF16)
        acc_ref[...] = jnp.zeros_like(acc_ref)

    h = h_ref[...]

    def conv(c, idx, cw_ref, cb_ref, cs):
        return (u_ref[c, idx, pl.ds(pad - 1, n_rows), :] * cw_ref[0:1, cs]
                + u_ref[c, idx, pl.ds(pad, n_rows), :] * cw_ref[1:2, cs]
                + u_ref[c, idx, pl.ds(pad + 1, n_rows), :] * cw_ref[2:3, cs] + cb_ref[:, cs])

    def up_project(c, idx, w_ref, cs):
        res = jnp.dot(h, w_ref[:, cs], preferred_element_type=F32)
        u_ref[c, idx, pl.ds(pad, work_rows), :] = res
        u_ref[c, idx, pad - 1:pad, :] = res[last_halo + 1:last_halo + 2, :]

    n_chunks = tf // tc
    for c in range(n_chunks):
        cs = slice(c * tc, (c + 1) * tc)
        up_project(c, 0, wa_ref, cs)
        up_project(c, 1, wg_ref, cs)
    for c in range(n_chunks):
        cs = slice(c * tc, (c + 1) * tc)
        ca = conv(c, 0, cwa_ref, cba_ref, cs)
        cg = conv(c, 1, cwg_ref, cbg_ref, cs)
        act = ((cg * jax.nn.sigmoid(cg)) * ca).astype(BF16)
        acc_ref[...] += jnp.dot(act, wd_ref[cs, :], preferred_element_type=F32)

    @pl.when(f == pl.num_programs(1) - 1)
    def _():
        for s in range(n_seg):
            rows = slice(s * seg, (s + 1) * seg)
            y = x_ref[rows, :] + mod_ref[5:6, :] * acc_ref[s * stride:s * stride + seg, :]
            if final_norm:
                y = (y * lax.rsqrt(jnp.mean(y * y, axis=-1, keepdims=True) + RMS_EPS)) * gf_ref[...]
            o_ref[rows, :] = y


def _conv_ffn(layer, x, mods, g, w_up, conv_w, conv_b, w_down, final_gain, *, seq, rows_per_mod, mod_row0,
              final_norm):
    m, d = x.shape
    ffn = w_down.shape[1]
    tm, tf = min(TM_FFN, m), min(TF_FFN, ffn)
    tc = min(TC_FFN, tf)
    halo = V7X_BF16_SUBLANES
    assert seq & (seq - 1) == 0 and (seq % tm == 0 or tm % seq == 0) and m % seq == 0
    assert m % tm == 0 and ffn % tf == 0 and tf % tc == 0 and min(seq, tm) % halo == 0 and rows_per_mod % tm == 0
    nf, nhb = ffn // tf, m // halo
    n_seg = tm // min(seq, tm)
    work_rows = tm + n_seg * halo
    acc = [pltpu.VMEM((work_rows - halo, d), F32)] if n_seg > 1 else []
    mat = lambda rows, cols, imap: pl.BlockSpec((None, rows, cols), imap)
    return pl.pallas_call(
        functools.partial(_ffn_kernel, seq=seq, tc=tc, final_norm=final_norm),
        grid_spec=pltpu.PrefetchScalarGridSpec(
            num_scalar_prefetch=1, grid=(m // tm, nf),
            in_specs=[pl.BlockSpec((tm, d), lambda i, f, l: (i, 0)),
                      pl.BlockSpec((halo, d), lambda i, f, l: (jnp.maximum(i * (tm // halo) - 1, 0), 0)),
                      pl.BlockSpec((halo, d),
                                   lambda i, f, l: (jnp.minimum((i + 1) * (tm // halo), nhb - 1), 0)),
                      pl.BlockSpec((None, None, N_MOD, d),
                                   lambda i, f, l: (l[0], mod_row0 + (i * tm) // rows_per_mod, 0, 0)),
                      mat(1, d, lambda i, f, l: (l[0], 0, 0)),
                      mat(d, tf, lambda i, f, l: (l[0], 0, f)),
                      mat(d, tf, lambda i, f, l: (l[0], 0, nf + f)),
                      mat(3, tf, lambda i, f, l: (l[0], 0, f)),
                      mat(3, tf, lambda i, f, l: (l[0], 0, nf + f)),
                      mat(1, tf, lambda i, f, l: (l[0], 0, f)),
                      mat(1, tf, lambda i, f, l: (l[0], 0, nf + f)),
                      mat(tf, d, lambda i, f, l: (l[0], f, 0)),
                      pl.BlockSpec((1, d), lambda i, f, l: (0, 0))],
            out_specs=pl.BlockSpec((tm, d), lambda i, f, l: (i, 0)),
            scratch_shapes=[pltpu.VMEM((work_rows, d), BF16),
                            pltpu.VMEM((tf // tc, 2, V7X_F32_SUBLANES + work_rows, tc), F32)] + acc),
        out_shape=jax.ShapeDtypeStruct((m, d), F32),
        compiler_params=_cparams(2),
        name="conv_ffn_final" if final_norm else "conv_ffn",
    )(_layer_arg(layer), x, x, x, mods, g.reshape(g.shape[0], 1, d), w_up, w_up, conv_w, conv_w,
      conv_b.reshape(conv_b.shape[0], 1, -1), conv_b.reshape(conv_b.shape[0], 1, -1), w_down,
      final_gain.reshape(1, d))


def kernel(x_prompt, x_sample, cache_k, cache_v, c, c_ctx, w_mod, b_mod, norm1_g, w_in, rpb, w_pa, w_pb,
           w_out, norm2_g, w_up, conv_w, conv_b, w_down, norm_f_g):
    n_ctx, seq, d = x_prompt.shape
    n_lat, n_tok, _ = x_sample.shape
    n_layers = w_mod.shape[0]
    assert n_lat < MOD_ROWS and n_layers >= 1

    c_all = jnp.zeros((MOD_ROWS, d), F32).at[:n_lat].set(c).at[n_lat].set(c_ctx)
    mods = _modulation(c_all, w_mod, b_mod).reshape(n_layers, MOD_ROWS, N_MOD, d)
    w_in, w_pa, w_pb, w_out, w_up, w_down = (w.astype(BF16) for w in (w_in, w_pa, w_pb, w_out, w_up, w_down))

    xp = x_prompt.reshape(n_ctx * seq, d)
    xs = x_sample.reshape(n_lat * n_tok, d)
    ctx_mod = dict(rows_per_mod=n_ctx * seq, mod_row0=n_lat)
    lat_mod = dict(rows_per_mod=n_tok, mod_row0=0)
    caches = (jnp.zeros((n_ctx, n_layers, N_HEADS, seq, HEAD_DIM), F32),) * 2
    for l in range(n_layers):
        last = l == n_layers - 1
        ffn_w = (mods, norm2_g, w_up, conv_w, conv_b, w_down, norm_f_g)
        proj, *caches = _in_proj(l, xp, mods, norm1_g, w_in, caches=caches, **ctx_mod)
        a = _ctx_attention(proj, n_ctx, seq)
        fo = _fourier(proj, n_ctx, seq)
        xp = _merge_out(l, xp, a, fo, proj, mods, w_pa, w_pb, w_out, **ctx_mod)
        xp = _conv_ffn(l, xp, *ffn_w, seq=seq, final_norm=last, **ctx_mod)

        (proj,) = _in_proj(l, xs, mods, norm1_g, w_in, **lat_mod)
        a = _nbr_attention(l, proj, cache_k, cache_v, rpb, n_lat, n_tok)
        fo = _fourier(proj, n_lat, n_tok)
        xs = _merge_out(l, xs, a, fo, proj, mods, w_pa, w_pb, w_out, **lat_mod)
        xs = _conv_ffn(l, xs, *ffn_w, seq=n_tok, final_norm=last, **lat_mod)

    return (xp.reshape(n_ctx, seq, d), xs.reshape(n_lat, n_tok, d), caches[0], caches[1])
```

```python
import functools
import math

import numpy as np
import jax
import jax.numpy as jnp
from jax import lax
from jax.experimental import pallas as pl
from jax.experimental.pallas import tpu as pltpu

F32 = jnp.float32
BF16 = jnp.bfloat16

RMS_EPS = 1e-6
LOG2_E = math.log2(math.e)
N_MOD = 6
N_HEADS = 8
HEAD_DIM = 128
GRID_W = 64
NA_KH = 8
NA_KW = 16
F_GROUPS = 4
NBR_ROWS_PER_GROUP = 4
MOD_ROWS = 8
V7X_LANES = 128
V7X_F32_SUBLANES = 8
V7X_BF16_SUBLANES = 16
V7X_VMEM_BYTES = 64 * 1024 * 1024
VMEM_LIMIT = V7X_VMEM_BYTES - 8 * 1024 * 1024

TM_IN, TN_IN, TN_IN_WIDE = 1024, 1024, 2048
TN_MOD = 1024
TM_MERGE, TC_MERGE = 512, 512
TM_FFN, TF_FFN, TC_FFN = 512, 512, 256
TM_FOURIER = 256
NORM_ROW_CHUNK = 16
NORM_UNROLL = 8


def _cparams(n_axes):
    return pltpu.CompilerParams(dimension_semantics=("arbitrary",) * n_axes,
                                vmem_limit_bytes=VMEM_LIMIT)


def _layer_arg(layer):
    return jnp.full((1,), layer, jnp.int32)


def _norm_mod_rows(x, gs, shift):
    return (x * lax.rsqrt(jnp.mean(x * x, axis=-1, keepdims=True) + RMS_EPS)) * gs + shift


def _norm_mod_store(dst_ref, dst_row0, src_ref, g, scale, shift, src_row0=0, n=None):
    n = src_ref.shape[0] if n is None else n
    chunk = min(NORM_ROW_CHUNK, n)
    assert n % chunk == 0 and src_row0 % chunk == 0 and dst_row0 % chunk == 0
    gs = g * (1.0 + scale)

    def rows(r, carry):
        r0 = pl.multiple_of(r * chunk, chunk)
        h = _norm_mod_rows(src_ref[pl.ds(src_row0 + r0, chunk), :], gs, shift)
        dst_ref[pl.ds(dst_row0 + r0, chunk), :] = h.astype(BF16)
        return carry

    lax.fori_loop(0, n // chunk, rows, 0, unroll=min(NORM_UNROLL, n // chunk))


def _mod_kernel(c_ref, w_ref, b_ref, o_ref):
    cv = c_ref[...]
    s = (cv * jax.nn.sigmoid(cv)).astype(BF16)
    o_ref[...] = jnp.dot(s, w_ref[...].astype(BF16), preferred_element_type=F32) + b_ref[...]


def _modulation(c_all, w_mod, b_mod):
    n_layers, d, nd = w_mod.shape
    tn = min(TN_MOD, nd)
    assert nd % tn == 0
    return pl.pallas_call(
        _mod_kernel,
        grid=(n_layers, nd // tn),
        in_specs=[pl.BlockSpec((MOD_ROWS, d), lambda l, j: (0, 0)),
                  pl.BlockSpec((None, d, tn), lambda l, j: (l, 0, j)),
                  pl.BlockSpec((None, 1, tn), lambda l, j: (l, 0, j))],
        out_specs=pl.BlockSpec((None, MOD_ROWS, tn), lambda l, j: (l, 0, j)),
        out_shape=jax.ShapeDtypeStruct((n_layers, MOD_ROWS, nd), F32),
        compiler_params=_cparams(2),
        name="modulation",
    )(c_all, w_mod, b_mod.reshape(n_layers, 1, nd))


def _in_proj_kernel(l_ref, x_ref, mod_ref, g_ref, w_ref, *rest, kv_tiles):
    h_ref = rest[-1]
    j = pl.program_id(1)

    @pl.when(j == 0)
    def _():
        _norm_mod_store(h_ref, 0, x_ref, g_ref[...], mod_ref[1:2, :], mod_ref[0:1, :])

    res = jnp.dot(h_ref[...], w_ref[...], preferred_element_type=F32)
    if kv_tiles is None:
        o_ref = rest[0]
        o_ref[...] = res.astype(o_ref.dtype)
    else:
        o_ref, kc_ref, vc_ref = rest[-4], rest[-3], rest[-2]
        o_ref[...] = res.astype(o_ref.dtype)
        k0, v0, v1 = kv_tiles
        nb, nh, s, dh = kc_ref.shape

        def scatter(dst_ref):
            for bb in range(nb):
                for hh in range(nh):
                    dst_ref[bb, hh] = res[bb * s:(bb + 1) * s, hh * dh:(hh + 1) * dh]

        @pl.when((j >= k0) & (j < v0))
        def _():
            scatter(kc_ref)

        @pl.when((j >= v0) & (j < v1))
        def _():
            scatter(vc_ref)


def _in_proj(layer, x, mods, g, w, *, rows_per_mod, mod_row0, caches=None):
    m, d = x.shape
    n_out = w.shape[2]
    wide = caches is None and n_out % TN_IN_WIDE == 0
    tm, tn = min(TM_IN, m), min(TN_IN_WIDE if wide else TN_IN, n_out)
    assert m % tm == 0 and n_out % tn == 0 and (rows_per_mod % tm == 0 or tm % rows_per_mod == 0)
    attn = N_HEADS * HEAD_DIM
    in_specs = [pl.BlockSpec((tm, d), lambda i, j, l: (i, 0)),
                pl.BlockSpec((None, None, N_MOD, d),
                             lambda i, j, l: (l[0], mod_row0 + (i * tm) // rows_per_mod, 0, 0)),
                pl.BlockSpec((None, 1, d), lambda i, j, l: (l[0], 0, 0)),
                pl.BlockSpec((None, d, tn), lambda i, j, l: (l[0], 0, j))]
    out_specs = [pl.BlockSpec((tm, tn), lambda i, j, l: (i, j))]
    out_shape = [jax.ShapeDtypeStruct((m, n_out), BF16)]
    args = [_layer_arg(layer), x, mods, g.reshape(g.shape[0], 1, d), w]
    kv_tiles, aliases = None, {}
    if caches is not None:
        seq = caches[0].shape[3]
        assert tm % seq == 0 and tn % HEAD_DIM == 0 and attn % tn == 0
        k0, v0, v1 = attn // tn, 2 * attn // tn, 3 * attn // tn
        kv_tiles = (k0, v0, v1)
        blk = (tm // seq, None, tn // HEAD_DIM, seq, HEAD_DIM)
        out_specs += [pl.BlockSpec(blk, lambda i, j, l: (i, l[0], jnp.clip(j - k0, 0, v0 - k0 - 1), 0, 0)),
                      pl.BlockSpec(blk, lambda i, j, l: (i, l[0], jnp.clip(j - v0, 0, v1 - v0 - 1), 0, 0))]
        out_shape += [jax.ShapeDtypeStruct(c.shape, c.dtype) for c in caches]
        in_specs += [pl.BlockSpec(memory_space=pl.ANY)] * 2
        aliases = {len(args): 1, len(args) + 1: 2}
        args += list(caches)
    return pl.pallas_call(
        functools.partial(_in_proj_kernel, kv_tiles=kv_tiles),
        grid_spec=pltpu.PrefetchScalarGridSpec(
            num_scalar_prefetch=1, grid=(m // tm, n_out // tn), in_specs=in_specs, out_specs=out_specs,
            scratch_shapes=[pltpu.VMEM((tm, d), BF16)]),
        out_shape=out_shape,
        input_output_aliases=aliases,
        compiler_params=_cparams(2),
        name="in_proj_ctx" if caches is not None else "in_proj_lat",
    )(*args)


def _ctx_attn_kernel(q_ref, k_ref, v_ref, o_ref, *, scale):
    for h in range(N_HEADS):
        sl = slice(h * HEAD_DIM, (h + 1) * HEAD_DIM)
        s = lax.dot_general(q_ref[:, sl], k_ref[:, sl], (((1,), (1,)), ((), ())),
                            preferred_element_type=F32) * scale
        e = jnp.exp(s - jnp.max(s, axis=-1, keepdims=True))
        p = e * (1.0 / jnp.sum(e, axis=-1, keepdims=True))
        o_ref[:, sl] = jnp.dot(p.astype(BF16), v_ref[:, sl], preferred_element_type=F32).astype(o_ref.dtype)


def _ctx_attention(proj, n_seq, seq):
    attn = N_HEADS * HEAD_DIM
    return pl.pallas_call(
        functools.partial(_ctx_attn_kernel, scale=HEAD_DIM ** -0.5),
        grid=(n_seq,),
        in_specs=[pl.BlockSpec((seq, attn), lambda b: (b, 0)),
                  pl.BlockSpec((seq, attn), lambda b: (b, 1)),
                  pl.BlockSpec((seq, attn), lambda b: (b, 2))],
        out_specs=pl.BlockSpec((seq, attn), lambda b: (b, 0)),
        out_shape=jax.ShapeDtypeStruct((n_seq * seq, attn), BF16),
        compiler_params=_cparams(1),
        name="ctx_attention",
    )(proj, proj, proj)


def _nbr_layout(rows):
    r_grp, win = NBR_ROWS_PER_GROUP, NA_KH + NBR_ROWS_PER_GROUP
    types, type_of_group = [], []
    for g in range(rows // r_grp):
        w0 = min(max(r_grp * g - NA_KH // 2, 0), rows - win)
        desc = []
        for qi in range(r_grp):
            r = r_grp * g + qi
            r0 = min(max(r - NA_KH // 2, 0), rows - NA_KH)
            desc.append(tuple((w0 + kj - r) if r0 <= w0 + kj < r0 + NA_KH else None for kj in range(win)))
        desc = tuple(desc)
        if desc not in types:
            types.append(desc)
        type_of_group.append(types.index(desc))
    pairs = sorted({(d[qi][2 * m], d[qi][2 * m + 1]) for d in types for qi in range(r_grp)
                    for m in range(win // 2)} - {(None, None)},
                   key=lambda p: tuple(-99 if v is None else v for v in p))
    return types, type_of_group, pairs


def _nbr_attn_kernel(l_ref, rpb_ref, q_ref, k_ref, v_ref, kc_ref, vc_ref, o_ref, bias_ref, pair_ref, *,
                     rows, layout, scale):
    types, type_of_group, pairs = layout
    w = GRID_W
    r_grp, win = NBR_ROWS_PER_GROUP, NA_KH + NBR_ROWS_PER_GROUP
    gq, wk = r_grp * w, win * w
    n_dr, n_dc = 2 * NA_KH - 1, 2 * NA_KW - 1
    neg_inf = float("-inf")
    h = pl.program_id(0)
    layer = l_ref[0]

    @pl.when(pl.program_id(1) == 0)
    def _build_bias_tables():
        cq = lax.broadcasted_iota(jnp.int32, (w, V7X_LANES), 0)
        lane = lax.broadcasted_iota(jnp.int32, (w, V7X_LANES), 1)
        ck = lane & (w - 1)
        c0 = jnp.clip(cq - NA_KW // 2, 0, w - NA_KW)
        in_win = (ck >= c0) & (ck < c0 + NA_KW)
        dc_idx = jnp.where(in_win, jnp.clip(ck - cq, -(NA_KW - 1), NA_KW - 1) + (NA_KW - 1), -1)
        left = lane < w
        for p in range(len(pairs)):
            pair_ref[p] = jnp.full((w, V7X_LANES), neg_inf, F32)

        def per_dc(dc, carry):
            hit = dc_idx == dc
            for p, (da, db) in enumerate(pairs):
                va = neg_inf if da is None else rpb_ref[layer, (h * n_dr + da + NA_KH - 1) * n_dc + dc]
                vb = neg_inf if db is None else rpb_ref[layer, (h * n_dr + db + NA_KH - 1) * n_dc + dc]
                pair_ref[p] = jnp.where(hit, jnp.where(left, va, vb) * LOG2_E, pair_ref[p])
            return carry

        lax.fori_loop(0, n_dc, per_dc, 0)
        for t, desc in enumerate(types):
            for qi in range(r_grp):
                for m in range(win // 2):
                    pr = (desc[qi][2 * m], desc[qi][2 * m + 1])
                    tile = (jnp.full((w, V7X_LANES), neg_inf, F32) if pr == (None, None)
                            else pair_ref[pairs.index(pr)])
                    bias_ref[t, qi * w:(qi + 1) * w, m * V7X_LANES:(m + 1) * V7X_LANES] = tile

    kc = kc_ref[...].astype(BF16)
    vc = vc_ref[...].astype(BF16)
    nt = (((1,), (1,)), ((), ()))
    scale2 = scale * LOG2_E
    default_type = max(set(type_of_group), key=type_of_group.count)

    def scores(g):
        w0 = jnp.clip(r_grp * g - NA_KH // 2, 0, rows - win)
        q0 = pl.multiple_of(g * gq, gq)
        k0 = pl.multiple_of(w0 * w, w)
        t = jnp.int32(default_type)
        for gi, ti in enumerate(type_of_group):
            if ti != default_type:
                t = jnp.where(g == gi, ti, t)
        q = q_ref[pl.ds(q0, gq), :]
        s = lax.dot_general(q, k_ref[pl.ds(k0, wk), :], nt, preferred_element_type=F32) * scale2 + bias_ref[t]
        sc = lax.dot_general(q, kc, nt, preferred_element_type=F32) * scale2
        return q0, k0, s, sc

    def attend(q0, k0, s, sc):
        mx = jnp.maximum(jnp.max(s, axis=-1, keepdims=True), jnp.max(sc, axis=-1, keepdims=True))
        e = jnp.exp2(s - mx)
        ec = jnp.exp2(sc - mx)
        inv = 1.0 / (jnp.sum(e, axis=-1, keepdims=True) + jnp.sum(ec, axis=-1, keepdims=True))
        o = (jnp.dot(e.astype(BF16), v_ref[pl.ds(k0, wk), :], preferred_element_type=F32)
             + jnp.dot(ec.astype(BF16), vc, preferred_element_type=F32))
        o_ref[pl.ds(q0, gq), :] = (o * inv).astype(o_ref.dtype)

    def per_pair(i, carry):
        first, second = scores(2 * i), scores(2 * i + 1)
        attend(*first)
        attend(*second)
        return carry

    lax.fori_loop(0, rows // (2 * r_grp), per_pair, 0, unroll=4)


def _nbr_attention(layer, proj, cache_k, cache_v, rpb, n_batch, n_tok):
    assert 2 * GRID_W == V7X_LANES and n_tok % GRID_W == 0
    rows = n_tok // GRID_W
    assert rows % (2 * NBR_ROWS_PER_GROUP) == 0 and rows >= NA_KH + NBR_ROWS_PER_GROUP and GRID_W >= NA_KW
    layout = _nbr_layout(rows)
    r_grp, win = NBR_ROWS_PER_GROUP, NA_KH + NBR_ROWS_PER_GROUP
    t_ctx = cache_k.shape[3]
    blk = lambda col0: pl.BlockSpec((n_tok, HEAD_DIM), lambda h, b, l: (b, col0 + h))
    ctx = pl.BlockSpec((None, None, None, t_ctx, HEAD_DIM), lambda h, b, l: (b, l[0], h, 0, 0))
    return pl.pallas_call(
        functools.partial(_nbr_attn_kernel, rows=rows, layout=layout, scale=HEAD_DIM ** -0.5),
        grid_spec=pltpu.PrefetchScalarGridSpec(
            num_scalar_prefetch=1, grid=(N_HEADS, n_batch),
            in_specs=[pl.BlockSpec(memory_space=pltpu.SMEM),
                      blk(0), blk(N_HEADS), blk(2 * N_HEADS), ctx, ctx],
            out_specs=pl.BlockSpec((n_tok, HEAD_DIM), lambda h, b, l: (b, h)),
            scratch_shapes=[pltpu.VMEM((len(layout[0]), r_grp * GRID_W, win * GRID_W), F32),
                            pltpu.VMEM((len(layout[2]), GRID_W, V7X_LANES), F32)]),
        out_shape=jax.ShapeDtypeStruct((n_batch * n_tok, N_HEADS * HEAD_DIM), BF16),
        compiler_params=_cparams(2),
        name="nbr_attention",
    )(_layer_arg(layer), rpb.reshape(rpb.shape[0], -1), proj, proj, proj, cache_k, cache_v)


@functools.lru_cache(maxsize=None)
def _dft_tables(n):
    jk = (np.arange(n, dtype=np.int64)[:, None] * np.arange(n, dtype=np.int64)[None, :]) % n
    ang = 2.0 * np.pi * jk.astype(np.float64) / n
    scale = 1.0 / math.sqrt(n)
    return ((np.cos(ang) * scale).astype(BF16), (np.sin(ang) * scale).astype(BF16))


@functools.lru_cache(maxsize=None)
def _mirror_matrix(t):
    r = np.zeros((t, t), np.float32)
    q = np.arange(1, t)
    r[q, t - q] = 1.0
    return r.astype(BF16)


def _fourier_channel_stage(f_ref, cc_ref, sc_ref, rev_ref, zc_ref, zs_ref, edge_ref):
    n, fd = f_ref.shape
    gd = fd // F_GROUPS
    half = n // 2
    tile = rev_ref.shape[0]
    n_tiles = half // tile
    for t in range(n_tiles):
        rows = slice(t * tile, (t + 1) * tile)
        lo = f_ref[rows, :].astype(F32)
        src = (2 * n_tiles - 1 - t) * tile
        mir = jnp.dot(rev_ref[...], f_ref[src:src + tile, :], preferred_element_type=F32)
        if t > 0:
            first = lax.broadcasted_iota(jnp.int32, (tile, 1), 0) == 0
            mir = jnp.where(first, f_ref[src + tile:src + tile + 1, :].astype(F32), mir)
        x_even = (lo + mir).astype(BF16)
        x_odd = (lo - mir).astype(BF16)
        for g in range(F_GROUPS):
            sl = slice(g * gd, (g + 1) * gd)
            zc_ref[rows, sl] = jnp.dot(x_even[:, sl], cc_ref[...], preferred_element_type=F32).astype(BF16)
            zs_ref[rows, sl] = jnp.dot(x_odd[:, sl], sc_ref[...], preferred_element_type=F32).astype(BF16)
    sign = 1.0 - 2.0 * (lax.broadcasted_iota(jnp.int32, (8, 1), 0) & 1).astype(F32)
    x_half = f_ref[half:half + V7X_BF16_SUBLANES, :]
    for g in range(F_GROUPS):
        sl = slice(g * gd, (g + 1) * gd)
        z_half = jnp.dot(x_half[:, sl], cc_ref[...], preferred_element_type=F32)[0:1, :]
        edge_ref[:, sl] = sign * (z_half * (1.0 / math.sqrt(n)))


def _fourier_kernel(f_ref, cc_ref, sc_ref, rev_ref, wc_ref, ws_ref, o_ref, zc_ref, zs_ref, edge_ref):
    fd = f_ref.shape[1]

    @pl.when(pl.program_id(1) == 0)
    def _():
        _fourier_channel_stage(f_ref, cc_ref, sc_ref, rev_ref, zc_ref, zs_ref, edge_ref)

    y = (jnp.dot(wc_ref[...], zc_ref[...], preferred_element_type=F32)
         + jnp.dot(ws_ref[...], zs_ref[...], preferred_element_type=F32))
    tm = y.shape[0]
    y = y.reshape(tm // 8, 8, fd) + edge_ref[...][None]
    o_ref[...] = y.reshape(tm, fd).astype(o_ref.dtype)


def _fourier_fold_kernel(f_ref, cc_ref, sc_ref, rev_ref, wc_ref, ws_ref, wh_ref, o_ref, zc_ref, zs_ref,
                         edge_ref, next_ref):
    fd = f_ref.shape[1]
    t_rows = wc_ref.shape[0]
    sub = rev_ref.shape[0]
    n_sub = t_rows // sub

    @pl.when(pl.program_id(1) == 0)
    def _():
        _fourier_channel_stage(f_ref, cc_ref, sc_ref, rev_ref, zc_ref, zs_ref, edge_ref)
        y_half = jnp.dot(wh_ref[...], zc_ref[...], preferred_element_type=F32)[0:1, :] + edge_ref[0:1, :]
        next_ref[...] = jnp.broadcast_to(y_half, next_ref.shape)

    e = jnp.dot(wc_ref[...], zc_ref[...], preferred_element_type=F32)
    e = (e.reshape(t_rows // 8, 8, fd) + edge_ref[...][None]).reshape(t_rows, fd)
    o = -jnp.dot(ws_ref[...], zs_ref[...], preferred_element_type=F32)
    o_ref[0] = (e - o).astype(o_ref.dtype)
    upper = e + o
    upper_bf = upper.astype(BF16)
    first = lax.broadcasted_iota(jnp.int32, (sub, 1), 0) == 0
    for j in range(n_sub):
        src = n_sub - 1 - j
        blk = jnp.dot(rev_ref[...], upper_bf[src * sub:(src + 1) * sub, :], preferred_element_type=F32)
        row0 = next_ref[0:1, :] if j == 0 else upper[(n_sub - j) * sub:(n_sub - j) * sub + 1, :]
        o_ref[1, j * sub:(j + 1) * sub, :] = jnp.where(first, row0, blk).astype(o_ref.dtype)
    next_ref[...] = jnp.broadcast_to(upper[0:1, :], next_ref.shape)


def _fourier(proj, n_seq, seq):
    attn = N_HEADS * HEAD_DIM
    fd = attn
    gd = fd // F_GROUPS
    cc, sc = _dft_tables(gd)
    cp, sp = _dft_tables(seq)
    half = seq // 2
    tm, tile = min(TM_FOURIER, seq), min(TM_FOURIER, half)
    assert seq % tm == 0 and half % tile == 0 and tile % V7X_BF16_SUBLANES == 0 and proj.shape[1] // fd > 3
    consts = (jnp.asarray(cc), jnp.asarray(sc), jnp.asarray(_mirror_matrix(tile)),
              jnp.asarray(cp[:, :half]), jnp.asarray(-sp[:, :half]))
    const_specs = [pl.BlockSpec((gd, gd), lambda b, i: (0, 0)),
                   pl.BlockSpec((gd, gd), lambda b, i: (0, 0)),
                   pl.BlockSpec((tile, tile), lambda b, i: (0, 0))]
    scratch = [pltpu.VMEM((half, fd), BF16), pltpu.VMEM((half, fd), BF16), pltpu.VMEM((8, fd), F32)]
    if half % TM_MERGE == 0:
        t_rows, n_tiles = TM_MERGE, half // TM_MERGE
        w_half = np.zeros((V7X_BF16_SUBLANES, half), np.float32)
        w_half[0] = cp[half, :half].astype(np.float32)
        rows = lambda b, i: (n_tiles - 1 - i, 0)
        return pl.pallas_call(
            _fourier_fold_kernel,
            grid=(n_seq, n_tiles),
            in_specs=[pl.BlockSpec((seq, fd), lambda b, i: (b, 3))] + const_specs
            + [pl.BlockSpec((t_rows, half), rows), pl.BlockSpec((t_rows, half), rows),
               pl.BlockSpec((V7X_BF16_SUBLANES, half), lambda b, i: (0, 0))],
            out_specs=pl.BlockSpec((None, None, 2, t_rows, fd), lambda b, i: (b, n_tiles - 1 - i, 0, 0, 0)),
            out_shape=jax.ShapeDtypeStruct((n_seq, n_tiles, 2, t_rows, fd), BF16),
            scratch_shapes=scratch + [pltpu.VMEM((8, fd), F32)],
            compiler_params=_cparams(2),
            name="fourier_fold_%d" % seq,
        )(proj, *consts, jnp.asarray(w_half.astype(BF16)))
    return pl.pallas_call(
        _fourier_kernel,
        grid=(n_seq, seq // tm),
        in_specs=[pl.BlockSpec((seq, fd), lambda b, i: (b, 3))] + const_specs
        + [pl.BlockSpec((tm, half), lambda b, i: (i, 0)), pl.BlockSpec((tm, half), lambda b, i: (i, 0))],
        out_specs=pl.BlockSpec((tm, fd), lambda b, i: (b * (seq // tm) + i, 0)),
        out_shape=jax.ShapeDtypeStruct((n_seq * seq, fd), BF16),
        scratch_shapes=scratch,
        compiler_params=_cparams(2),
        name="fourier_%d" % seq,
    )(proj, *consts)


def _merge_out_kernel(l_ref, x_ref, a_ref, fo_ref, ga_ref, gb_ref, mod_ref, wpa_ref, wpb_ref, wout_ref,
                      o_ref, y_ref, *, tc):
    d = x_ref.shape[1]
    for c in range(d // tc):
        sl = slice(c * tc, (c + 1) * tc)
        ya = jnp.dot(a_ref[...], wpa_ref[:, sl], preferred_element_type=F32)
        yb = jnp.dot(fo_ref[...], wpb_ref[:, sl], preferred_element_type=F32)
        y = (jax.nn.sigmoid(ga_ref[:, sl].astype(F32)) * ya
             + jax.nn.sigmoid(gb_ref[:, sl].astype(F32)) * yb)
        y_ref[:, sl] = y.astype(BF16)
    for c in range(d // tc):
        sl = slice(c * tc, (c + 1) * tc)
        o = jnp.dot(y_ref[...], wout_ref[:, sl], preferred_element_type=F32)
        o_ref[:, sl] = x_ref[:, sl] + mod_ref[2:3, sl] * o


def _merge_out(layer, x, a, fo, proj, mods, w_pa, w_pb, w_out, *, rows_per_mod, mod_row0):
    m, d = x.shape
    attn, fd = a.shape[1], fo.shape[-1]
    tm = min(TM_MERGE, m)
    assert m % tm == 0 and (3 * attn + fd) % d == 0 and rows_per_mod % tm == 0
    gate_blk0 = (3 * attn + fd) // d
    whole = lambda arr: pl.BlockSpec((None,) + arr.shape[1:], lambda i, l: (l[0], 0, 0),
                                     pipeline_mode=pl.Buffered(1))
    if fo.ndim == 2:
        fo_spec = pl.BlockSpec((tm, fd), lambda i, l: (i, 0))
    else:
        n_tiles = fo.shape[1]
        assert fo.shape[2:4] == (2, tm) and fo.shape[0] * 2 * n_tiles * tm == m

        def folded(i, l):
            j = i % (2 * n_tiles)
            upper = j >= n_tiles
            return (i // (2 * n_tiles), jnp.where(upper, 2 * n_tiles - 1 - j, j), upper.astype(jnp.int32), 0, 0)

        fo_spec = pl.BlockSpec((None, None, None, tm, fd), folded)
    return pl.pallas_call(
        functools.partial(_merge_out_kernel, tc=min(TC_MERGE, d)),
        grid_spec=pltpu.PrefetchScalarGridSpec(
            num_scalar_prefetch=1, grid=(m // tm,),
            in_specs=[pl.BlockSpec((tm, d), lambda i, l: (i, 0)),
                      pl.BlockSpec((tm, attn), lambda i, l: (i, 0)),
                      fo_spec,
                      pl.BlockSpec((tm, d), lambda i, l: (i, gate_blk0)),
                      pl.BlockSpec((tm, d), lambda i, l: (i, gate_blk0 + 1)),
                      pl.BlockSpec((None, None, N_MOD, d),
                                   lambda i, l: (l[0], mod_row0 + (i * tm) // rows_per_mod, 0, 0)),
                      whole(w_pa), whole(w_pb), whole(w_out)],
            out_specs=pl.BlockSpec((tm, d), lambda i, l: (i, 0)),
            scratch_shapes=[pltpu.VMEM((tm, d), BF16)]),
        out_shape=jax.ShapeDtypeStruct((m, d), F32),
        compiler_params=_cparams(1),
        name="merge_out",
    )(_layer_arg(layer), x, a, fo, proj, proj, mods, w_pa, w_pb, w_out)


def _ffn_kernel(l_ref, x_ref, xp_ref, xn_ref, mod_ref, g_ref, wa_ref, wg_ref, cwa_ref, cwg_ref, cba_ref,
                cbg_ref, wd_ref, gf_ref, o_ref, h_ref, u_ref, *maybe_acc, seq, tc, final_norm):
    tm = x_ref.shape[0]
    halo = xp_ref.shape[0]
    tf = wa_ref.shape[1]
    seg = min(seq, tm)
    n_seg, stride = tm // seg, seg + halo
    work_rows = h_ref.shape[0]
    pad = u_ref.shape[2] - work_rows
    last_halo = work_rows - halo
    acc_ref = maybe_acc[0] if maybe_acc else o_ref
    n_rows = acc_ref.shape[0]
    i, f = pl.program_id(0), pl.program_id(1)

    @pl.when(f == 0)
    def _():
        g, scale, shift = g_ref[...], mod_ref[4:5, :], mod_ref[3:4, :]
        gs = g * (1.0 + scale)
        tok0 = i * tm
        for s in range(n_seg):
            _norm_mod_store(h_ref, s * stride, x_ref, g, scale, shift, src_row0=s * seg, n=seg)
            if s < n_seg - 1:
                h_ref[s * stride + seg:(s + 1) * stride, :] = jnp.zeros((halo, h_ref.shape[1]), BF16)
        after = jnp.where(((tok0 + tm) & (seq - 1)) != 0, _norm_mod_rows(xn_ref[...], gs, shift)[0:1, :], 0.0)
        before = jnp.where((tok0 & (seq - 1)) != 0, _norm_mod_rows(xp_ref[...], gs, shift)[halo - 1:halo, :], 0.0)
        r = lax.broadcasted_iota(jnp.int32, (halo, 1), 0)
        h_ref[last_halo:, :] = jnp.where(r == 0, after, jnp.where(r == 1, before, 0.0)).astype(BF16)
        acc_ref[...] = jnp.zeros_like(acc_ref)

    h = h_ref[...]

    def conv(c, idx, cw_ref, cb_ref, cs):
        return (u_ref[c, idx, pl.ds(pad - 1, n_rows), :] * cw_ref[0:1, cs]
                + u_ref[c, idx, pl.ds(pad, n_rows), :] * cw_ref[1:2, cs]
                + u_ref[c, idx, pl.ds(pad + 1, n_rows), :] * cw_ref[2:3, cs] + cb_ref[:, cs])

    def up_project(c, idx, w_ref, cs):
        res = jnp.dot(h, w_ref[:, cs], preferred_element_type=F32)
        u_ref[c, idx, pl.ds(pad, work_rows), :] = res
        u_ref[c, idx, pad - 1:pad, :] = res[last_halo + 1:last_halo + 2, :]

    n_chunks = tf // tc
    for c in range(n_chunks):
        cs = slice(c * tc, (c + 1) * tc)
        up_project(c, 0, wa_ref, cs)
        up_project(c, 1, wg_ref, cs)
    for c in range(n_chunks):
        cs = slice(c * tc, (c + 1) * tc)
        ca = conv(c, 0, cwa_ref, cba_ref, cs)
        cg = conv(c, 1, cwg_ref, cbg_ref, cs)
        act = ((cg * jax.nn.sigmoid(cg)) * ca).astype(BF16)
        acc_ref[...] += jnp.dot(act, wd_ref[cs, :], preferred_element_type=F32)

    @pl.when(f == pl.num_programs(1) - 1)
    def _():
        for s in range(n_seg):
            rows = slice(s * seg, (s + 1) * seg)
            y = x_ref[rows, :] + mod_ref[5:6, :] * acc_ref[s * stride:s * stride + seg, :]
            if final_norm:
                y = (y * lax.rsqrt(jnp.mean(y * y, axis=-1, keepdims=True) + RMS_EPS)) * gf_ref[...]
            o_ref[rows, :] = y


def _conv_ffn(layer, x, mods, g, w_up, conv_w, conv_b, w_down, final_gain, *, seq, rows_per_mod, mod_row0,
              final_norm):
    m, d = x.shape
    ffn = w_down.shape[1]
    tm, tf = min(TM_FFN, m), min(TF_FFN, ffn)
    tc = min(TC_FFN, tf)
    halo = V7X_BF16_SUBLANES
    assert seq & (seq - 1) == 0 and (seq % tm == 0 or tm % seq == 0) and m % seq == 0
    assert m % tm == 0 and ffn % tf == 0 and tf % tc == 0 and min(seq, tm) % halo == 0 and rows_per_mod % tm == 0
    nf, nhb = ffn // tf, m // halo
    n_seg = tm // min(seq, tm)
    work_rows = tm + n_seg * halo
    acc = [pltpu.VMEM((work_rows - halo, d), F32)] if n_seg > 1 else []
    mat = lambda rows, cols, imap: pl.BlockSpec((None, rows, cols), imap)
    return pl.pallas_call(
        functools.partial(_ffn_kernel, seq=seq, tc=tc, final_norm=final_norm),
        grid_spec=pltpu.PrefetchScalarGridSpec(
            num_scalar_prefetch=1, grid=(m // tm, nf),
            in_specs=[pl.BlockSpec((tm, d), lambda i, f, l: (i, 0)),
                      pl.BlockSpec((halo, d), lambda i, f, l: (jnp.maximum(i * (tm // halo) - 1, 0), 0)),
                      pl.BlockSpec((halo, d),
                                   lambda i, f, l: (jnp.minimum((i + 1) * (tm // halo), nhb - 1), 0)),
                      pl.BlockSpec((None, None, N_MOD, d),
                                   lambda i, f, l: (l[0], mod_row0 + (i * tm) // rows_per_mod, 0, 0)),
                      mat(1, d, lambda i, f, l: (l[0], 0, 0)),
                      mat(d, tf, lambda i, f, l: (l[0], 0, f)),
                      mat(d, tf, lambda i, f, l: (l[0], 0, nf + f)),
                      mat(3, tf, lambda i, f, l: (l[0], 0, f)),
                      mat(3, tf, lambda i, f, l: (l[0], 0, nf + f)),
                      mat(1, tf, lambda i, f, l: (l[0], 0, f)),
                      mat(1, tf, lambda i, f, l: (l[0], 0, nf + f)),
                      mat(tf, d, lambda i, f, l: (l[0], f, 0)),
                      pl.BlockSpec((1, d), lambda i, f, l: (0, 0))],
            out_specs=pl.BlockSpec((tm, d), lambda i, f, l: (i, 0)),
            scratch_shapes=[pltpu.VMEM((work_rows, d), BF16),
                            pltpu.VMEM((tf // tc, 2, V7X_F32_SUBLANES + work_rows, tc), F32)] + acc),
        out_shape=jax.ShapeDtypeStruct((m, d), F32),
        compiler_params=_cparams(2),
        name="conv_ffn_final" if final_norm else "conv_ffn",
    )(_layer_arg(layer), x, x, x, mods, g.reshape(g.shape[0], 1, d), w_up, w_up, conv_w, conv_w,
      conv_b.reshape(conv_b.shape[0], 1, -1), conv_b.reshape(conv_b.shape[0], 1, -1), w_down,
      final_gain.reshape(1, d))


def kernel(x_prompt, x_sample, cache_k, cache_v, c, c_ctx, w_mod, b_mod, norm1_g, w_in, rpb, w_pa, w_pb,
           w_out, norm2_g, w_up, conv_w, conv_b, w_down, norm_f_g):
    n_ctx, seq, d = x_prompt.shape
    n_lat, n_tok, _ = x_sample.shape
    n_layers = w_mod.shape[0]
    assert n_lat < MOD_ROWS and n_layers >= 1

    c_all = jnp.zeros((MOD_ROWS, d), F32).at[:n_lat].set(c).at[n_lat].set(c_ctx)
    mods = _modulation(c_all, w_mod, b_mod).reshape(n_layers, MOD_ROWS, N_MOD, d)
    w_in, w_pa, w_pb, w_out, w_up, w_down = (w.astype(BF16) for w in (w_in, w_pa, w_pb, w_out, w_up, w_down))

    xp = x_prompt.reshape(n_ctx * seq, d)
    xs = x_sample.reshape(n_lat * n_tok, d)
    ctx_mod = dict(rows_per_mod=n_ctx * seq, mod_row0=n_lat)
    lat_mod = dict(rows_per_mod=n_tok, mod_row0=0)
    caches = (jnp.zeros((n_ctx, n_layers, N_HEADS, seq, HEAD_DIM), F32),) * 2
    for l in range(n_layers):
        last = l == n_layers - 1
        ffn_w = (mods, norm2_g, w_up, conv_w, conv_b, w_down, norm_f_g)
        proj, *caches = _in_proj(l, xp, mods, norm1_g, w_in, caches=caches, **ctx_mod)
        a = _ctx_attention(proj, n_ctx, seq)
        fo = _fourier(proj, n_ctx, seq)
        xp = _merge_out(l, xp, a, fo, proj, mods, w_pa, w_pb, w_out, **ctx_mod)
        xp = _conv_ffn(l, xp, *ffn_w, seq=seq, final_norm=last, **ctx_mod)

        (proj,) = _in_proj(l, xs, mods, norm1_g, w_in, **lat_mod)
        a = _nbr_attention(l, proj, cache_k, cache_v, rpb, n_lat, n_tok)
        fo = _fourier(proj, n_lat, n_tok)
        xs = _merge_out(l, xs, a, fo, proj, mods, w_pa, w_pb, w_out, **lat_mod)
        xs = _conv_ffn(l, xs, *ffn_w, seq=n_tok, final_norm=last, **lat_mod)

    return (xp.reshape(n_ctx, seq, d), xs.reshape(n_lat, n_tok, d), caches[0], caches[1])
```

```python
import functools
import math

import numpy as np
import jax
import jax.numpy as jnp
from jax import lax
from jax.experimental import pallas as pl
from jax.experimental.pallas import tpu as pltpu

F32 = jnp.float32
BF16 = jnp.bfloat16

RMS_EPS = 1e-6
LOG2_E = math.log2(math.e)
N_MOD = 6
N_HEADS = 8
HEAD_DIM = 128
GRID_W = 64
NA_KH = 8
NA_KW = 16
F_GROUPS = 4
NBR_ROWS_PER_GROUP = 4
MOD_ROWS = 8
V7X_LANES = 128
V7X_F32_SUBLANES = 8
V7X_BF16_SUBLANES = 16
V7X_VMEM_BYTES = 64 * 1024 * 1024
VMEM_LIMIT = V7X_VMEM_BYTES - 8 * 1024 * 1024

TM_IN, TN_IN, TN_IN_WIDE = 1024, 1024, 2048
TN_MOD = 1024
TM_MERGE, TC_MERGE = 512, 512
TM_FFN, TF_FFN, TC_FFN = 512, 512, 256
TM_FOURIER = 256
NORM_ROW_CHUNK = 16
NORM_UNROLL = 8


def _cparams(n_axes):
    return pltpu.CompilerParams(dimension_semantics=("arbitrary",) * n_axes,
                                vmem_limit_bytes=VMEM_LIMIT)


def _layer_arg(layer):
    return jnp.full((1,), layer, jnp.int32)


def _norm_mod_rows(x, gs, shift):
    return (x * lax.rsqrt(jnp.mean(x * x, axis=-1, keepdims=True) + RMS_EPS)) * gs + shift


def _norm_mod_store(dst_ref, dst_row0, src_ref, g, scale, shift, src_row0=0, n=None):
    n = src_ref.shape[0] if n is None else n
    chunk = min(NORM_ROW_CHUNK, n)
    assert n % chunk == 0 and src_row0 % chunk == 0 and dst_row0 % chunk == 0
    gs = g * (1.0 + scale)

    def rows(r, carry):
        r0 = pl.multiple_of(r * chunk, chunk)
        h = _norm_mod_rows(src_ref[pl.ds(src_row0 + r0, chunk), :], gs, shift)
        dst_ref[pl.ds(dst_row0 + r0, chunk), :] = h.astype(BF16)
        return carry

    lax.fori_loop(0, n // chunk, rows, 0, unroll=min(NORM_UNROLL, n // chunk))


def _mod_kernel(c_ref, w_ref, b_ref, o_ref):
    cv = c_ref[...]
    s = (cv * jax.nn.sigmoid(cv)).astype(BF16)
    o_ref[...] = jnp.dot(s, w_ref[...].astype(BF16), preferred_element_type=F32) + b_ref[...]


def _modulation(c_all, w_mod, b_mod):
    n_layers, d, nd = w_mod.shape
    tn = min(TN_MOD, nd)
    assert nd % tn == 0
    return pl.pallas_call(
        _mod_kernel,
        grid=(n_layers, nd // tn),
        in_specs=[pl.BlockSpec((MOD_ROWS, d), lambda l, j: (0, 0)),
                  pl.BlockSpec((None, d, tn), lambda l, j: (l, 0, j)),
                  pl.BlockSpec((None, 1, tn), lambda l, j: (l, 0, j))],
        out_specs=pl.BlockSpec((None, MOD_ROWS, tn), lambda l, j: (l, 0, j)),
        out_shape=jax.ShapeDtypeStruct((n_layers, MOD_ROWS, nd), F32),
        compiler_params=_cparams(2),
        name="modulation",
    )(c_all, w_mod, b_mod.reshape(n_layers, 1, nd))


def _in_proj_kernel(l_ref, x_ref, mod_ref, g_ref, w_ref, *rest, kv_tiles):
    h_ref = rest[-1]
    j = pl.program_id(1)

    @pl.when(j == 0)
    def _():
        _norm_mod_store(h_ref, 0, x_ref, g_ref[...], mod_ref[1:2, :], mod_ref[0:1, :])

    res = jnp.dot(h_ref[...], w_ref[...], preferred_element_type=F32)
    if kv_tiles is None:
        o_ref = rest[0]
        o_ref[...] = res.astype(o_ref.dtype)
    else:
        o_ref, kc_ref, vc_ref = rest[-4], rest[-3], rest[-2]
        o_ref[...] = res.astype(o_ref.dtype)
        k0, v0, v1 = kv_tiles
        nb, nh, s, dh = kc_ref.shape

        def scatter(dst_ref):
            for bb in range(nb):
                for hh in range(nh):
                    dst_ref[bb, hh] = res[bb * s:(bb + 1) * s, hh * dh:(hh + 1) * dh]

        @pl.when((j >= k0) & (j < v0))
        def _():
            scatter(kc_ref)

        @pl.when((j >= v0) & (j < v1))
        def _():
            scatter(vc_ref)


def _in_proj(layer, x, mods, g, w, *, rows_per_mod, mod_row0, caches=None):
    m, d = x.shape
    n_out = w.shape[2]
    wide = caches is None and n_out % TN_IN_WIDE == 0
    tm, tn = min(TM_IN, m), min(TN_IN_WIDE if wide else TN_IN, n_out)
    assert m % tm == 0 and n_out % tn == 0 and (rows_per_mod % tm == 0 or tm % rows_per_mod == 0)
    attn = N_HEADS * HEAD_DIM
    in_specs = [pl.BlockSpec((tm, d), lambda i, j, l: (i, 0)),
                pl.BlockSpec((None, None, N_MOD, d),
                             lambda i, j, l: (l[0], mod_row0 + (i * tm) // rows_per_mod, 0, 0)),
                pl.BlockSpec((None, 1, d), lambda i, j, l: (l[0], 0, 0)),
                pl.BlockSpec((None, d, tn), lambda i, j, l: (l[0], 0, j))]
    out_specs = [pl.BlockSpec((tm, tn), lambda i, j, l: (i, j))]
    out_shape = [jax.ShapeDtypeStruct((m, n_out), BF16)]
    args = [_layer_arg(layer), x, mods, g.reshape(g.shape[0], 1, d), w]
    kv_tiles, aliases = None, {}
    if caches is not None:
        seq = caches[0].shape[3]
        assert tm % seq == 0 and tn % HEAD_DIM == 0 and attn % tn == 0
        k0, v0, v1 = attn // tn, 2 * attn // tn, 3 * attn // tn
        kv_tiles = (k0, v0, v1)
        blk = (tm // seq, None, tn // HEAD_DIM, seq, HEAD_DIM)
        out_specs += [pl.BlockSpec(blk, lambda i, j, l: (i, l[0], jnp.clip(j - k0, 0, v0 - k0 - 1), 0, 0)),
                      pl.BlockSpec(blk, lambda i, j, l: (i, l[0], jnp.clip(j - v0, 0, v1 - v0 - 1), 0, 0))]
        out_shape += [jax.ShapeDtypeStruct(c.shape, c.dtype) for c in caches]
        in_specs += [pl.BlockSpec(memory_space=pl.ANY)] * 2
        aliases = {len(args): 1, len(args) + 1: 2}
        args += list(caches)
    return pl.pallas_call(
        functools.partial(_in_proj_kernel, kv_tiles=kv_tiles),
        grid_spec=pltpu.PrefetchScalarGridSpec(
            num_scalar_prefetch=1, grid=(m // tm, n_out // tn), in_specs=in_specs, out_specs=out_specs,
            scratch_shapes=[pltpu.VMEM((tm, d), BF16)]),
        out_shape=out_shape,
        input_output_aliases=aliases,
        compiler_params=_cparams(2),
        name="in_proj_ctx" if caches is not None else "in_proj_lat",
    )(*args)


def _ctx_attn_kernel(q_ref, k_ref, v_ref, o_ref, *, scale):
    for h in range(N_HEADS):
        sl = slice(h * HEAD_DIM, (h + 1) * HEAD_DIM)
        s = lax.dot_general(q_ref[:, sl], k_ref[:, sl], (((1,), (1,)), ((), ())),
                            preferred_element_type=F32) * scale
        e = jnp.exp(s - jnp.max(s, axis=-1, keepdims=True))
        p = e * (1.0 / jnp.sum(e, axis=-1, keepdims=True))
        o_ref[:, sl] = jnp.dot(p.astype(BF16), v_ref[:, sl], preferred_element_type=F32).astype(o_ref.dtype)


def _ctx_attention(proj, n_seq, seq):
    attn = N_HEADS * HEAD_DIM
    return pl.pallas_call(
        functools.partial(_ctx_attn_kernel, scale=HEAD_DIM ** -0.5),
        grid=(n_seq,),
        in_specs=[pl.BlockSpec((seq, attn), lambda b: (b, 0)),
                  pl.BlockSpec((seq, attn), lambda b: (b, 1)),
                  pl.BlockSpec((seq, attn), lambda b: (b, 2))],
        out_specs=pl.BlockSpec((seq, attn), lambda b: (b, 0)),
        out_shape=jax.ShapeDtypeStruct((n_seq * seq, attn), BF16),
        compiler_params=_cparams(1),
        name="ctx_attention",
    )(proj, proj, proj)


def _nbr_layout(rows):
    r_grp, win = NBR_ROWS_PER_GROUP, NA_KH + NBR_ROWS_PER_GROUP
    types, type_of_group = [], []
    for g in range(rows // r_grp):
        w0 = min(max(r_grp * g - NA_KH // 2, 0), rows - win)
        desc = []
        for qi in range(r_grp):
            r = r_grp * g + qi
            r0 = min(max(r - NA_KH // 2, 0), rows - NA_KH)
            desc.append(tuple((w0 + kj - r) if r0 <= w0 + kj < r0 + NA_KH else None for kj in range(win)))
        desc = tuple(desc)
        if desc not in types:
            types.append(desc)
        type_of_group.append(types.index(desc))
    pairs = sorted({(d[qi][2 * m], d[qi][2 * m + 1]) for d in types for qi in range(r_grp)
                    for m in range(win // 2)} - {(None, None)},
                   key=lambda p: tuple(-99 if v is None else v for v in p))
    return types, type_of_group, pairs


def _nbr_attn_kernel(l_ref, rpb_ref, q_ref, k_ref, v_ref, kc_ref, vc_ref, o_ref, bias_ref, pair_ref, *,
                     rows, layout, scale):
    types, type_of_group, pairs = layout
    w = GRID_W
    r_grp, win = NBR_ROWS_PER_GROUP, NA_KH + NBR_ROWS_PER_GROUP
    gq, wk = r_grp * w, win * w
    n_dr, n_dc = 2 * NA_KH - 1, 2 * NA_KW - 1
    neg_inf = float("-inf")
    h = pl.program_id(0)
    layer = l_ref[0]

    @pl.when(pl.program_id(1) == 0)
    def _build_bias_tables():
        cq = lax.broadcasted_iota(jnp.int32, (w, V7X_LANES), 0)
        lane = lax.broadcasted_iota(jnp.int32, (w, V7X_LANES), 1)
        ck = lane & (w - 1)
        c0 = jnp.clip(cq - NA_KW // 2, 0, w - NA_KW)
        in_win = (ck >= c0) & (ck < c0 + NA_KW)
        dc_idx = jnp.where(in_win, jnp.clip(ck - cq, -(NA_KW - 1), NA_KW - 1) + (NA_KW - 1), -1)
        left = lane < w
        for p in range(len(pairs)):
            pair_ref[p] = jnp.full((w, V7X_LANES), neg_inf, F32)

        def per_dc(dc, carry):
            hit = dc_idx == dc
            for p, (da, db) in enumerate(pairs):
                va = neg_inf if da is None else rpb_ref[layer, (h * n_dr + da + NA_KH - 1) * n_dc + dc]
                vb = neg_inf if db is None else rpb_ref[layer, (h * n_dr + db + NA_KH - 1) * n_dc + dc]
                pair_ref[p] = jnp.where(hit, jnp.where(left, va, vb) * LOG2_E, pair_ref[p])
            return carry

        lax.fori_loop(0, n_dc, per_dc, 0)
        for t, desc in enumerate(types):
            for qi in range(r_grp):
                for m in range(win // 2):
                    pr = (desc[qi][2 * m], desc[qi][2 * m + 1])
                    tile = (jnp.full((w, V7X_LANES), neg_inf, F32) if pr == (None, None)
                            else pair_ref[pairs.index(pr)])
                    bias_ref[t, qi * w:(qi + 1) * w, m * V7X_LANES:(m + 1) * V7X_LANES] = tile

    kc = kc_ref[...].astype(BF16)
    vc = vc_ref[...].astype(BF16)
    nt = (((1,), (1,)), ((), ()))
    scale2 = scale * LOG2_E
    default_type = max(set(type_of_group), key=type_of_group.count)

    def scores(g):
        w0 = jnp.clip(r_grp * g - NA_KH // 2, 0, rows - win)
        q0 = pl.multiple_of(g * gq, gq)
        k0 = pl.multiple_of(w0 * w, w)
        t = jnp.int32(default_type)
        for gi, ti in enumerate(type_of_group):
            if ti != default_type:
                t = jnp.where(g == gi, ti, t)
        q = q_ref[pl.ds(q0, gq), :]
        s = lax.dot_general(q, k_ref[pl.ds(k0, wk), :], nt, preferred_element_type=F32) * scale2 + bias_ref[t]
        sc = lax.dot_general(q, kc, nt, preferred_element_type=F32) * scale2
        return q0, k0, s, sc

    def attend(q0, k0, s, sc):
        mx = jnp.maximum(jnp.max(s, axis=-1, keepdims=True), jnp.max(sc, axis=-1, keepdims=True))
        e = jnp.exp2(s - mx)
        ec = jnp.exp2(sc - mx)
        inv = 1.0 / (jnp.sum(e, axis=-1, keepdims=True) + jnp.sum(ec, axis=-1, keepdims=True))
        o = (jnp.dot(e.astype(BF16), v_ref[pl.ds(k0, wk), :], preferred_element_type=F32)
             + jnp.dot(ec.astype(BF16), vc, preferred_element_type=F32))
        o_ref[pl.ds(q0, gq), :] = (o * inv).astype(o_ref.dtype)

    def per_pair(i, carry):
        first, second = scores(2 * i), scores(2 * i + 1)
        attend(*first)
        attend(*second)
        return carry

    lax.fori_loop(0, rows // (2 * r_grp), per_pair, 0, unroll=8)


def _nbr_attention(layer, proj, cache_k, cache_v, rpb, n_batch, n_tok):
    assert 2 * GRID_W == V7X_LANES and n_tok % GRID_W == 0
    rows = n_tok // GRID_W
    assert rows % (2 * NBR_ROWS_PER_GROUP) == 0 and rows >= NA_KH + NBR_ROWS_PER_GROUP and GRID_W >= NA_KW
    layout = _nbr_layout(rows)
    r_grp, win = NBR_ROWS_PER_GROUP, NA_KH + NBR_ROWS_PER_GROUP
    t_ctx = cache_k.shape[3]
    blk = lambda col0: pl.BlockSpec((n_tok, HEAD_DIM), lambda h, b, l: (b, col0 + h))
    ctx = pl.BlockSpec((None, None, None, t_ctx, HEAD_DIM), lambda h, b, l: (b, l[0], h, 0, 0))
    return pl.pallas_call(
        functools.partial(_nbr_attn_kernel, rows=rows, layout=layout, scale=HEAD_DIM ** -0.5),
        grid_spec=pltpu.PrefetchScalarGridSpec(
            num_scalar_prefetch=1, grid=(N_HEADS, n_batch),
            in_specs=[pl.BlockSpec(memory_space=pltpu.SMEM),
                      blk(0), blk(N_HEADS), blk(2 * N_HEADS), ctx, ctx],
            out_specs=pl.BlockSpec((n_tok, HEAD_DIM), lambda h, b, l: (b, h)),
            scratch_shapes=[pltpu.VMEM((len(layout[0]), r_grp * GRID_W, win * GRID_W), F32),
                            pltpu.VMEM((len(layout[2]), GRID_W, V7X_LANES), F32)]),
        out_shape=jax.ShapeDtypeStruct((n_batch * n_tok, N_HEADS * HEAD_DIM), BF16),
        compiler_params=_cparams(2),
        name="nbr_attention",
    )(_layer_arg(layer), rpb.reshape(rpb.shape[0], -1), proj, proj, proj, cache_k, cache_v)


@functools.lru_cache(maxsize=None)
def _dft_tables(n):
    jk = (np.arange(n, dtype=np.int64)[:, None] * np.arange(n, dtype=np.int64)[None, :]) % n
    ang = 2.0 * np.pi * jk.astype(np.float64) / n
    scale = 1.0 / math.sqrt(n)
    return ((np.cos(ang) * scale).astype(BF16), (np.sin(ang) * scale).astype(BF16))


@functools.lru_cache(maxsize=None)
def _mirror_matrix(t):
    r = np.zeros((t, t), np.float32)
    q = np.arange(1, t)
    r[q, t - q] = 1.0
    return r.astype(BF16)


def _fourier_channel_stage(f_ref, cc_ref, sc_ref, rev_ref, zc_ref, zs_ref, edge_ref):
    n, fd = f_ref.shape
    gd = fd // F_GROUPS
    half = n // 2
    tile = rev_ref.shape[0]
    n_tiles = half // tile
    for t in range(n_tiles):
        rows = slice(t * tile, (t + 1) * tile)
        lo = f_ref[rows, :].astype(F32)
        src = (2 * n_tiles - 1 - t) * tile
        mir = jnp.dot(rev_ref[...], f_ref[src:src + tile, :], preferred_element_type=F32)
        if t > 0:
            first = lax.broadcasted_iota(jnp.int32, (tile, 1), 0) == 0
            mir = jnp.where(first, f_ref[src + tile:src + tile + 1, :].astype(F32), mir)
        x_even = (lo + mir).astype(BF16)
        x_odd = (lo - mir).astype(BF16)
        for g in range(F_GROUPS):
            sl = slice(g * gd, (g + 1) * gd)
            zc_ref[rows, sl] = jnp.dot(x_even[:, sl], cc_ref[...], preferred_element_type=F32).astype(BF16)
            zs_ref[rows, sl] = jnp.dot(x_odd[:, sl], sc_ref[...], preferred_element_type=F32).astype(BF16)
    sign = 1.0 - 2.0 * (lax.broadcasted_iota(jnp.int32, (8, 1), 0) & 1).astype(F32)
    x_half = f_ref[half:half + V7X_BF16_SUBLANES, :]
    for g in range(F_GROUPS):
        sl = slice(g * gd, (g + 1) * gd)
        z_half = jnp.dot(x_half[:, sl], cc_ref[...], preferred_element_type=F32)[0:1, :]
        edge_ref[:, sl] = sign * (z_half * (1.0 / math.sqrt(n)))


def _fourier_kernel(f_ref, cc_ref, sc_ref, rev_ref, wc_ref, ws_ref, o_ref, zc_ref, zs_ref, edge_ref):
    fd = f_ref.shape[1]

    @pl.when(pl.program_id(1) == 0)
    def _():
        _fourier_channel_stage(f_ref, cc_ref, sc_ref, rev_ref, zc_ref, zs_ref, edge_ref)

    y = (jnp.dot(wc_ref[...], zc_ref[...], preferred_element_type=F32)
         + jnp.dot(ws_ref[...], zs_ref[...], preferred_element_type=F32))
    tm = y.shape[0]
    y = y.reshape(tm // 8, 8, fd) + edge_ref[...][None]
    o_ref[...] = y.reshape(tm, fd).astype(o_ref.dtype)


def _fourier_fold_kernel(f_ref, cc_ref, sc_ref, rev_ref, wc_ref, ws_ref, wh_ref, o_ref, zc_ref, zs_ref,
                         edge_ref, next_ref):
    fd = f_ref.shape[1]
    t_rows = wc_ref.shape[0]
    sub = rev_ref.shape[0]
    n_sub = t_rows // sub

    @pl.when(pl.program_id(1) == 0)
    def _():
        _fourier_channel_stage(f_ref, cc_ref, sc_ref, rev_ref, zc_ref, zs_ref, edge_ref)
        y_half = jnp.dot(wh_ref[...], zc_ref[...], preferred_element_type=F32)[0:1, :] + edge_ref[0:1, :]
        next_ref[...] = jnp.broadcast_to(y_half, next_ref.shape)

    e = jnp.dot(wc_ref[...], zc_ref[...], preferred_element_type=F32)
    e = (e.reshape(t_rows // 8, 8, fd) + edge_ref[...][None]).reshape(t_rows, fd)
    o = -jnp.dot(ws_ref[...], zs_ref[...], preferred_element_type=F32)
    o_ref[0] = (e - o).astype(o_ref.dtype)
    upper = e + o
    upper_bf = upper.astype(BF16)
    first = lax.broadcasted_iota(jnp.int32, (sub, 1), 0) == 0
    for j in range(n_sub):
        src = n_sub - 1 - j
        blk = jnp.dot(rev_ref[...], upper_bf[src * sub:(src + 1) * sub, :], preferred_element_type=F32)
        row0 = next_ref[0:1, :] if j == 0 else upper[(n_sub - j) * sub:(n_sub - j) * sub + 1, :]
        o_ref[1, j * sub:(j + 1) * sub, :] = jnp.where(first, row0, blk).astype(o_ref.dtype)
    next_ref[...] = jnp.broadcast_to(upper[0:1, :], next_ref.shape)


def _fourier(proj, n_seq, seq):
    attn = N_HEADS * HEAD_DIM
    fd = attn
    gd = fd // F_GROUPS
    cc, sc = _dft_tables(gd)
    cp, sp = _dft_tables(seq)
    half = seq // 2
    tm, tile = min(TM_FOURIER, seq), min(TM_FOURIER, half)
    assert seq % tm == 0 and half % tile == 0 and tile % V7X_BF16_SUBLANES == 0 and proj.shape[1] // fd > 3
    consts = (jnp.asarray(cc), jnp.asarray(sc), jnp.asarray(_mirror_matrix(tile)),
              jnp.asarray(cp[:, :half]), jnp.asarray(-sp[:, :half]))
    const_specs = [pl.BlockSpec((gd, gd), lambda b, i: (0, 0)),
                   pl.BlockSpec((gd, gd), lambda b, i: (0, 0)),
                   pl.BlockSpec((tile, tile), lambda b, i: (0, 0))]
    scratch = [pltpu.VMEM((half, fd), BF16), pltpu.VMEM((half, fd), BF16), pltpu.VMEM((8, fd), F32)]
    if half % TM_MERGE == 0:
        t_rows, n_tiles = TM_MERGE, half // TM_MERGE
        w_half = np.zeros((V7X_BF16_SUBLANES, half), np.float32)
        w_half[0] = cp[half, :half].astype(np.float32)
        rows = lambda b, i: (n_tiles - 1 - i, 0)
        return pl.pallas_call(
            _fourier_fold_kernel,
            grid=(n_seq, n_tiles),
            in_specs=[pl.BlockSpec((seq, fd), lambda b, i: (b, 3))] + const_specs
            + [pl.BlockSpec((t_rows, half), rows), pl.BlockSpec((t_rows, half), rows),
               pl.BlockSpec((V7X_BF16_SUBLANES, half), lambda b, i: (0, 0))],
            out_specs=pl.BlockSpec((None, None, 2, t_rows, fd), lambda b, i: (b, n_tiles - 1 - i, 0, 0, 0)),
            out_shape=jax.ShapeDtypeStruct((n_seq, n_tiles, 2, t_rows, fd), BF16),
            scratch_shapes=scratch + [pltpu.VMEM((8, fd), F32)],
            compiler_params=_cparams(2),
            name="fourier_fold_%d" % seq,
        )(proj, *consts, jnp.asarray(w_half.astype(BF16)))
    return pl.pallas_call(
        _fourier_kernel,
        grid=(n_seq, seq // tm),
        in_specs=[pl.BlockSpec((seq, fd), lambda b, i: (b, 3))] + const_specs
        + [pl.BlockSpec((tm, half), lambda b, i: (i, 0)), pl.BlockSpec((tm, half), lambda b, i: (i, 0))],
        out_specs=pl.BlockSpec((tm, fd), lambda b, i: (b * (seq // tm) + i, 0)),
        out_shape=jax.ShapeDtypeStruct((n_seq * seq, fd), BF16),
        scratch_shapes=scratch,
        compiler_params=_cparams(2),
        name="fourier_%d" % seq,
    )(proj, *consts)


def _merge_out_kernel(l_ref, x_ref, a_ref, fo_ref, ga_ref, gb_ref, mod_ref, wpa_ref, wpb_ref, wout_ref,
                      o_ref, y_ref, *, tc):
    d = x_ref.shape[1]
    for c in range(d // tc):
        sl = slice(c * tc, (c + 1) * tc)
        ya = jnp.dot(a_ref[...], wpa_ref[:, sl], preferred_element_type=F32)
        yb = jnp.dot(fo_ref[...], wpb_ref[:, sl], preferred_element_type=F32)
        y = (jax.nn.sigmoid(ga_ref[:, sl].astype(F32)) * ya
             + jax.nn.sigmoid(gb_ref[:, sl].astype(F32)) * yb)
        y_ref[:, sl] = y.astype(BF16)
    for c in range(d // tc):
        sl = slice(c * tc, (c + 1) * tc)
        o = jnp.dot(y_ref[...], wout_ref[:, sl], preferred_element_type=F32)
        o_ref[:, sl] = x_ref[:, sl] + mod_ref[2:3, sl] * o


def _merge_out(layer, x, a, fo, proj, mods, w_pa, w_pb, w_out, *, rows_per_mod, mod_row0):
    m, d = x.shape
    attn, fd = a.shape[1], fo.shape[-1]
    tm = min(TM_MERGE, m)
    assert m % tm == 0 and (3 * attn + fd) % d == 0 and rows_per_mod % tm == 0
    gate_blk0 = (3 * attn + fd) // d
    whole = lambda arr: pl.BlockSpec((None,) + arr.shape[1:], lambda i, l: (l[0], 0, 0),
                                     pipeline_mode=pl.Buffered(1))
    if fo.ndim == 2:
        fo_spec = pl.BlockSpec((tm, fd), lambda i, l: (i, 0))
    else:
        n_tiles = fo.shape[1]
        assert fo.shape[2:4] == (2, tm) and fo.shape[0] * 2 * n_tiles * tm == m

        def folded(i, l):
            j = i % (2 * n_tiles)
            upper = j >= n_tiles
            return (i // (2 * n_tiles), jnp.where(upper, 2 * n_tiles - 1 - j, j), upper.astype(jnp.int32), 0, 0)

        fo_spec = pl.BlockSpec((None, None, None, tm, fd), folded)
    return pl.pallas_call(
        functools.partial(_merge_out_kernel, tc=min(TC_MERGE, d)),
        grid_spec=pltpu.PrefetchScalarGridSpec(
            num_scalar_prefetch=1, grid=(m // tm,),
            in_specs=[pl.BlockSpec((tm, d), lambda i, l: (i, 0)),
                      pl.BlockSpec((tm, attn), lambda i, l: (i, 0)),
                      fo_spec,
                      pl.BlockSpec((tm, d), lambda i, l: (i, gate_blk0)),
                      pl.BlockSpec((tm, d), lambda i, l: (i, gate_blk0 + 1)),
                      pl.BlockSpec((None, None, N_MOD, d),
                                   lambda i, l: (l[0], mod_row0 + (i * tm) // rows_per_mod, 0, 0)),
                      whole(w_pa), whole(w_pb), whole(w_out)],
            out_specs=pl.BlockSpec((tm, d), lambda i, l: (i, 0)),
            scratch_shapes=[pltpu.VMEM((tm, d), BF16)]),
        out_shape=jax.ShapeDtypeStruct((m, d), F32),
        compiler_params=_cparams(1),
        name="merge_out",
    )(_layer_arg(layer), x, a, fo, proj, proj, mods, w_pa, w_pb, w_out)


def _ffn_kernel(l_ref, x_ref, xp_ref, xn_ref, mod_ref, gains_ref, wu_ref, cw_ref, wd_ref,
                o_ref, h_ref, u_ref, *maybe_acc, seq, tc, final_norm):
    tm = x_ref.shape[0]
    halo = xp_ref.shape[0]
    tf = wu_ref.shape[2]
    seg = min(seq, tm)
    n_seg, stride = tm // seg, seg + halo
    work_rows = h_ref.shape[0]
    pad = u_ref.shape[2] - work_rows
    last_halo = work_rows - halo
    acc_ref = maybe_acc[0] if maybe_acc else o_ref
    n_rows = acc_ref.shape[0]
    i, f = pl.program_id(0), pl.program_id(1)

    @pl.when(f == 0)
    def _():
        g, scale, shift = gains_ref[0:1, :], mod_ref[4:5, :], mod_ref[3:4, :]
        gs = g * (1.0 + scale)
        tok0 = i * tm
        for s in range(n_seg):
            _norm_mod_store(h_ref, s * stride, x_ref, g, scale, shift, src_row0=s * seg, n=seg)
            if s < n_seg - 1:
                h_ref[s * stride + seg:(s + 1) * stride, :] = jnp.zeros((halo, h_ref.shape[1]), BF16)
        after = jnp.where(((tok0 + tm) & (seq - 1)) != 0, _norm_mod_rows(xn_ref[...], gs, shift)[0:1, :], 0.0)
        before = jnp.where((tok0 & (seq - 1)) != 0, _norm_mod_rows(xp_ref[...], gs, shift)[halo - 1:halo, :], 0.0)
        r = lax.broadcasted_iota(jnp.int32, (halo, 1), 0)
        h_ref[last_halo:, :] = jnp.where(r == 0, after, jnp.where(r == 1, before, 0.0)).astype(BF16)
        acc_ref[...] = jnp.zeros_like(acc_ref)

    h = h_ref[...]

    def conv(c, idx, cs):
        tap = lambda k: cw_ref[k, idx:idx + 1, cs]
        return (u_ref[c, idx, pl.ds(pad - 1, n_rows), :] * tap(0)
                + u_ref[c, idx, pl.ds(pad, n_rows), :] * tap(1)
                + u_ref[c, idx, pl.ds(pad + 1, n_rows), :] * tap(2) + tap(3))

    def up_project(c, idx, cs):
        res = jnp.dot(h, wu_ref[idx, :, cs], preferred_element_type=F32)
        u_ref[c, idx, pl.ds(pad, work_rows), :] = res
        u_ref[c, idx, pad - 1:pad, :] = res[last_halo + 1:last_halo + 2, :]

    n_chunks = tf // tc
    for c in range(n_chunks):
        cs = slice(c * tc, (c + 1) * tc)
        up_project(c, 0, cs)
        up_project(c, 1, cs)
    for c in range(n_chunks):
        cs = slice(c * tc, (c + 1) * tc)
        ca = conv(c, 0, cs)
        cg = conv(c, 1, cs)
        act = ((cg * jax.nn.sigmoid(cg)) * ca).astype(BF16)
        acc_ref[...] += jnp.dot(act, wd_ref[cs, :], preferred_element_type=F32)

    @pl.when(f == pl.num_programs(1) - 1)
    def _():
        for s in range(n_seg):
            rows = slice(s * seg, (s + 1) * seg)
            y = x_ref[rows, :] + mod_ref[5:6, :] * acc_ref[s * stride:s * stride + seg, :]
            if final_norm:
                y = (y * lax.rsqrt(jnp.mean(y * y, axis=-1, keepdims=True) + RMS_EPS)) * gains_ref[1:2, :]
            o_ref[rows, :] = y


def _conv_ffn(layer, x, mods, gains, w_up, conv_wb, w_down, *, seq, rows_per_mod, mod_row0, final_norm):
    m, d = x.shape
    ffn = w_down.shape[1]
    tm, tf = min(TM_FFN, m), min(TF_FFN, ffn)
    tc = min(TC_FFN, tf)
    halo = V7X_BF16_SUBLANES
    assert seq & (seq - 1) == 0 and (seq % tm == 0 or tm % seq == 0) and m % seq == 0
    assert m % tm == 0 and ffn % tf == 0 and tf % tc == 0 and min(seq, tm) % halo == 0 and rows_per_mod % tm == 0
    nf, nhb = ffn // tf, m // halo
    n_seg = tm // min(seq, tm)
    work_rows = tm + n_seg * halo
    acc = [pltpu.VMEM((work_rows - halo, d), F32)] if n_seg > 1 else []
    mat = lambda rows, cols, imap: pl.BlockSpec((None, rows, cols), imap)
    return pl.pallas_call(
        functools.partial(_ffn_kernel, seq=seq, tc=tc, final_norm=final_norm),
        grid_spec=pltpu.PrefetchScalarGridSpec(
            num_scalar_prefetch=1, grid=(m // tm, nf),
            in_specs=[pl.BlockSpec((tm, d), lambda i, f, l: (i, 0)),
                      pl.BlockSpec((halo, d), lambda i, f, l: (jnp.maximum(i * (tm // halo) - 1, 0), 0)),
                      pl.BlockSpec((halo, d),
                                   lambda i, f, l: (jnp.minimum((i + 1) * (tm // halo), nhb - 1), 0)),
                      pl.BlockSpec((None, None, N_MOD, d),
                                   lambda i, f, l: (l[0], mod_row0 + (i * tm) // rows_per_mod, 0, 0)),
                      mat(2, d, lambda i, f, l: (l[0], 0, 0)),
                      pl.BlockSpec((None, 2, d, tf), lambda i, f, l: (l[0], 0, 0, f)),
                      pl.BlockSpec((None, 4, 2, tf), lambda i, f, l: (l[0], 0, 0, f)),
                      mat(tf, d, lambda i, f, l: (l[0], f, 0))],
            out_specs=pl.BlockSpec((tm, d), lambda i, f, l: (i, 0)),
            scratch_shapes=[pltpu.VMEM((work_rows, d), BF16),
                            pltpu.VMEM((tf // tc, 2, V7X_F32_SUBLANES + work_rows, tc), F32)] + acc),
        out_shape=jax.ShapeDtypeStruct((m, d), F32),
        compiler_params=_cparams(2),
        name="conv_ffn_final" if final_norm else "conv_ffn",
    )(_layer_arg(layer), x, x, x, mods, gains, w_up, conv_wb, w_down)


def kernel(x_prompt, x_sample, cache_k, cache_v, c, c_ctx, w_mod, b_mod, norm1_g, w_in, rpb, w_pa, w_pb,
           w_out, norm2_g, w_up, conv_w, conv_b, w_down, norm_f_g):
    n_ctx, seq, d = x_prompt.shape
    n_lat, n_tok, _ = x_sample.shape
    n_layers = w_mod.shape[0]
    assert n_lat < MOD_ROWS and n_layers >= 1

    c_all = jnp.zeros((MOD_ROWS, d), F32).at[:n_lat].set(c).at[n_lat].set(c_ctx)
    mods = _modulation(c_all, w_mod, b_mod).reshape(n_layers, MOD_ROWS, N_MOD, d)
    w_in, w_pa, w_pb, w_out, w_up, w_down = (w.astype(BF16) for w in (w_in, w_pa, w_pb, w_out, w_up, w_down))

    ffn_gains = jnp.stack([norm2_g, jnp.broadcast_to(norm_f_g, norm2_g.shape)], axis=1)
    ffn = w_down.shape[1]
    conv_wb = jnp.concatenate([conv_w, conv_b[:, None, :]], axis=1).reshape(n_layers, 4, 2, ffn)
    w_up = w_up.reshape(n_layers, d, 2, ffn).transpose(0, 2, 1, 3)

    xp = x_prompt.reshape(n_ctx * seq, d)
    xs = x_sample.reshape(n_lat * n_tok, d)
    ctx_mod = dict(rows_per_mod=n_ctx * seq, mod_row0=n_lat)
    lat_mod = dict(rows_per_mod=n_tok, mod_row0=0)
    caches = (jnp.zeros((n_ctx, n_layers, N_HEADS, seq, HEAD_DIM), F32),) * 2
    for l in range(n_layers):
        last = l == n_layers - 1
        ffn_w = (mods, ffn_gains, w_up, conv_wb, w_down)
        proj, *caches = _in_proj(l, xp, mods, norm1_g, w_in, caches=caches, **ctx_mod)
        a = _ctx_attention(proj, n_ctx, seq)
        fo = _fourier(proj, n_ctx, seq)
        xp = _merge_out(l, xp, a, fo, proj, mods, w_pa, w_pb, w_out, **ctx_mod)
        xp = _conv_ffn(l, xp, *ffn_w, seq=seq, final_norm=last, **ctx_mod)

        (proj,) = _in_proj(l, xs, mods, norm1_g, w_in, **lat_mod)
        a = _nbr_attention(l, proj, cache_k, cache_v, rpb, n_lat, n_tok)
        fo = _fourier(proj, n_lat, n_tok)
        xs = _merge_out(l, xs, a, fo, proj, mods, w_pa, w_pb, w_out, **lat_mod)
        xs = _conv_ffn(l, xs, *ffn_w, seq=n_tok, final_norm=last, **lat_mod)

    return (xp.reshape(n_ctx, seq, d), xs.reshape(n_lat, n_tok, d), caches[0], caches[1])
```

```python
import functools
import math

import numpy as np
import jax
import jax.numpy as jnp
from jax import lax
from jax.experimental import pallas as pl
from jax.experimental.pallas import tpu as pltpu

F32 = jnp.float32
BF16 = jnp.bfloat16

RMS_EPS = 1e-6
LOG2_E = math.log2(math.e)
N_MOD = 6
N_HEADS = 8
HEAD_DIM = 128
GRID_W = 64
NA_KH = 8
NA_KW = 16
F_GROUPS = 4
NBR_ROWS_PER_GROUP = 4
MOD_ROWS = 8
V7X_LANES = 128
V7X_F32_SUBLANES = 8
V7X_BF16_SUBLANES = 16
V7X_VMEM_BYTES = 64 * 1024 * 1024
VMEM_LIMIT = V7X_VMEM_BYTES - 8 * 1024 * 1024

TM_IN, TN_IN, TN_IN_WIDE = 1024, 1024, 2048
TN_MOD = 1024
TM_MERGE, TC_MERGE = 512, 512
TM_FFN, TF_FFN, TC_FFN = 512, 512, 256
TM_FOURIER = 256
NORM_ROW_CHUNK = 16
NORM_UNROLL = 8


def _cparams(n_axes):
    return pltpu.CompilerParams(dimension_semantics=("arbitrary",) * n_axes,
                                vmem_limit_bytes=VMEM_LIMIT)


def _layer_arg(layer):
    return jnp.full((1,), layer, jnp.int32)


def _norm_mod_rows(x, gs, shift):
    return (x * lax.rsqrt(jnp.mean(x * x, axis=-1, keepdims=True) + RMS_EPS)) * gs + shift


def _norm_mod_store(dst_ref, dst_row0, src_ref, g, scale, shift, src_row0=0, n=None):
    n = src_ref.shape[0] if n is None else n
    chunk = min(NORM_ROW_CHUNK, n)
    assert n % chunk == 0 and src_row0 % chunk == 0 and dst_row0 % chunk == 0
    gs = g * (1.0 + scale)

    def rows(r, carry):
        r0 = pl.multiple_of(r * chunk, chunk)
        h = _norm_mod_rows(src_ref[pl.ds(src_row0 + r0, chunk), :], gs, shift)
        dst_ref[pl.ds(dst_row0 + r0, chunk), :] = h.astype(BF16)
        return carry

    lax.fori_loop(0, n // chunk, rows, 0, unroll=min(NORM_UNROLL, n // chunk))


def _mod_kernel(c_ref, w_ref, b_ref, o_ref):
    cv = c_ref[...]
    s = (cv * jax.nn.sigmoid(cv)).astype(BF16)
    o_ref[...] = jnp.dot(s, w_ref[...].astype(BF16), preferred_element_type=F32) + b_ref[...]


def _modulation(c_all, w_mod, b_mod):
    n_layers, d, nd = w_mod.shape
    tn = min(TN_MOD, nd)
    assert nd % tn == 0
    return pl.pallas_call(
        _mod_kernel,
        grid=(n_layers, nd // tn),
        in_specs=[pl.BlockSpec((MOD_ROWS, d), lambda l, j: (0, 0)),
                  pl.BlockSpec((None, d, tn), lambda l, j: (l, 0, j)),
                  pl.BlockSpec((None, 1, tn), lambda l, j: (l, 0, j))],
        out_specs=pl.BlockSpec((None, MOD_ROWS, tn), lambda l, j: (l, 0, j)),
        out_shape=jax.ShapeDtypeStruct((n_layers, MOD_ROWS, nd), F32),
        compiler_params=_cparams(2),
        name="modulation",
    )(c_all, w_mod, b_mod.reshape(n_layers, 1, nd))


def _in_proj_kernel(l_ref, x_ref, mod_ref, g_ref, w_ref, *rest, kv_tiles):
    h_ref = rest[-1]
    j = pl.program_id(1)

    @pl.when(j == 0)
    def _():
        _norm_mod_store(h_ref, 0, x_ref, g_ref[...], mod_ref[1:2, :], mod_ref[0:1, :])

    res = jnp.dot(h_ref[...], w_ref[...], preferred_element_type=F32)
    if kv_tiles is None:
        o_ref = rest[0]
        o_ref[...] = res.astype(o_ref.dtype)
    else:
        o_ref, kc_ref, vc_ref = rest[-4], rest[-3], rest[-2]
        o_ref[...] = res.astype(o_ref.dtype)
        k0, v0, v1 = kv_tiles
        nb, nh, s, dh = kc_ref.shape

        def scatter(dst_ref):
            for bb in range(nb):
                for hh in range(nh):
                    dst_ref[bb, hh] = res[bb * s:(bb + 1) * s, hh * dh:(hh + 1) * dh]

        @pl.when((j >= k0) & (j < v0))
        def _():
            scatter(kc_ref)

        @pl.when((j >= v0) & (j < v1))
        def _():
            scatter(vc_ref)


def _in_proj(layer, x, mods, g, w, *, rows_per_mod, mod_row0, caches=None):
    m, d = x.shape
    n_out = w.shape[2]
    wide = caches is None and n_out % TN_IN_WIDE == 0
    tm, tn = min(TM_IN, m), min(TN_IN_WIDE if wide else TN_IN, n_out)
    assert m % tm == 0 and n_out % tn == 0 and (rows_per_mod % tm == 0 or tm % rows_per_mod == 0)
    attn = N_HEADS * HEAD_DIM
    in_specs = [pl.BlockSpec((tm, d), lambda i, j, l: (i, 0)),
                pl.BlockSpec((None, None, N_MOD, d),
                             lambda i, j, l: (l[0], mod_row0 + (i * tm) // rows_per_mod, 0, 0)),
                pl.BlockSpec((None, 1, d), lambda i, j, l: (l[0], 0, 0)),
                pl.BlockSpec((None, d, tn), lambda i, j, l: (l[0], 0, j))]
    out_specs = [pl.BlockSpec((tm, tn), lambda i, j, l: (i, j))]
    out_shape = [jax.ShapeDtypeStruct((m, n_out), BF16)]
    args = [_layer_arg(layer), x, mods, g.reshape(g.shape[0], 1, d), w]
    kv_tiles, aliases = None, {}
    if caches is not None:
        seq = caches[0].shape[3]
        assert tm % seq == 0 and tn % HEAD_DIM == 0 and attn % tn == 0
        k0, v0, v1 = attn // tn, 2 * attn // tn, 3 * attn // tn
        kv_tiles = (k0, v0, v1)
        blk = (tm // seq, None, tn // HEAD_DIM, seq, HEAD_DIM)
        out_specs += [pl.BlockSpec(blk, lambda i, j, l: (i, l[0], jnp.clip(j - k0, 0, v0 - k0 - 1), 0, 0)),
                      pl.BlockSpec(blk, lambda i, j, l: (i, l[0], jnp.clip(j - v0, 0, v1 - v0 - 1), 0, 0))]
        out_shape += [jax.ShapeDtypeStruct(c.shape, c.dtype) for c in caches]
        in_specs += [pl.BlockSpec(memory_space=pl.ANY)] * 2
        aliases = {len(args): 1, len(args) + 1: 2}
        args += list(caches)
    return pl.pallas_call(
        functools.partial(_in_proj_kernel, kv_tiles=kv_tiles),
        grid_spec=pltpu.PrefetchScalarGridSpec(
            num_scalar_prefetch=1, grid=(m // tm, n_out // tn), in_specs=in_specs, out_specs=out_specs,
            scratch_shapes=[pltpu.VMEM((tm, d), BF16)]),
        out_shape=out_shape,
        input_output_aliases=aliases,
        compiler_params=_cparams(2),
        name="in_proj_ctx" if caches is not None else "in_proj_lat",
    )(*args)


def _ctx_attn_kernel(q_ref, k_ref, v_ref, o_ref, *, scale):
    for h in range(N_HEADS):
        sl = slice(h * HEAD_DIM, (h + 1) * HEAD_DIM)
        s = lax.dot_general(q_ref[:, sl], k_ref[:, sl], (((1,), (1,)), ((), ())),
                            preferred_element_type=F32) * scale
        e = jnp.exp(s - jnp.max(s, axis=-1, keepdims=True))
        p = e * (1.0 / jnp.sum(e, axis=-1, keepdims=True))
        o_ref[:, sl] = jnp.dot(p.astype(BF16), v_ref[:, sl], preferred_element_type=F32).astype(o_ref.dtype)


def _ctx_attention(proj, n_seq, seq):
    attn = N_HEADS * HEAD_DIM
    return pl.pallas_call(
        functools.partial(_ctx_attn_kernel, scale=HEAD_DIM ** -0.5),
        grid=(n_seq,),
        in_specs=[pl.BlockSpec((seq, attn), lambda b: (b, 0)),
                  pl.BlockSpec((seq, attn), lambda b: (b, 1)),
                  pl.BlockSpec((seq, attn), lambda b: (b, 2))],
        out_specs=pl.BlockSpec((seq, attn), lambda b: (b, 0)),
        out_shape=jax.ShapeDtypeStruct((n_seq * seq, attn), BF16),
        compiler_params=_cparams(1),
        name="ctx_attention",
    )(proj, proj, proj)


def _nbr_layout(rows):
    r_grp, win = NBR_ROWS_PER_GROUP, NA_KH + NBR_ROWS_PER_GROUP
    types, type_of_group = [], []
    for g in range(rows // r_grp):
        w0 = min(max(r_grp * g - NA_KH // 2, 0), rows - win)
        desc = []
        for qi in range(r_grp):
            r = r_grp * g + qi
            r0 = min(max(r - NA_KH // 2, 0), rows - NA_KH)
            desc.append(tuple((w0 + kj - r) if r0 <= w0 + kj < r0 + NA_KH else None for kj in range(win)))
        desc = tuple(desc)
        if desc not in types:
            types.append(desc)
        type_of_group.append(types.index(desc))
    pairs = sorted({(d[qi][2 * m], d[qi][2 * m + 1]) for d in types for qi in range(r_grp)
                    for m in range(win // 2)} - {(None, None)},
                   key=lambda p: tuple(-99 if v is None else v for v in p))
    return types, type_of_group, pairs


def _nbr_attn_kernel(l_ref, rpb_ref, q_ref, k_ref, v_ref, kc_ref, vc_ref, o_ref, bias_ref, pair_ref, *,
                     rows, layout, scale):
    types, type_of_group, pairs = layout
    w = GRID_W
    r_grp, win = NBR_ROWS_PER_GROUP, NA_KH + NBR_ROWS_PER_GROUP
    gq, wk = r_grp * w, win * w
    n_dr, n_dc = 2 * NA_KH - 1, 2 * NA_KW - 1
    neg_inf = float("-inf")
    h = pl.program_id(0)
    layer = l_ref[0]

    @pl.when(pl.program_id(1) == 0)
    def _build_bias_tables():
        cq = lax.broadcasted_iota(jnp.int32, (w, V7X_LANES), 0)
        lane = lax.broadcasted_iota(jnp.int32, (w, V7X_LANES), 1)
        ck = lane & (w - 1)
        c0 = jnp.clip(cq - NA_KW // 2, 0, w - NA_KW)
        in_win = (ck >= c0) & (ck < c0 + NA_KW)
        dc_idx = jnp.where(in_win, jnp.clip(ck - cq, -(NA_KW - 1), NA_KW - 1) + (NA_KW - 1), -1)
        left = lane < w
        for p in range(len(pairs)):
            pair_ref[p] = jnp.full((w, V7X_LANES), neg_inf, F32)

        def per_dc(dc, carry):
            hit = dc_idx == dc
            for p, (da, db) in enumerate(pairs):
                va = neg_inf if da is None else rpb_ref[layer, (h * n_dr + da + NA_KH - 1) * n_dc + dc]
                vb = neg_inf if db is None else rpb_ref[layer, (h * n_dr + db + NA_KH - 1) * n_dc + dc]
                pair_ref[p] = jnp.where(hit, jnp.where(left, va, vb) * LOG2_E, pair_ref[p])
            return carry

        lax.fori_loop(0, n_dc, per_dc, 0)
        for t, desc in enumerate(types):
            for qi in range(r_grp):
                for m in range(win // 2):
                    pr = (desc[qi][2 * m], desc[qi][2 * m + 1])
                    tile = (jnp.full((w, V7X_LANES), neg_inf, F32) if pr == (None, None)
                            else pair_ref[pairs.index(pr)])
                    bias_ref[t, qi * w:(qi + 1) * w, m * V7X_LANES:(m + 1) * V7X_LANES] = tile

    kc = kc_ref[...].astype(BF16)
    vc = vc_ref[...].astype(BF16)
    nt = (((1,), (1,)), ((), ()))
    scale2 = scale * LOG2_E
    default_type = max(set(type_of_group), key=type_of_group.count)

    def scores(g):
        w0 = jnp.clip(r_grp * g - NA_KH // 2, 0, rows - win)
        q0 = pl.multiple_of(g * gq, gq)
        k0 = pl.multiple_of(w0 * w, w)
        t = jnp.int32(default_type)
        for gi, ti in enumerate(type_of_group):
            if ti != default_type:
                t = jnp.where(g == gi, ti, t)
        q = q_ref[pl.ds(q0, gq), :]
        s = lax.dot_general(q, k_ref[pl.ds(k0, wk), :], nt, preferred_element_type=F32) * scale2 + bias_ref[t]
        sc = lax.dot_general(q, kc, nt, preferred_element_type=F32) * scale2
        return q0, k0, s, sc

    def attend(q0, k0, s, sc):
        mx = jnp.maximum(jnp.max(s, axis=-1, keepdims=True), jnp.max(sc, axis=-1, keepdims=True))
        e = jnp.exp2(s - mx)
        ec = jnp.exp2(sc - mx)
        inv = 1.0 / (jnp.sum(e, axis=-1, keepdims=True) + jnp.sum(ec, axis=-1, keepdims=True))
        o = (jnp.dot(e.astype(BF16), v_ref[pl.ds(k0, wk), :], preferred_element_type=F32)
             + jnp.dot(ec.astype(BF16), vc, preferred_element_type=F32))
        o_ref[pl.ds(q0, gq), :] = (o * inv).astype(o_ref.dtype)

    def per_pair(i, carry):
        first, second = scores(2 * i), scores(2 * i + 1)
        attend(*first)
        attend(*second)
        return carry

    lax.fori_loop(0, rows // (2 * r_grp), per_pair, 0, unroll=True)


def _nbr_attention(layer, proj, cache_k, cache_v, rpb, n_batch, n_tok):
    assert 2 * GRID_W == V7X_LANES and n_tok % GRID_W == 0
    rows = n_tok // GRID_W
    assert rows % (2 * NBR_ROWS_PER_GROUP) == 0 and rows >= NA_KH + NBR_ROWS_PER_GROUP and GRID_W >= NA_KW
    layout = _nbr_layout(rows)
    r_grp, win = NBR_ROWS_PER_GROUP, NA_KH + NBR_ROWS_PER_GROUP
    t_ctx = cache_k.shape[3]
    blk = lambda col0: pl.BlockSpec((n_tok, HEAD_DIM), lambda h, b, l: (b, col0 + h))
    ctx = pl.BlockSpec((None, None, None, t_ctx, HEAD_DIM), lambda h, b, l: (b, l[0], h, 0, 0))
    return pl.pallas_call(
        functools.partial(_nbr_attn_kernel, rows=rows, layout=layout, scale=HEAD_DIM ** -0.5),
        grid_spec=pltpu.PrefetchScalarGridSpec(
            num_scalar_prefetch=1, grid=(N_HEADS, n_batch),
            in_specs=[pl.BlockSpec(memory_space=pltpu.SMEM),
                      blk(0), blk(N_HEADS), blk(2 * N_HEADS), ctx, ctx],
            out_specs=pl.BlockSpec((n_tok, HEAD_DIM), lambda h, b, l: (b, h)),
            scratch_shapes=[pltpu.VMEM((len(layout[0]), r_grp * GRID_W, win * GRID_W), F32),
                            pltpu.VMEM((len(layout[2]), GRID_W, V7X_LANES), F32)]),
        out_shape=jax.ShapeDtypeStruct((n_batch * n_tok, N_HEADS * HEAD_DIM), BF16),
        compiler_params=_cparams(2),
        name="nbr_attention",
    )(_layer_arg(layer), rpb.reshape(rpb.shape[0], -1), proj, proj, proj, cache_k, cache_v)


@functools.lru_cache(maxsize=None)
def _dft_tables(n):
    jk = (np.arange(n, dtype=np.int64)[:, None] * np.arange(n, dtype=np.int64)[None, :]) % n
    ang = 2.0 * np.pi * jk.astype(np.float64) / n
    scale = 1.0 / math.sqrt(n)
    return ((np.cos(ang) * scale).astype(BF16), (np.sin(ang) * scale).astype(BF16))


@functools.lru_cache(maxsize=None)
def _mirror_matrix(t):
    r = np.zeros((t, t), np.float32)
    q = np.arange(1, t)
    r[q, t - q] = 1.0
    return r.astype(BF16)


def _fourier_channel_stage(f_ref, cc_ref, sc_ref, rev_ref, zc_ref, zs_ref, edge_ref):
    n, fd = f_ref.shape
    gd = fd // F_GROUPS
    half = n // 2
    tile = rev_ref.shape[0]
    n_tiles = half // tile
    for t in range(n_tiles):
        rows = slice(t * tile, (t + 1) * tile)
        lo = f_ref[rows, :].astype(F32)
        src = (2 * n_tiles - 1 - t) * tile
        mir = jnp.dot(rev_ref[...], f_ref[src:src + tile, :], preferred_element_type=F32)
        if t > 0:
            first = lax.broadcasted_iota(jnp.int32, (tile, 1), 0) == 0
            mir = jnp.where(first, f_ref[src + tile:src + tile + 1, :].astype(F32), mir)
        x_even = (lo + mir).astype(BF16)
        x_odd = (lo - mir).astype(BF16)
        for g in range(F_GROUPS):
            sl = slice(g * gd, (g + 1) * gd)
            zc_ref[rows, sl] = jnp.dot(x_even[:, sl], cc_ref[...], preferred_element_type=F32).astype(BF16)
            zs_ref[rows, sl] = jnp.dot(x_odd[:, sl], sc_ref[...], preferred_element_type=F32).astype(BF16)
    sign = 1.0 - 2.0 * (lax.broadcasted_iota(jnp.int32, (V7X_F32_SUBLANES, 1), 0) & 1).astype(F32)
    x_half = f_ref[half:half + V7X_BF16_SUBLANES, :]
    for g in range(F_GROUPS):
        sl = slice(g * gd, (g + 1) * gd)
        z_half = jnp.dot(x_half[:, sl], cc_ref[...], preferred_element_type=F32)[0:1, :]
        edge_ref[:, sl] = sign * (z_half * (1.0 / math.sqrt(n)))


def _fourier_kernel(f_ref, cc_ref, sc_ref, rev_ref, wc_ref, ws_ref, o_ref, zc_ref, zs_ref, edge_ref):
    fd = f_ref.shape[1]

    @pl.when(pl.program_id(1) == 0)
    def _():
        _fourier_channel_stage(f_ref, cc_ref, sc_ref, rev_ref, zc_ref, zs_ref, edge_ref)

    y = (jnp.dot(wc_ref[...], zc_ref[...], preferred_element_type=F32)
         + jnp.dot(ws_ref[...], zs_ref[...], preferred_element_type=F32))
    tm = y.shape[0]
    y = y.reshape(tm // V7X_F32_SUBLANES, V7X_F32_SUBLANES, fd) + edge_ref[...][None]
    o_ref[...] = y.reshape(tm, fd).astype(o_ref.dtype)


def _fourier_fold_kernel(f_ref, cc_ref, sc_ref, rev_ref, wc_ref, ws_ref, wh_ref, o_ref, zc_ref, zs_ref,
                         edge_ref, next_ref):
    fd = f_ref.shape[1]
    t_rows = wc_ref.shape[0]
    sub = rev_ref.shape[0]
    n_sub = t_rows // sub

    @pl.when(pl.program_id(1) == 0)
    def _():
        _fourier_channel_stage(f_ref, cc_ref, sc_ref, rev_ref, zc_ref, zs_ref, edge_ref)
        y_half = jnp.dot(wh_ref[...], zc_ref[...], preferred_element_type=F32)[0:1, :] + edge_ref[0:1, :]
        next_ref[...] = jnp.broadcast_to(y_half, next_ref.shape)

    e = jnp.dot(wc_ref[...], zc_ref[...], preferred_element_type=F32)
    e = (e.reshape(t_rows // V7X_F32_SUBLANES, V7X_F32_SUBLANES, fd) + edge_ref[...][None]).reshape(t_rows, fd)
    o = -jnp.dot(ws_ref[...], zs_ref[...], preferred_element_type=F32)
    o_ref[0] = (e - o).astype(o_ref.dtype)
    upper = e + o
    upper_bf = upper.astype(BF16)
    first = lax.broadcasted_iota(jnp.int32, (sub, 1), 0) == 0
    for j in range(n_sub):
        src = n_sub - 1 - j
        blk = jnp.dot(rev_ref[...], upper_bf[src * sub:(src + 1) * sub, :], preferred_element_type=F32)
        row0 = next_ref[0:1, :] if j == 0 else upper[(n_sub - j) * sub:(n_sub - j) * sub + 1, :]
        o_ref[1, j * sub:(j + 1) * sub, :] = jnp.where(first, row0, blk).astype(o_ref.dtype)
    next_ref[...] = jnp.broadcast_to(upper[0:1, :], next_ref.shape)


def _fourier(proj, n_seq, seq):
    attn = N_HEADS * HEAD_DIM
    fd = attn
    gd = fd // F_GROUPS
    cc, sc = _dft_tables(gd)
    cp, sp = _dft_tables(seq)
    half = seq // 2
    tm, tile = min(TM_FOURIER, seq), min(TM_FOURIER, half)
    assert seq % tm == 0 and half % tile == 0 and tile % V7X_BF16_SUBLANES == 0 and proj.shape[1] // fd > 3
    consts = (jnp.asarray(cc), jnp.asarray(sc), jnp.asarray(_mirror_matrix(tile)),
              jnp.asarray(cp[:, :half]), jnp.asarray(-sp[:, :half]))
    const_specs = [pl.BlockSpec((gd, gd), lambda b, i: (0, 0)),
                   pl.BlockSpec((gd, gd), lambda b, i: (0, 0)),
                   pl.BlockSpec((tile, tile), lambda b, i: (0, 0))]
    row_tile = pltpu.VMEM((V7X_F32_SUBLANES, fd), F32)
    scratch = [pltpu.VMEM((half, fd), BF16), pltpu.VMEM((half, fd), BF16), row_tile]
    if half % TM_MERGE == 0:
        t_rows, n_tiles = TM_MERGE, half // TM_MERGE
        w_half = np.zeros((V7X_BF16_SUBLANES, half), np.float32)
        w_half[0] = cp[half, :half].astype(np.float32)
        rows = lambda b, i: (n_tiles - 1 - i, 0)
        return pl.pallas_call(
            _fourier_fold_kernel,
            grid=(n_seq, n_tiles),
            in_specs=[pl.BlockSpec((seq, fd), lambda b, i: (b, 3))] + const_specs
            + [pl.BlockSpec((t_rows, half), rows), pl.BlockSpec((t_rows, half), rows),
               pl.BlockSpec((V7X_BF16_SUBLANES, half), lambda b, i: (0, 0))],
            out_specs=pl.BlockSpec((None, None, 2, t_rows, fd), lambda b, i: (b, n_tiles - 1 - i, 0, 0, 0)),
            out_shape=jax.ShapeDtypeStruct((n_seq, n_tiles, 2, t_rows, fd), BF16),
            scratch_shapes=scratch + [row_tile],
            compiler_params=_cparams(2),
            name="fourier_fold_%d" % seq,
        )(proj, *consts, jnp.asarray(w_half.astype(BF16)))
    return pl.pallas_call(
        _fourier_kernel,
        grid=(n_seq, seq // tm),
        in_specs=[pl.BlockSpec((seq, fd), lambda b, i: (b, 3))] + const_specs
        + [pl.BlockSpec((tm, half), lambda b, i: (i, 0)), pl.BlockSpec((tm, half), lambda b, i: (i, 0))],
        out_specs=pl.BlockSpec((tm, fd), lambda b, i: (b * (seq // tm) + i, 0)),
        out_shape=jax.ShapeDtypeStruct((n_seq * seq, fd), BF16),
        scratch_shapes=scratch,
        compiler_params=_cparams(2),
        name="fourier_%d" % seq,
    )(proj, *consts)


def _merge_out_kernel(l_ref, x_ref, a_ref, fo_ref, ga_ref, gb_ref, mod_ref, wpa_ref, wpb_ref, wout_ref,
                      o_ref, y_ref, *, tc):
    d = x_ref.shape[1]
    for c in range(d // tc):
        sl = slice(c * tc, (c + 1) * tc)
        ya = jnp.dot(a_ref[...], wpa_ref[:, sl], preferred_element_type=F32)
        yb = jnp.dot(fo_ref[...], wpb_ref[:, sl], preferred_element_type=F32)
        y = (jax.nn.sigmoid(ga_ref[:, sl].astype(F32)) * ya
             + jax.nn.sigmoid(gb_ref[:, sl].astype(F32)) * yb)
        y_ref[:, sl] = y.astype(BF16)
    for c in range(d // tc):
        sl = slice(c * tc, (c + 1) * tc)
        o = jnp.dot(y_ref[...], wout_ref[:, sl], preferred_element_type=F32)
        o_ref[:, sl] = x_ref[:, sl] + mod_ref[2:3, sl] * o


def _merge_out(layer, x, a, fo, proj, mods, w_pa, w_pb, w_out, *, rows_per_mod, mod_row0):
    m, d = x.shape
    attn, fd = a.shape[1], fo.shape[-1]
    tm = min(TM_MERGE, m)
    assert m % tm == 0 and (3 * attn + fd) % d == 0 and rows_per_mod % tm == 0
    gate_blk0 = (3 * attn + fd) // d
    whole = lambda arr: pl.BlockSpec((None,) + arr.shape[1:], lambda i, l: (l[0], 0, 0),
                                     pipeline_mode=pl.Buffered(1))
    if fo.ndim == 2:
        fo_spec = pl.BlockSpec((tm, fd), lambda i, l: (i, 0))
    else:
        n_tiles = fo.shape[1]
        assert fo.shape[2:4] == (2, tm) and fo.shape[0] * 2 * n_tiles * tm == m

        def folded(i, l):
            j = i % (2 * n_tiles)
            upper = j >= n_tiles
            return (i // (2 * n_tiles), jnp.where(upper, 2 * n_tiles - 1 - j, j), upper.astype(jnp.int32), 0, 0)

        fo_spec = pl.BlockSpec((None, None, None, tm, fd), folded)
    return pl.pallas_call(
        functools.partial(_merge_out_kernel, tc=min(TC_MERGE, d)),
        grid_spec=pltpu.PrefetchScalarGridSpec(
            num_scalar_prefetch=1, grid=(m // tm,),
            in_specs=[pl.BlockSpec((tm, d), lambda i, l: (i, 0)),
                      pl.BlockSpec((tm, attn), lambda i, l: (i, 0)),
                      fo_spec,
                      pl.BlockSpec((tm, d), lambda i, l: (i, gate_blk0)),
                      pl.BlockSpec((tm, d), lambda i, l: (i, gate_blk0 + 1)),
                      pl.BlockSpec((None, None, N_MOD, d),
                                   lambda i, l: (l[0], mod_row0 + (i * tm) // rows_per_mod, 0, 0)),
                      whole(w_pa), whole(w_pb), whole(w_out)],
            out_specs=pl.BlockSpec((tm, d), lambda i, l: (i, 0)),
            scratch_shapes=[pltpu.VMEM((tm, d), BF16)]),
        out_shape=jax.ShapeDtypeStruct((m, d), F32),
        compiler_params=_cparams(1),
        name="merge_out",
    )(_layer_arg(layer), x, a, fo, proj, proj, mods, w_pa, w_pb, w_out)


def _ffn_kernel(l_ref, x_ref, xp_ref, xn_ref, mod_ref, gains_ref, wa_ref, wg_ref, cwa_ref, cwg_ref, wd_ref,
                o_ref, h_ref, u_ref, *maybe_acc, seq, tc, final_norm):
    tm = x_ref.shape[0]
    halo = xp_ref.shape[0]
    tf = wa_ref.shape[1]
    seg = min(seq, tm)
    n_seg, stride = tm // seg, seg + halo
    work_rows = h_ref.shape[0]
    pad = u_ref.shape[2] - work_rows
    last_halo = work_rows - halo
    acc_ref = maybe_acc[0] if maybe_acc else o_ref
    n_rows = acc_ref.shape[0]
    i, f = pl.program_id(0), pl.program_id(1)

    @pl.when(f == 0)
    def _():
        g, scale, shift = gains_ref[0:1, :], mod_ref[4:5, :], mod_ref[3:4, :]
        gs = g * (1.0 + scale)
        tok0 = i * tm
        for s in range(n_seg):
            _norm_mod_store(h_ref, s * stride, x_ref, g, scale, shift, src_row0=s * seg, n=seg)
            if s < n_seg - 1:
                h_ref[s * stride + seg:(s + 1) * stride, :] = jnp.zeros((halo, h_ref.shape[1]), BF16)
        after = jnp.where(((tok0 + tm) & (seq - 1)) != 0, _norm_mod_rows(xn_ref[...], gs, shift)[0:1, :], 0.0)
        before = jnp.where((tok0 & (seq - 1)) != 0, _norm_mod_rows(xp_ref[...], gs, shift)[halo - 1:halo, :], 0.0)
        r = lax.broadcasted_iota(jnp.int32, (halo, 1), 0)
        h_ref[last_halo:, :] = jnp.where(r == 0, after, jnp.where(r == 1, before, 0.0)).astype(BF16)
        acc_ref[...] = jnp.zeros_like(acc_ref)

    h = h_ref[...]

    def conv(c, idx, cw_ref, cs):
        return (u_ref[c, idx, pl.ds(pad - 1, n_rows), :] * cw_ref[0:1, cs]
                + u_ref[c, idx, pl.ds(pad, n_rows), :] * cw_ref[1:2, cs]
                + u_ref[c, idx, pl.ds(pad + 1, n_rows), :] * cw_ref[2:3, cs] + cw_ref[3:4, cs])

    def up_project(c, idx, w_ref, cs):
        res = jnp.dot(h, w_ref[:, cs], preferred_element_type=F32)
        u_ref[c, idx, pl.ds(pad, work_rows), :] = res
        u_ref[c, idx, pad - 1:pad, :] = res[last_halo + 1:last_halo + 2, :]

    n_chunks = tf // tc
    for c in range(n_chunks):
        cs = slice(c * tc, (c + 1) * tc)
        up_project(c, 0, wa_ref, cs)
        up_project(c, 1, wg_ref, cs)
    for c in range(n_chunks):
        cs = slice(c * tc, (c + 1) * tc)
        ca = conv(c, 0, cwa_ref, cs)
        cg = conv(c, 1, cwg_ref, cs)
        act = ((cg * jax.nn.sigmoid(cg)) * ca).astype(BF16)
        acc_ref[...] += jnp.dot(act, wd_ref[cs, :], preferred_element_type=F32)

    @pl.when(f == pl.num_programs(1) - 1)
    def _():
        for s in range(n_seg):
            rows = slice(s * seg, (s + 1) * seg)
            y = x_ref[rows, :] + mod_ref[5:6, :] * acc_ref[s * stride:s * stride + seg, :]
            if final_norm:
                y = (y * lax.rsqrt(jnp.mean(y * y, axis=-1, keepdims=True) + RMS_EPS)) * gains_ref[1:2, :]
            o_ref[rows, :] = y


def _conv_ffn(layer, x, mods, gains, w_up, conv_wb, w_down, *, seq, rows_per_mod, mod_row0, final_norm):
    m, d = x.shape
    ffn = w_down.shape[1]
    tm, tf = min(TM_FFN, m), min(TF_FFN, ffn)
    tc = min(TC_FFN, tf)
    halo = V7X_BF16_SUBLANES
    assert seq & (seq - 1) == 0 and (seq % tm == 0 or tm % seq == 0) and m % seq == 0
    assert m % tm == 0 and ffn % tf == 0 and tf % tc == 0 and min(seq, tm) % halo == 0 and rows_per_mod % tm == 0
    nf, nhb = ffn // tf, m // halo
    n_seg = tm // min(seq, tm)
    work_rows = tm + n_seg * halo
    acc = [pltpu.VMEM((work_rows - halo, d), F32)] if n_seg > 1 else []
    mat = lambda rows, cols, imap: pl.BlockSpec((None, rows, cols), imap)
    return pl.pallas_call(
        functools.partial(_ffn_kernel, seq=seq, tc=tc, final_norm=final_norm),
        grid_spec=pltpu.PrefetchScalarGridSpec(
            num_scalar_prefetch=1, grid=(m // tm, nf),
            in_specs=[pl.BlockSpec((tm, d), lambda i, f, l: (i, 0)),
                      pl.BlockSpec((halo, d), lambda i, f, l: (jnp.maximum(i * (tm // halo) - 1, 0), 0)),
                      pl.BlockSpec((halo, d),
                                   lambda i, f, l: (jnp.minimum((i + 1) * (tm // halo), nhb - 1), 0)),
                      pl.BlockSpec((None, None, N_MOD, d),
                                   lambda i, f, l: (l[0], mod_row0 + (i * tm) // rows_per_mod, 0, 0)),
                      mat(2, d, lambda i, f, l: (l[0], 0, 0)),
                      mat(d, tf, lambda i, f, l: (l[0], 0, f)),
                      mat(d, tf, lambda i, f, l: (l[0], 0, nf + f)),
                      mat(4, tf, lambda i, f, l: (l[0], 0, f)),
                      mat(4, tf, lambda i, f, l: (l[0], 0, nf + f)),
                      mat(tf, d, lambda i, f, l: (l[0], f, 0))],
            out_specs=pl.BlockSpec((tm, d), lambda i, f, l: (i, 0)),
            scratch_shapes=[pltpu.VMEM((work_rows, d), BF16),
                            pltpu.VMEM((tf // tc, 2, V7X_F32_SUBLANES + work_rows, tc), F32)] + acc),
        out_shape=jax.ShapeDtypeStruct((m, d), F32),
        compiler_params=_cparams(2),
        name="conv_ffn_final" if final_norm else "conv_ffn",
    )(_layer_arg(layer), x, x, x, mods, gains, w_up, w_up, conv_wb, conv_wb, w_down)


def kernel(x_prompt, x_sample, cache_k, cache_v, c, c_ctx, w_mod, b_mod, norm1_g, w_in, rpb, w_pa, w_pb,
           w_out, norm2_g, w_up, conv_w, conv_b, w_down, norm_f_g):
    n_ctx, seq, d = x_prompt.shape
    n_lat, n_tok, _ = x_sample.shape
    n_layers = w_mod.shape[0]
    assert n_lat < MOD_ROWS and n_layers >= 1

    c_all = jnp.zeros((MOD_ROWS, d), F32).at[:n_lat].set(c).at[n_lat].set(c_ctx)
    mods = _modulation(c_all, w_mod, b_mod).reshape(n_layers, MOD_ROWS, N_MOD, d)
    w_in, w_pa, w_pb, w_out, w_up, w_down = (w.astype(BF16) for w in (w_in, w_pa, w_pb, w_out, w_up, w_down))

    ffn_gains = jnp.stack([norm2_g, jnp.broadcast_to(norm_f_g, norm2_g.shape)], axis=1)
    conv_wb = jnp.concatenate([conv_w, conv_b[:, None, :]], axis=1)

    xp = x_prompt.reshape(n_ctx * seq, d)
    xs = x_sample.reshape(n_lat * n_tok, d)
    ctx_mod = dict(rows_per_mod=n_ctx * seq, mod_row0=n_lat)
    lat_mod = dict(rows_per_mod=n_tok, mod_row0=0)
    caches = (jnp.zeros((n_ctx, n_layers, N_HEADS, seq, HEAD_DIM), F32),) * 2
    for l in range(n_layers):
        last = l == n_layers - 1
        ffn_w = (mods, ffn_gains, w_up, conv_wb, w_down)
        proj, *caches = _in_proj(l, xp, mods, norm1_g, w_in, caches=caches, **ctx_mod)
        a = _ctx_attention(proj, n_ctx, seq)
        fo = _fourier(proj, n_ctx, seq)
        xp = _merge_out(l, xp, a, fo, proj, mods, w_pa, w_pb, w_out, **ctx_mod)
        xp = _conv_ffn(l, xp, *ffn_w, seq=seq, final_norm=last, **ctx_mod)

        (proj,) = _in_proj(l, xs, mods, norm1_g, w_in, **lat_mod)
        a = _nbr_attention(l, proj, cache_k, cache_v, rpb, n_lat, n_tok)
        fo = _fourier(proj, n_lat, n_tok)
        xs = _merge_out(l, xs, a, fo, proj, mods, w_pa, w_pb, w_out, **lat_mod)
        xs = _conv_ffn(l, xs, *ffn_w, seq=n_tok, final_norm=last, **lat_mod)

    return (xp.reshape(n_ctx, seq, d), xs.reshape(n_lat, n_tok, d), caches[0], caches[1])
```

```python
import functools
import math

import numpy as np
import jax
import jax.numpy as jnp
from jax import lax
from jax.experimental import pallas as pl
from jax.experimental.pallas import tpu as pltpu

F32 = jnp.float32
BF16 = jnp.bfloat16

RMS_EPS = 1e-6
LOG2_E = math.log2(math.e)
N_MOD = 6
N_HEADS = 8
HEAD_DIM = 128
GRID_W = 64
NA_KH = 8
NA_KW = 16
F_GROUPS = 4
NBR_ROWS_PER_GROUP = 4
MOD_ROWS = 8
V7X_LANES = 128
V7X_F32_SUBLANES = 8
V7X_BF16_SUBLANES = 16
V7X_VMEM_BYTES = 64 * 1024 * 1024
VMEM_LIMIT = V7X_VMEM_BYTES - 6 * 1024 * 1024

TM_IN, TN_IN, TN_IN_WIDE = 1024, 1024, 2048
TM_IN_CACHE = 512
TN_MOD = 1024
TM_MERGE, TC_MERGE = 512, 512
TM_FFN, TF_FFN, TC_FFN = 512, 512, 256
TM_FFN_LONG = 1024
FFN_OUT_ROW_CHUNK = 256
TM_FOURIER = 256
NORM_ROW_CHUNK = 16
NORM_UNROLL = 8


def _cparams(n_axes):
    return pltpu.CompilerParams(dimension_semantics=("arbitrary",) * n_axes,
                                vmem_limit_bytes=VMEM_LIMIT)


def _layer_arg(layer):
    return jnp.full((1,), layer, jnp.int32)


def _norm_mod_rows(x, gs, shift):
    return (x * lax.rsqrt(jnp.mean(x * x, axis=-1, keepdims=True) + RMS_EPS)) * gs + shift


def _norm_mod_store(dst_ref, dst_row0, src_ref, g, scale, shift, src_row0=0, n=None):
    n = src_ref.shape[0] if n is None else n
    chunk = min(NORM_ROW_CHUNK, n)
    assert n % chunk == 0 and src_row0 % chunk == 0 and dst_row0 % chunk == 0
    gs = g * (1.0 + scale)

    def rows(r, carry):
        r0 = pl.multiple_of(r * chunk, chunk)
        h = _norm_mod_rows(src_ref[pl.ds(src_row0 + r0, chunk), :], gs, shift)
        dst_ref[pl.ds(dst_row0 + r0, chunk), :] = h.astype(BF16)
        return carry

    lax.fori_loop(0, n // chunk, rows, 0, unroll=min(NORM_UNROLL, n // chunk))


def _mod_kernel(c_ref, w_ref, b_ref, o_ref):
    cv = c_ref[...]
    s = (cv * jax.nn.sigmoid(cv)).astype(BF16)
    o_ref[...] = jnp.dot(s, w_ref[...].astype(BF16), preferred_element_type=F32) + b_ref[...]


def _modulation(c_all, w_mod, b_mod):
    n_layers, d, nd = w_mod.shape
    tn = min(TN_MOD, nd)
    assert nd % tn == 0
    return pl.pallas_call(
        _mod_kernel,
        grid=(n_layers, nd // tn),
        in_specs=[pl.BlockSpec((MOD_ROWS, d), lambda l, j: (0, 0)),
                  pl.BlockSpec((None, d, tn), lambda l, j: (l, 0, j)),
                  pl.BlockSpec((None, 1, tn), lambda l, j: (l, 0, j))],
        out_specs=pl.BlockSpec((None, MOD_ROWS, tn), lambda l, j: (l, 0, j)),
        out_shape=jax.ShapeDtypeStruct((n_layers, MOD_ROWS, nd), F32),
        compiler_params=_cparams(2),
        name="modulation",
    )(c_all, w_mod, b_mod.reshape(n_layers, 1, nd))


def _in_proj_kernel(l_ref, x_ref, mod_ref, g_ref, w_ref, *rest, kv_tiles):
    h_ref = rest[-1]
    j = pl.program_id(1)

    @pl.when(j == 0)
    def _():
        _norm_mod_store(h_ref, 0, x_ref, g_ref[...], mod_ref[1:2, :], mod_ref[0:1, :])

    res = jnp.dot(h_ref[...], w_ref[...], preferred_element_type=F32)
    if kv_tiles is None:
        o_ref = rest[0]
        o_ref[...] = res.astype(o_ref.dtype)
    else:
        o_ref, kc_ref, vc_ref = rest[-4], rest[-3], rest[-2]
        o_ref[...] = res.astype(o_ref.dtype)
        (k_tile, k_col), (v_tile, v_col) = kv_tiles
        nb, nh, s, dh = kc_ref.shape

        def scatter(dst_ref, col0):
            for bb in range(nb):
                for hh in range(nh):
                    dst_ref[bb, hh] = res[bb * s:(bb + 1) * s, col0 + hh * dh:col0 + (hh + 1) * dh]

        @pl.when(j == k_tile)
        def _():
            scatter(kc_ref, k_col)

        @pl.when(j == v_tile)
        def _():
            scatter(vc_ref, v_col)


def _in_proj(layer, x, mods, g, w, *, rows_per_mod, mod_row0, caches=None):
    m, d = x.shape
    n_out = w.shape[2]
    tm = min(TM_IN if caches is None else TM_IN_CACHE, m)
    tn = min(TN_IN_WIDE if n_out % TN_IN_WIDE == 0 else TN_IN, n_out)
    assert m % tm == 0 and n_out % tn == 0 and (rows_per_mod % tm == 0 or tm % rows_per_mod == 0)
    attn = N_HEADS * HEAD_DIM
    in_specs = [pl.BlockSpec((tm, d), lambda i, j, l: (i, 0)),
                pl.BlockSpec((None, None, N_MOD, d),
                             lambda i, j, l: (l[0], mod_row0 + (i * tm) // rows_per_mod, 0, 0)),
                pl.BlockSpec((None, 1, d), lambda i, j, l: (l[0], 0, 0)),
                pl.BlockSpec((None, d, tn), lambda i, j, l: (l[0], 0, j))]
    out_specs = [pl.BlockSpec((tm, tn), lambda i, j, l: (i, j))]
    out_shape = [jax.ShapeDtypeStruct((m, n_out), BF16)]
    args = [_layer_arg(layer), x, mods, g.reshape(g.shape[0], 1, d), w]
    kv_tiles, aliases = None, {}
    if caches is not None:
        seq = caches[0].shape[3]
        kv_tiles = tuple(divmod(c * attn, tn) for c in (1, 2))
        assert tm % seq == 0 and all(col + attn <= tn for _, col in kv_tiles)
        blk = pl.BlockSpec((tm // seq, None, N_HEADS, seq, HEAD_DIM), lambda i, j, l: (i, l[0], 0, 0, 0))
        out_specs += [blk, blk]
        out_shape += [jax.ShapeDtypeStruct(c.shape, c.dtype) for c in caches]
        in_specs += [pl.BlockSpec(memory_space=pl.ANY)] * 2
        aliases = {len(args): 1, len(args) + 1: 2}
        args += list(caches)
    return pl.pallas_call(
        functools.partial(_in_proj_kernel, kv_tiles=kv_tiles),
        grid_spec=pltpu.PrefetchScalarGridSpec(
            num_scalar_prefetch=1, grid=(m // tm, n_out // tn), in_specs=in_specs, out_specs=out_specs,
            scratch_shapes=[pltpu.VMEM((tm, d), BF16)]),
        out_shape=out_shape,
        input_output_aliases=aliases,
        compiler_params=_cparams(2),
        name="in_proj_ctx" if caches is not None else "in_proj_lat",
    )(*args)


def _ctx_attn_kernel(q_ref, k_ref, v_ref, o_ref, *, scale):
    for h in range(N_HEADS):
        sl = slice(h * HEAD_DIM, (h + 1) * HEAD_DIM)
        s = lax.dot_general(q_ref[:, sl], k_ref[:, sl], (((1,), (1,)), ((), ())),
                            preferred_element_type=F32) * scale
        e = jnp.exp(s - jnp.max(s, axis=-1, keepdims=True))
        p = e * (1.0 / jnp.sum(e, axis=-1, keepdims=True))
        o_ref[:, sl] = jnp.dot(p.astype(BF16), v_ref[:, sl], preferred_element_type=F32).astype(o_ref.dtype)


def _ctx_attention(proj, n_seq, seq):
    attn = N_HEADS * HEAD_DIM
    return pl.pallas_call(
        functools.partial(_ctx_attn_kernel, scale=HEAD_DIM ** -0.5),
        grid=(n_seq,),
        in_specs=[pl.BlockSpec((seq, attn), lambda b: (b, 0)),
                  pl.BlockSpec((seq, attn), lambda b: (b, 1)),
                  pl.BlockSpec((seq, attn), lambda b: (b, 2))],
        out_specs=pl.BlockSpec((seq, attn), lambda b: (b, 0)),
        out_shape=jax.ShapeDtypeStruct((n_seq * seq, attn), BF16),
        compiler_params=_cparams(1),
        name="ctx_attention",
    )(proj, proj, proj)


def _nbr_layout(rows):
    r_grp, win = NBR_ROWS_PER_GROUP, NA_KH + NBR_ROWS_PER_GROUP
    types, type_of_group = [], []
    for g in range(rows // r_grp):
        w0 = min(max(r_grp * g - NA_KH // 2, 0), rows - win)
        desc = []
        for qi in range(r_grp):
            r = r_grp * g + qi
            r0 = min(max(r - NA_KH // 2, 0), rows - NA_KH)
            desc.append(tuple((w0 + kj - r) if r0 <= w0 + kj < r0 + NA_KH else None for kj in range(win)))
        desc = tuple(desc)
        if desc not in types:
            types.append(desc)
        type_of_group.append(types.index(desc))
    pairs = sorted({(d[qi][2 * m], d[qi][2 * m + 1]) for d in types for qi in range(r_grp)
                    for m in range(win // 2)} - {(None, None)},
                   key=lambda p: tuple(-99 if v is None else v for v in p))
    return types, type_of_group, pairs


def _nbr_attn_kernel(l_ref, rpb_ref, q_ref, k_ref, v_ref, kc_ref, vc_ref, o_ref, bias_ref, pair_ref, *,
                     rows, layout, scale):
    types, type_of_group, pairs = layout
    w = GRID_W
    r_grp, win = NBR_ROWS_PER_GROUP, NA_KH + NBR_ROWS_PER_GROUP
    gq, wk = r_grp * w, win * w
    n_dr, n_dc = 2 * NA_KH - 1, 2 * NA_KW - 1
    neg_inf = float("-inf")
    h = pl.program_id(0)
    layer = l_ref[0]

    @pl.when(pl.program_id(1) == 0)
    def _build_bias_tables():
        cq = lax.broadcasted_iota(jnp.int32, (w, V7X_LANES), 0)
        lane = lax.broadcasted_iota(jnp.int32, (w, V7X_LANES), 1)
        ck = lane & (w - 1)
        c0 = jnp.clip(cq - NA_KW // 2, 0, w - NA_KW)
        in_win = (ck >= c0) & (ck < c0 + NA_KW)
        dc_idx = jnp.where(in_win, jnp.clip(ck - cq, -(NA_KW - 1), NA_KW - 1) + (NA_KW - 1), -1)
        left = lane < w
        for p in range(len(pairs)):
            pair_ref[p] = jnp.full((w, V7X_LANES), neg_inf, F32)

        def per_dc(dc, carry):
            hit = dc_idx == dc
            for p, (da, db) in enumerate(pairs):
                va = neg_inf if da is None else rpb_ref[layer, (h * n_dr + da + NA_KH - 1) * n_dc + dc]
                vb = neg_inf if db is None else rpb_ref[layer, (h * n_dr + db + NA_KH - 1) * n_dc + dc]
                pair_ref[p] = jnp.where(hit, jnp.where(left, va, vb) * LOG2_E, pair_ref[p])
            return carry

        lax.fori_loop(0, n_dc, per_dc, 0)
        for t, desc in enumerate(types):
            for qi in range(r_grp):
                for m in range(win // 2):
                    pr = (desc[qi][2 * m], desc[qi][2 * m + 1])
                    tile = (jnp.full((w, V7X_LANES), neg_inf, F32) if pr == (None, None)
                            else pair_ref[pairs.index(pr)])
                    bias_ref[t, qi * w:(qi + 1) * w, m * V7X_LANES:(m + 1) * V7X_LANES] = tile

    kc = kc_ref[...].astype(BF16)
    vc = vc_ref[...].astype(BF16)
    nt = (((1,), (1,)), ((), ()))
    scale2 = scale * LOG2_E
    default_type = max(set(type_of_group), key=type_of_group.count)

    def scores(g):
        w0 = jnp.clip(r_grp * g - NA_KH // 2, 0, rows - win)
        q0 = pl.multiple_of(g * gq, gq)
        k0 = pl.multiple_of(w0 * w, w)
        t = jnp.int32(default_type)
        for gi, ti in enumerate(type_of_group):
            if ti != default_type:
                t = jnp.where(g == gi, ti, t)
        q = q_ref[pl.ds(q0, gq), :]
        s = lax.dot_general(q, k_ref[pl.ds(k0, wk), :], nt, preferred_element_type=F32) * scale2 + bias_ref[t]
        sc = lax.dot_general(q, kc, nt, preferred_element_type=F32) * scale2
        return q0, k0, s, sc

    def attend(q0, k0, s, sc):
        mx = jnp.maximum(jnp.max(s, axis=-1, keepdims=True), jnp.max(sc, axis=-1, keepdims=True))
        e = jnp.exp2(s - mx)
        ec = jnp.exp2(sc - mx)
        inv = 1.0 / (jnp.sum(e, axis=-1, keepdims=True) + jnp.sum(ec, axis=-1, keepdims=True))
        o = (jnp.dot(e.astype(BF16), v_ref[pl.ds(k0, wk), :], preferred_element_type=F32)
             + jnp.dot(ec.astype(BF16), vc, preferred_element_type=F32))
        o_ref[pl.ds(q0, gq), :] = (o * inv).astype(o_ref.dtype)

    def per_pair(i, carry):
        first, second = scores(2 * i), scores(2 * i + 1)
        attend(*first)
        attend(*second)
        return carry

    lax.fori_loop(0, rows // (2 * r_grp), per_pair, 0, unroll=True)


def _nbr_attention(layer, proj, cache_k, cache_v, rpb, n_batch, n_tok):
    assert 2 * GRID_W == V7X_LANES and n_tok % GRID_W == 0
    rows = n_tok // GRID_W
    assert rows % (2 * NBR_ROWS_PER_GROUP) == 0 and rows >= NA_KH + NBR_ROWS_PER_GROUP and GRID_W >= NA_KW
    layout = _nbr_layout(rows)
    r_grp, win = NBR_ROWS_PER_GROUP, NA_KH + NBR_ROWS_PER_GROUP
    t_ctx = cache_k.shape[3]
    blk = lambda col0: pl.BlockSpec((n_tok, HEAD_DIM), lambda h, b, l: (b, col0 + h))
    ctx = pl.BlockSpec((None, None, None, t_ctx, HEAD_DIM), lambda h, b, l: (b, l[0], h, 0, 0))
    return pl.pallas_call(
        functools.partial(_nbr_attn_kernel, rows=rows, layout=layout, scale=HEAD_DIM ** -0.5),
        grid_spec=pltpu.PrefetchScalarGridSpec(
            num_scalar_prefetch=1, grid=(N_HEADS, n_batch),
            in_specs=[pl.BlockSpec(memory_space=pltpu.SMEM),
                      blk(0), blk(N_HEADS), blk(2 * N_HEADS), ctx, ctx],
            out_specs=pl.BlockSpec((n_tok, HEAD_DIM), lambda h, b, l: (b, h)),
            scratch_shapes=[pltpu.VMEM((len(layout[0]), r_grp * GRID_W, win * GRID_W), F32),
                            pltpu.VMEM((len(layout[2]), GRID_W, V7X_LANES), F32)]),
        out_shape=jax.ShapeDtypeStruct((n_batch * n_tok, N_HEADS * HEAD_DIM), BF16),
        compiler_params=_cparams(2),
        name="nbr_attention",
    )(_layer_arg(layer), rpb.reshape(rpb.shape[0], -1), proj, proj, proj, cache_k, cache_v)


@functools.lru_cache(maxsize=None)
def _dft_tables(n):
    jk = (np.arange(n, dtype=np.int64)[:, None] * np.arange(n, dtype=np.int64)[None, :]) % n
    ang = 2.0 * np.pi * jk.astype(np.float64) / n
    scale = 1.0 / math.sqrt(n)
    return ((np.cos(ang) * scale).astype(BF16), (np.sin(ang) * scale).astype(BF16))


@functools.lru_cache(maxsize=None)
def _mirror_matrix(t):
    r = np.zeros((t, t), np.float32)
    q = np.arange(1, t)
    r[q, t - q] = 1.0
    return r.astype(BF16)


def _fourier_channel_stage(f_ref, cc_ref, sc_ref, rev_ref, zc_ref, zs_ref, edge_ref):
    n, fd = f_ref.shape
    gd = fd // F_GROUPS
    half = n // 2
    tile = rev_ref.shape[0]
    n_tiles = half // tile
    for t in range(n_tiles):
        rows = slice(t * tile, (t + 1) * tile)
        lo = f_ref[rows, :].astype(F32)
        src = (2 * n_tiles - 1 - t) * tile
        mir = jnp.dot(rev_ref[...], f_ref[src:src + tile, :], preferred_element_type=F32)
        if t > 0:
            first = lax.broadcasted_iota(jnp.int32, (tile, 1), 0) == 0
            mir = jnp.where(first, f_ref[src + tile:src + tile + 1, :].astype(F32), mir)
        x_even = (lo + mir).astype(BF16)
        x_odd = (lo - mir).astype(BF16)
        for g in range(F_GROUPS):
            sl = slice(g * gd, (g + 1) * gd)
            zc_ref[rows, sl] = jnp.dot(x_even[:, sl], cc_ref[...], preferred_element_type=F32).astype(BF16)
            zs_ref[rows, sl] = jnp.dot(x_odd[:, sl], sc_ref[...], preferred_element_type=F32).astype(BF16)
    sign = 1.0 - 2.0 * (lax.broadcasted_iota(jnp.int32, (V7X_F32_SUBLANES, 1), 0) & 1).astype(F32)
    x_half = f_ref[half:half + V7X_BF16_SUBLANES, :]
    for g in range(F_GROUPS):
        sl = slice(g * gd, (g + 1) * gd)
        z_half = jnp.dot(x_half[:, sl], cc_ref[...], preferred_element_type=F32)[0:1, :]
        edge_ref[:, sl] = sign * (z_half * (1.0 / math.sqrt(n)))


def _fourier_kernel(f_ref, cc_ref, sc_ref, rev_ref, wc_ref, ws_ref, o_ref, zc_ref, zs_ref, edge_ref):
    fd = f_ref.shape[1]

    @pl.when(pl.program_id(1) == 0)
    def _():
        _fourier_channel_stage(f_ref, cc_ref, sc_ref, rev_ref, zc_ref, zs_ref, edge_ref)

    y = (jnp.dot(wc_ref[...], zc_ref[...], preferred_element_type=F32)
         + jnp.dot(ws_ref[...], zs_ref[...], preferred_element_type=F32))
    tm = y.shape[0]
    y = y.reshape(tm // V7X_F32_SUBLANES, V7X_F32_SUBLANES, fd) + edge_ref[...][None]
    o_ref[...] = y.reshape(tm, fd).astype(o_ref.dtype)


def _fourier_fold_kernel(f_ref, cc_ref, sc_ref, rev_ref, wc_ref, ws_ref, wh_ref, o_ref, zc_ref, zs_ref,
                         edge_ref, next_ref):
    fd = f_ref.shape[1]
    t_rows = wc_ref.shape[0]
    sub = rev_ref.shape[0]
    n_sub = t_rows // sub

    @pl.when(pl.program_id(1) == 0)
    def _():
        _fourier_channel_stage(f_ref, cc_ref, sc_ref, rev_ref, zc_ref, zs_ref, edge_ref)
        y_half = jnp.dot(wh_ref[...], zc_ref[...], preferred_element_type=F32)[0:1, :] + edge_ref[0:1, :]
        next_ref[...] = jnp.broadcast_to(y_half, next_ref.shape)

    e = jnp.dot(wc_ref[...], zc_ref[...], preferred_element_type=F32)
    e = (e.reshape(t_rows // V7X_F32_SUBLANES, V7X_F32_SUBLANES, fd) + edge_ref[...][None]).reshape(t_rows, fd)
    o = -jnp.dot(ws_ref[...], zs_ref[...], preferred_element_type=F32)
    o_ref[0] = (e - o).astype(o_ref.dtype)
    upper = e + o
    upper_bf = upper.astype(BF16)
    first = lax.broadcasted_iota(jnp.int32, (sub, 1), 0) == 0
    for j in range(n_sub):
        src = n_sub - 1 - j
        blk = jnp.dot(rev_ref[...], upper_bf[src * sub:(src + 1) * sub, :], preferred_element_type=F32)
        row0 = next_ref[0:1, :] if j == 0 else upper[(n_sub - j) * sub:(n_sub - j) * sub + 1, :]
        o_ref[1, j * sub:(j + 1) * sub, :] = jnp.where(first, row0, blk).astype(o_ref.dtype)
    next_ref[...] = jnp.broadcast_to(upper[0:1, :], next_ref.shape)


def _fourier(proj, n_seq, seq):
    attn = N_HEADS * HEAD_DIM
    fd = attn
    gd = fd // F_GROUPS
    cc, sc = _dft_tables(gd)
    cp, sp = _dft_tables(seq)
    half = seq // 2
    tm, tile = min(TM_FOURIER, seq), min(TM_FOURIER, half)
    assert seq % tm == 0 and half % tile == 0 and tile % V7X_BF16_SUBLANES == 0 and proj.shape[1] // fd > 3
    consts = (jnp.asarray(cc), jnp.asarray(sc), jnp.asarray(_mirror_matrix(tile)),
              jnp.asarray(cp[:, :half]), jnp.asarray(-sp[:, :half]))
    const_specs = [pl.BlockSpec((gd, gd), lambda b, i: (0, 0)),
                   pl.BlockSpec((gd, gd), lambda b, i: (0, 0)),
                   pl.BlockSpec((tile, tile), lambda b, i: (0, 0))]
    row_tile = pltpu.VMEM((V7X_F32_SUBLANES, fd), F32)
    scratch = [pltpu.VMEM((half, fd), BF16), pltpu.VMEM((half, fd), BF16), row_tile]
    if half % TM_MERGE == 0:
        t_rows, n_tiles = TM_MERGE, half // TM_MERGE
        w_half = np.zeros((V7X_BF16_SUBLANES, half), np.float32)
        w_half[0] = cp[half, :half].astype(np.float32)
        rows = lambda b, i: (n_tiles - 1 - i, 0)
        return pl.pallas_call(
            _fourier_fold_kernel,
            grid=(n_seq, n_tiles),
            in_specs=[pl.BlockSpec((seq, fd), lambda b, i: (b, 3))] + const_specs
            + [pl.BlockSpec((t_rows, half), rows), pl.BlockSpec((t_rows, half), rows),
               pl.BlockSpec((V7X_BF16_SUBLANES, half), lambda b, i: (0, 0))],
            out_specs=pl.BlockSpec((None, None, 2, t_rows, fd), lambda b, i: (b, n_tiles - 1 - i, 0, 0, 0)),
            out_shape=jax.ShapeDtypeStruct((n_seq, n_tiles, 2, t_rows, fd), BF16),
            scratch_shapes=scratch + [row_tile],
            compiler_params=_cparams(2),
            name="fourier_fold_%d" % seq,
        )(proj, *consts, jnp.asarray(w_half.astype(BF16)))
    return pl.pallas_call(
        _fourier_kernel,
        grid=(n_seq, seq // tm),
        in_specs=[pl.BlockSpec((seq, fd), lambda b, i: (b, 3))] + const_specs
        + [pl.BlockSpec((tm, half), lambda b, i: (i, 0)), pl.BlockSpec((tm, half), lambda b, i: (i, 0))],
        out_specs=pl.BlockSpec((tm, fd), lambda b, i: (b * (seq // tm) + i, 0)),
        out_shape=jax.ShapeDtypeStruct((n_seq * seq, fd), BF16),
        scratch_shapes=scratch,
        compiler_params=_cparams(2),
        name="fourier_%d" % seq,
    )(proj, *consts)


def _merge_out_kernel(l_ref, x_ref, a_ref, fo_ref, ga_ref, gb_ref, mod_ref, wpa_ref, wpb_ref, wout_ref,
                      o_ref, y_ref, *, tc):
    d = x_ref.shape[1]
    for c in range(d // tc):
        sl = slice(c * tc, (c + 1) * tc)
        ya = jnp.dot(a_ref[...], wpa_ref[:, sl], preferred_element_type=F32)
        yb = jnp.dot(fo_ref[...], wpb_ref[:, sl], preferred_element_type=F32)
        y = (jax.nn.sigmoid(ga_ref[:, sl].astype(F32)) * ya
             + jax.nn.sigmoid(gb_ref[:, sl].astype(F32)) * yb)
        y_ref[:, sl] = y.astype(BF16)
    for c in range(d // tc):
        sl = slice(c * tc, (c + 1) * tc)
        o = jnp.dot(y_ref[...], wout_ref[:, sl], preferred_element_type=F32)
        o_ref[:, sl] = x_ref[:, sl] + mod_ref[2:3, sl] * o


def _merge_out(layer, x, a, fo, proj, mods, w_pa, w_pb, w_out, *, rows_per_mod, mod_row0):
    m, d = x.shape
    attn, fd = a.shape[1], fo.shape[-1]
    tm = min(TM_MERGE, m)
    assert m % tm == 0 and (3 * attn + fd) % d == 0 and rows_per_mod % tm == 0
    gate_blk0 = (3 * attn + fd) // d
    whole = lambda arr: pl.BlockSpec((None,) + arr.shape[1:], lambda i, l: (l[0], 0, 0),
                                     pipeline_mode=pl.Buffered(1))
    if fo.ndim == 2:
        fo_spec = pl.BlockSpec((tm, fd), lambda i, l: (i, 0))
    else:
        n_tiles = fo.shape[1]
        assert fo.shape[2:4] == (2, tm) and fo.shape[0] * 2 * n_tiles * tm == m

        def folded(i, l):
            j = i % (2 * n_tiles)
            upper = j >= n_tiles
            return (i // (2 * n_tiles), jnp.where(upper, 2 * n_tiles - 1 - j, j), upper.astype(jnp.int32), 0, 0)

        fo_spec = pl.BlockSpec((None, None, None, tm, fd), folded)
    return pl.pallas_call(
        functools.partial(_merge_out_kernel, tc=min(TC_MERGE, d)),
        grid_spec=pltpu.PrefetchScalarGridSpec(
            num_scalar_prefetch=1, grid=(m // tm,),
            in_specs=[pl.BlockSpec((tm, d), lambda i, l: (i, 0)),
                      pl.BlockSpec((tm, attn), lambda i, l: (i, 0)),
                      fo_spec,
                      pl.BlockSpec((tm, d), lambda i, l: (i, gate_blk0)),
                      pl.BlockSpec((tm, d), lambda i, l: (i, gate_blk0 + 1)),
                      pl.BlockSpec((None, None, N_MOD, d),
                                   lambda i, l: (l[0], mod_row0 + (i * tm) // rows_per_mod, 0, 0)),
                      whole(w_pa), whole(w_pb), whole(w_out)],
            out_specs=pl.BlockSpec((tm, d), lambda i, l: (i, 0)),
            scratch_shapes=[pltpu.VMEM((tm, d), BF16)]),
        out_shape=jax.ShapeDtypeStruct((m, d), F32),
        compiler_params=_cparams(1),
        name="merge_out",
    )(_layer_arg(layer), x, a, fo, proj, proj, mods, w_pa, w_pb, w_out)


def _ffn_kernel(l_ref, x_ref, xp_ref, xn_ref, mod_ref, gains_ref, wa_ref, wg_ref, cwa_ref, cwg_ref, wd_ref,
                o_ref, h_ref, u_ref, *maybe_acc, seq, tc, final_norm):
    tm = x_ref.shape[0]
    halo = xp_ref.shape[0]
    tf = wa_ref.shape[1]
    seg = min(seq, tm)
    n_seg, stride = tm // seg, seg + halo
    work_rows = h_ref.shape[0]
    pad = u_ref.shape[2] - work_rows
    last_halo = work_rows - halo
    acc_ref = maybe_acc[0] if maybe_acc else o_ref
    n_rows = acc_ref.shape[0]
    i, f = pl.program_id(0), pl.program_id(1)

    @pl.when(f == 0)
    def _():
        g, scale, shift = gains_ref[0:1, :], mod_ref[4:5, :], mod_ref[3:4, :]
        gs = g * (1.0 + scale)
        tok0 = i * tm
        for s in range(n_seg):
            _norm_mod_store(h_ref, s * stride, x_ref, g, scale, shift, src_row0=s * seg, n=seg)
            if s < n_seg - 1:
                h_ref[s * stride + seg:(s + 1) * stride, :] = jnp.zeros((halo, h_ref.shape[1]), BF16)
        after = jnp.where(((tok0 + tm) & (seq - 1)) != 0, _norm_mod_rows(xn_ref[...], gs, shift)[0:1, :], 0.0)
        before = jnp.where((tok0 & (seq - 1)) != 0, _norm_mod_rows(xp_ref[...], gs, shift)[halo - 1:halo, :], 0.0)
        r = lax.broadcasted_iota(jnp.int32, (halo, 1), 0)
        h_ref[last_halo:, :] = jnp.where(r == 0, after, jnp.where(r == 1, before, 0.0)).astype(BF16)
        acc_ref[...] = jnp.zeros_like(acc_ref)

    h = h_ref[...]

    def conv(c, idx, cw_ref, cs):
        return (u_ref[c, idx, pl.ds(pad - 1, n_rows), :] * cw_ref[0:1, cs]
                + u_ref[c, idx, pl.ds(pad, n_rows), :] * cw_ref[1:2, cs]
                + u_ref[c, idx, pl.ds(pad + 1, n_rows), :] * cw_ref[2:3, cs] + cw_ref[3:4, cs])

    def up_project(c, idx, w_ref, cs):
        res = jnp.dot(h, w_ref[:, cs], preferred_element_type=F32)
        u_ref[c, idx, pl.ds(pad, work_rows), :] = res
        u_ref[c, idx, pad - 1:pad, :] = res[last_halo + 1:last_halo + 2, :]

    n_chunks = tf // tc
    for c in range(n_chunks):
        cs = slice(c * tc, (c + 1) * tc)
        up_project(c, 0, wa_ref, cs)
        up_project(c, 1, wg_ref, cs)
    for c in range(n_chunks):
        cs = slice(c * tc, (c + 1) * tc)
        ca = conv(c, 0, cwa_ref, cs)
        cg = conv(c, 1, cwg_ref, cs)
        act = ((cg * jax.nn.sigmoid(cg)) * ca).astype(BF16)
        acc_ref[...] += jnp.dot(act, wd_ref[cs, :], preferred_element_type=F32)

    @pl.when(f == pl.num_programs(1) - 1)
    def _():
        chunk = min(seg, FFN_OUT_ROW_CHUNK)
        for s in range(n_seg):
            for r0 in range(0, seg, chunk):
                rows = slice(s * seg + r0, s * seg + r0 + chunk)
                y = x_ref[rows, :] + mod_ref[5:6, :] * acc_ref[s * stride + r0:s * stride + r0 + chunk, :]
                if final_norm:
                    y = (y * lax.rsqrt(jnp.mean(y * y, axis=-1, keepdims=True) + RMS_EPS)) * gains_ref[1:2, :]
                o_ref[rows, :] = y


def _conv_ffn(layer, x, mods, gains, w_up, conv_wb, w_down, *, seq, rows_per_mod, mod_row0, final_norm):
    m, d = x.shape
    ffn = w_down.shape[1]
    long_rows = seq >= TM_FFN_LONG and m % TM_FFN_LONG == 0
    tm, tf = (TM_FFN_LONG if long_rows else min(TM_FFN, m)), min(TF_FFN, ffn)
    x_mode = dict(pipeline_mode=pl.Buffered(1)) if long_rows else {}
    tc = min(TC_FFN, tf)
    halo = V7X_BF16_SUBLANES
    assert seq & (seq - 1) == 0 and (seq % tm == 0 or tm % seq == 0) and m % seq == 0
    assert m % tm == 0 and ffn % tf == 0 and tf % tc == 0 and min(seq, tm) % halo == 0 and rows_per_mod % tm == 0
    nf, nhb = ffn // tf, m // halo
    n_seg = tm // min(seq, tm)
    work_rows = tm + n_seg * halo
    acc = [pltpu.VMEM((work_rows - halo, d), F32)] if n_seg > 1 else []
    mat = lambda rows, cols, imap: pl.BlockSpec((None, rows, cols), imap)
    return pl.pallas_call(
        functools.partial(_ffn_kernel, seq=seq, tc=tc, final_norm=final_norm),
        grid_spec=pltpu.PrefetchScalarGridSpec(
            num_scalar_prefetch=1, grid=(m // tm, nf),
            in_specs=[pl.BlockSpec((tm, d), lambda i, f, l: (i, 0), **x_mode),
                      pl.BlockSpec((halo, d), lambda i, f, l: (jnp.maximum(i * (tm // halo) - 1, 0), 0)),
                      pl.BlockSpec((halo, d),
                                   lambda i, f, l: (jnp.minimum((i + 1) * (tm // halo), nhb - 1), 0)),
                      pl.BlockSpec((None, None, N_MOD, d),
                                   lambda i, f, l: (l[0], mod_row0 + (i * tm) // rows_per_mod, 0, 0)),
                      mat(2, d, lambda i, f, l: (l[0], 0, 0)),
                      mat(d, tf, lambda i, f, l: (l[0], 0, f)),
                      mat(d, tf, lambda i, f, l: (l[0], 0, nf + f)),
                      mat(4, tf, lambda i, f, l: (l[0], 0, f)),
                      mat(4, tf, lambda i, f, l: (l[0], 0, nf + f)),
                      mat(tf, d, lambda i, f, l: (l[0], f, 0))],
            out_specs=pl.BlockSpec((tm, d), lambda i, f, l: (i, 0)),
            scratch_shapes=[pltpu.VMEM((work_rows, d), BF16),
                            pltpu.VMEM((tf // tc, 2, V7X_F32_SUBLANES + work_rows, tc), F32)] + acc),
        out_shape=jax.ShapeDtypeStruct((m, d), F32),
        compiler_params=_cparams(2),
        name="conv_ffn_final" if final_norm else "conv_ffn",
    )(_layer_arg(layer), x, x, x, mods, gains, w_up, w_up, conv_wb, conv_wb, w_down)


def kernel(x_prompt, x_sample, cache_k, cache_v, c, c_ctx, w_mod, b_mod, norm1_g, w_in, rpb, w_pa, w_pb,
           w_out, norm2_g, w_up, conv_w, conv_b, w_down, norm_f_g):
    n_ctx, seq, d = x_prompt.shape
    n_lat, n_tok, _ = x_sample.shape
    n_layers = w_mod.shape[0]
    assert n_lat < MOD_ROWS and n_layers >= 1

    c_all = jnp.zeros((MOD_ROWS, d), F32).at[:n_lat].set(c).at[n_lat].set(c_ctx)
    mods = _modulation(c_all, w_mod, b_mod).reshape(n_layers, MOD_ROWS, N_MOD, d)
    w_in, w_pa, w_pb, w_out, w_up, w_down = (w.astype(BF16) for w in (w_in, w_pa, w_pb, w_out, w_up, w_down))

    ffn_gains = jnp.stack([norm2_g, jnp.broadcast_to(norm_f_g, norm2_g.shape)], axis=1)
    conv_wb = jnp.concatenate([conv_w, conv_b[:, None, :]], axis=1)

    xp = x_prompt.reshape(n_ctx * seq, d)
    xs = x_sample.reshape(n_lat * n_tok, d)
    ctx_mod = dict(rows_per_mod=n_ctx * seq, mod_row0=n_lat)
    lat_mod = dict(rows_per_mod=n_tok, mod_row0=0)
    caches = (jnp.zeros((n_ctx, n_layers, N_HEADS, seq, HEAD_DIM), F32),) * 2
    for l in range(n_layers):
        last = l == n_layers - 1
        ffn_w = (mods, ffn_gains, w_up, conv_wb, w_down)
        proj, *caches = _in_proj(l, xp, mods, norm1_g, w_in, caches=caches, **ctx_mod)
        a = _ctx_attention(proj, n_ctx, seq)
        fo = _fourier(proj, n_ctx, seq)
        xp = _merge_out(l, xp, a, fo, proj, mods, w_pa, w_pb, w_out, **ctx_mod)
        xp = _conv_ffn(l, xp, *ffn_w, seq=seq, final_norm=last, **ctx_mod)

        (proj,) = _in_proj(l, xs, mods, norm1_g, w_in, **lat_mod)
        a = _nbr_attention(l, proj, cache_k, cache_v, rpb, n_lat, n_tok)
        fo = _fourier(proj, n_lat, n_tok)
        xs = _merge_out(l, xs, a, fo, proj, mods, w_pa, w_pb, w_out, **lat_mod)
        xs = _conv_ffn(l, xs, *ffn_w, seq=n_tok, final_norm=last, **lat_mod)

    return (xp.reshape(n_ctx, seq, d), xs.reshape(n_lat, n_tok, d), caches[0], caches[1])
```

```python
import functools
import math

import numpy as np
import jax
import jax.numpy as jnp
from jax import lax
from jax.experimental import pallas as pl
from jax.experimental.pallas import tpu as pltpu

F32 = jnp.float32
BF16 = jnp.bfloat16

RMS_EPS = 1e-6
LOG2_E = math.log2(math.e)
N_MOD = 6
N_HEADS = 8
HEAD_DIM = 128
GRID_W = 64
NA_KH = 8
NA_KW = 16
F_GROUPS = 4
NBR_ROWS_PER_GROUP = 4
MOD_ROWS = 8
V7X_LANES = 128
V7X_F32_SUBLANES = 8
V7X_BF16_SUBLANES = 16
V7X_VMEM_BYTES = 64 * 1024 * 1024
VMEM_LIMIT = V7X_VMEM_BYTES - 8 * 1024 * 1024

TM_IN, TN_IN, TN_IN_WIDE = 1024, 1024, 2048
TM_IN_CACHE = 512
TN_MOD = 1024
TM_MERGE, TC_MERGE = 512, 512
TM_FFN, TF_FFN, TC_FFN = 512, 512, 256
TM_FOURIER = 256
NORM_ROW_CHUNK = 16
NORM_UNROLL = 8


def _cparams(n_axes):
    return pltpu.CompilerParams(dimension_semantics=("arbitrary",) * n_axes,
                                vmem_limit_bytes=VMEM_LIMIT)


def _layer_arg(layer):
    return jnp.full((1,), layer, jnp.int32)


def _norm_mod_rows(x, gs, shift):
    return (x * lax.rsqrt(jnp.mean(x * x, axis=-1, keepdims=True) + RMS_EPS)) * gs + shift


def _norm_mod_store(dst_ref, dst_row0, src_ref, g, scale, shift, src_row0=0, n=None):
    n = src_ref.shape[0] if n is None else n
    chunk = min(NORM_ROW_CHUNK, n)
    assert n % chunk == 0 and src_row0 % chunk == 0 and dst_row0 % chunk == 0
    gs = g * (1.0 + scale)

    def rows(r, carry):
        r0 = pl.multiple_of(r * chunk, chunk)
        h = _norm_mod_rows(src_ref[pl.ds(src_row0 + r0, chunk), :], gs, shift)
        dst_ref[pl.ds(dst_row0 + r0, chunk), :] = h.astype(BF16)
        return carry

    lax.fori_loop(0, n // chunk, rows, 0, unroll=min(NORM_UNROLL, n // chunk))


def _mod_kernel(c_ref, w_ref, b_ref, o_ref):
    cv = c_ref[...]
    s = (cv * jax.nn.sigmoid(cv)).astype(BF16)
    o_ref[...] = jnp.dot(s, w_ref[...].astype(BF16), preferred_element_type=F32) + b_ref[...]


def _modulation(c_all, w_mod, b_mod):
    n_layers, d, nd = w_mod.shape
    tn = min(TN_MOD, nd)
    assert nd % tn == 0
    return pl.pallas_call(
        _mod_kernel,
        grid=(n_layers, nd // tn),
        in_specs=[pl.BlockSpec((MOD_ROWS, d), lambda l, j: (0, 0)),
                  pl.BlockSpec((None, d, tn), lambda l, j: (l, 0, j)),
                  pl.BlockSpec((None, 1, tn), lambda l, j: (l, 0, j))],
        out_specs=pl.BlockSpec((None, MOD_ROWS, tn), lambda l, j: (l, 0, j)),
        out_shape=jax.ShapeDtypeStruct((n_layers, MOD_ROWS, nd), F32),
        compiler_params=_cparams(2),
        name="modulation",
    )(c_all, w_mod, b_mod.reshape(n_layers, 1, nd))


def _in_proj_kernel(l_ref, x_ref, mod_ref, g_ref, w_ref, *rest, kv_tiles):
    h_ref = rest[-1]
    j = pl.program_id(1)

    @pl.when(j == 0)
    def _():
        _norm_mod_store(h_ref, 0, x_ref, g_ref[...], mod_ref[1:2, :], mod_ref[0:1, :])

    res = jnp.dot(h_ref[...], w_ref[...], preferred_element_type=F32)
    if kv_tiles is None:
        o_ref = rest[0]
        o_ref[...] = res.astype(o_ref.dtype)
    else:
        o_ref, kc_ref, vc_ref = rest[-4], rest[-3], rest[-2]
        o_ref[...] = res.astype(o_ref.dtype)
        (k_tile, k_col), (v_tile, v_col) = kv_tiles
        nb, nh, s, dh = kc_ref.shape

        def scatter(dst_ref, col0):
            for bb in range(nb):
                for hh in range(nh):
                    dst_ref[bb, hh] = res[bb * s:(bb + 1) * s, col0 + hh * dh:col0 + (hh + 1) * dh]

        @pl.when(j == k_tile)
        def _():
            scatter(kc_ref, k_col)

        @pl.when(j == v_tile)
        def _():
            scatter(vc_ref, v_col)


def _in_proj(layer, x, mods, g, w, *, rows_per_mod, mod_row0, caches=None):
    m, d = x.shape
    n_out = w.shape[2]
    tm = min(TM_IN if caches is None else TM_IN_CACHE, m)
    tn = min(TN_IN_WIDE if n_out % TN_IN_WIDE == 0 else TN_IN, n_out)
    assert m % tm == 0 and n_out % tn == 0 and (rows_per_mod % tm == 0 or tm % rows_per_mod == 0)
    attn = N_HEADS * HEAD_DIM
    in_specs = [pl.BlockSpec((tm, d), lambda i, j, l: (i, 0)),
                pl.BlockSpec((None, None, N_MOD, d),
                             lambda i, j, l: (l[0], mod_row0 + (i * tm) // rows_per_mod, 0, 0)),
                pl.BlockSpec((None, 1, d), lambda i, j, l: (l[0], 0, 0)),
                pl.BlockSpec((None, d, tn), lambda i, j, l: (l[0], 0, j))]
    out_specs = [pl.BlockSpec((tm, tn), lambda i, j, l: (i, j))]
    out_shape = [jax.ShapeDtypeStruct((m, n_out), BF16)]
    args = [_layer_arg(layer), x, mods, g.reshape(g.shape[0], 1, d), w]
    kv_tiles, aliases = None, {}
    if caches is not None:
        seq = caches[0].shape[3]
        kv_tiles = tuple(divmod(c * attn, tn) for c in (1, 2))
        assert tm % seq == 0 and all(col + attn <= tn for _, col in kv_tiles)
        blk = pl.BlockSpec((tm // seq, None, N_HEADS, seq, HEAD_DIM), lambda i, j, l: (i, l[0], 0, 0, 0))
        out_specs += [blk, blk]
        out_shape += [jax.ShapeDtypeStruct(c.shape, c.dtype) for c in caches]
        in_specs += [pl.BlockSpec(memory_space=pl.ANY)] * 2
        aliases = {len(args): 1, len(args) + 1: 2}
        args += list(caches)
    return pl.pallas_call(
        functools.partial(_in_proj_kernel, kv_tiles=kv_tiles),
        grid_spec=pltpu.PrefetchScalarGridSpec(
            num_scalar_prefetch=1, grid=(m // tm, n_out // tn), in_specs=in_specs, out_specs=out_specs,
            scratch_shapes=[pltpu.VMEM((tm, d), BF16)]),
        out_shape=out_shape,
        input_output_aliases=aliases,
        compiler_params=_cparams(2),
        name="in_proj_ctx" if caches is not None else "in_proj_lat",
    )(*args)


def _ctx_attn_kernel(q_ref, k_ref, v_ref, o_ref, *, scale):
    for h in range(N_HEADS):
        sl = slice(h * HEAD_DIM, (h + 1) * HEAD_DIM)
        s = lax.dot_general(q_ref[:, sl], k_ref[:, sl], (((1,), (1,)), ((), ())),
                            preferred_element_type=F32) * scale
        e = jnp.exp(s - jnp.max(s, axis=-1, keepdims=True))
        p = e * (1.0 / jnp.sum(e, axis=-1, keepdims=True))
        o_ref[:, sl] = jnp.dot(p.astype(BF16), v_ref[:, sl], preferred_element_type=F32).astype(o_ref.dtype)


def _ctx_attention(proj, n_seq, seq):
    attn = N_HEADS * HEAD_DIM
    return pl.pallas_call(
        functools.partial(_ctx_attn_kernel, scale=HEAD_DIM ** -0.5),
        grid=(n_seq,),
        in_specs=[pl.BlockSpec((seq, attn), lambda b: (b, 0)),
                  pl.BlockSpec((seq, attn), lambda b: (b, 1)),
                  pl.BlockSpec((seq, attn), lambda b: (b, 2))],
        out_specs=pl.BlockSpec((seq, attn), lambda b: (b, 0)),
        out_shape=jax.ShapeDtypeStruct((n_seq * seq, attn), BF16),
        compiler_params=_cparams(1),
        name="ctx_attention",
    )(proj, proj, proj)


def _nbr_layout(rows):
    r_grp, win = NBR_ROWS_PER_GROUP, NA_KH + NBR_ROWS_PER_GROUP
    types, type_of_group = [], []
    for g in range(rows // r_grp):
        w0 = min(max(r_grp * g - NA_KH // 2, 0), rows - win)
        desc = []
        for qi in range(r_grp):
            r = r_grp * g + qi
            r0 = min(max(r - NA_KH // 2, 0), rows - NA_KH)
            desc.append(tuple((w0 + kj - r) if r0 <= w0 + kj < r0 + NA_KH else None for kj in range(win)))
        desc = tuple(desc)
        if desc not in types:
            types.append(desc)
        type_of_group.append(types.index(desc))
    pairs = sorted({(d[qi][2 * m], d[qi][2 * m + 1]) for d in types for qi in range(r_grp)
                    for m in range(win // 2)} - {(None, None)},
                   key=lambda p: tuple(-99 if v is None else v for v in p))
    return types, type_of_group, pairs


def _nbr_attn_kernel(l_ref, rpb_ref, q_ref, k_ref, v_ref, kc_ref, vc_ref, o_ref, bias_ref, pair_ref, *,
                     rows, layout, scale):
    types, type_of_group, pairs = layout
    w = GRID_W
    r_grp, win = NBR_ROWS_PER_GROUP, NA_KH + NBR_ROWS_PER_GROUP
    gq, wk = r_grp * w, win * w
    n_dr, n_dc = 2 * NA_KH - 1, 2 * NA_KW - 1
    neg_inf = float("-inf")
    h = pl.program_id(0)
    layer = l_ref[0]

    @pl.when(pl.program_id(1) == 0)
    def _build_bias_tables():
        cq = lax.broadcasted_iota(jnp.int32, (w, V7X_LANES), 0)
        lane = lax.broadcasted_iota(jnp.int32, (w, V7X_LANES), 1)
        ck = lane & (w - 1)
        c0 = jnp.clip(cq - NA_KW // 2, 0, w - NA_KW)
        in_win = (ck >= c0) & (ck < c0 + NA_KW)
        dc_idx = jnp.where(in_win, jnp.clip(ck - cq, -(NA_KW - 1), NA_KW - 1) + (NA_KW - 1), -1)
        left = lane < w
        for p in range(len(pairs)):
            pair_ref[p] = jnp.full((w, V7X_LANES), neg_inf, F32)

        def per_dc(dc, carry):
            hit = dc_idx == dc
            for p, (da, db) in enumerate(pairs):
                va = neg_inf if da is None else rpb_ref[layer, (h * n_dr + da + NA_KH - 1) * n_dc + dc]
                vb = neg_inf if db is None else rpb_ref[layer, (h * n_dr + db + NA_KH - 1) * n_dc + dc]
                pair_ref[p] = jnp.where(hit, jnp.where(left, va, vb) * LOG2_E, pair_ref[p])
            return carry

        lax.fori_loop(0, n_dc, per_dc, 0)
        for t, desc in enumerate(types):
            for qi in range(r_grp):
                for m in range(win // 2):
                    pr = (desc[qi][2 * m], desc[qi][2 * m + 1])
                    tile = (jnp.full((w, V7X_LANES), neg_inf, F32) if pr == (None, None)
                            else pair_ref[pairs.index(pr)])
                    bias_ref[t, qi * w:(qi + 1) * w, m * V7X_LANES:(m + 1) * V7X_LANES] = tile

    kc = kc_ref[...].astype(BF16)
    vc = vc_ref[...].astype(BF16)
    nt = (((1,), (1,)), ((), ()))
    scale2 = scale * LOG2_E
    default_type = max(set(type_of_group), key=type_of_group.count)

    def scores(g):
        w0 = jnp.clip(r_grp * g - NA_KH // 2, 0, rows - win)
        q0 = pl.multiple_of(g * gq, gq)
        k0 = pl.multiple_of(w0 * w, w)
        t = jnp.int32(default_type)
        for gi, ti in enumerate(type_of_group):
            if ti != default_type:
                t = jnp.where(g == gi, ti, t)
        q = q_ref[pl.ds(q0, gq), :]
        s = lax.dot_general(q, k_ref[pl.ds(k0, wk), :], nt, preferred_element_type=F32) * scale2 + bias_ref[t]
        sc = lax.dot_general(q, kc, nt, preferred_element_type=F32) * scale2
        return q0, k0, s, sc

    def attend(q0, k0, s, sc):
        mx = jnp.maximum(jnp.max(s, axis=-1, keepdims=True), jnp.max(sc, axis=-1, keepdims=True))
        e = jnp.exp2(s - mx)
        ec = jnp.exp2(sc - mx)
        inv = 1.0 / (jnp.sum(e, axis=-1, keepdims=True) + jnp.sum(ec, axis=-1, keepdims=True))
        o = (jnp.dot(e.astype(BF16), v_ref[pl.ds(k0, wk), :], preferred_element_type=F32)
             + jnp.dot(ec.astype(BF16), vc, preferred_element_type=F32))
        o_ref[pl.ds(q0, gq), :] = (o * inv).astype(o_ref.dtype)

    def per_pair(i, carry):
        first, second = scores(2 * i), scores(2 * i + 1)
        attend(*first)
        attend(*second)
        return carry

    lax.fori_loop(0, rows // (2 * r_grp), per_pair, 0, unroll=True)


def _nbr_attention(layer, proj, cache_k, cache_v, rpb, n_batch, n_tok):
    assert 2 * GRID_W == V7X_LANES and n_tok % GRID_W == 0
    rows = n_tok // GRID_W
    assert rows % (2 * NBR_ROWS_PER_GROUP) == 0 and rows >= NA_KH + NBR_ROWS_PER_GROUP and GRID_W >= NA_KW
    layout = _nbr_layout(rows)
    r_grp, win = NBR_ROWS_PER_GROUP, NA_KH + NBR_ROWS_PER_GROUP
    t_ctx = cache_k.shape[3]
    blk = lambda col0: pl.BlockSpec((n_tok, HEAD_DIM), lambda h, b, l: (b, col0 + h))
    ctx = pl.BlockSpec((None, None, None, t_ctx, HEAD_DIM), lambda h, b, l: (b, l[0], h, 0, 0))
    return pl.pallas_call(
        functools.partial(_nbr_attn_kernel, rows=rows, layout=layout, scale=HEAD_DIM ** -0.5),
        grid_spec=pltpu.PrefetchScalarGridSpec(
            num_scalar_prefetch=1, grid=(N_HEADS, n_batch),
            in_specs=[pl.BlockSpec(memory_space=pltpu.SMEM),
                      blk(0), blk(N_HEADS), blk(2 * N_HEADS), ctx, ctx],
            out_specs=pl.BlockSpec((n_tok, HEAD_DIM), lambda h, b, l: (b, h)),
            scratch_shapes=[pltpu.VMEM((len(layout[0]), r_grp * GRID_W, win * GRID_W), F32),
                            pltpu.VMEM((len(layout[2]), GRID_W, V7X_LANES), F32)]),
        out_shape=jax.ShapeDtypeStruct((n_batch * n_tok, N_HEADS * HEAD_DIM), BF16),
        compiler_params=_cparams(2),
        name="nbr_attention",
    )(_layer_arg(layer), rpb.reshape(rpb.shape[0], -1), proj, proj, proj, cache_k, cache_v)


@functools.lru_cache(maxsize=None)
def _dft_tables(n):
    jk = (np.arange(n, dtype=np.int64)[:, None] * np.arange(n, dtype=np.int64)[None, :]) % n
    ang = 2.0 * np.pi * jk.astype(np.float64) / n
    scale = 1.0 / math.sqrt(n)
    return ((np.cos(ang) * scale).astype(BF16), (np.sin(ang) * scale).astype(BF16))


@functools.lru_cache(maxsize=None)
def _mirror_matrix(t):
    r = np.zeros((t, t), np.float32)
    q = np.arange(1, t)
    r[q, t - q] = 1.0
    return r.astype(BF16)


def _fourier_channel_stage(f_ref, cc_ref, sc_ref, rev_ref, zc_ref, zs_ref, edge_ref):
    n, fd = f_ref.shape
    gd = fd // F_GROUPS
    half = n // 2
    tile = rev_ref.shape[0]
    n_tiles = half // tile
    for t in range(n_tiles):
        rows = slice(t * tile, (t + 1) * tile)
        lo = f_ref[rows, :].astype(F32)
        src = (2 * n_tiles - 1 - t) * tile
        mir = jnp.dot(rev_ref[...], f_ref[src:src + tile, :], preferred_element_type=F32)
        if t > 0:
            first = lax.broadcasted_iota(jnp.int32, (tile, 1), 0) == 0
            mir = jnp.where(first, f_ref[src + tile:src + tile + 1, :].astype(F32), mir)
        x_even = (lo + mir).astype(BF16)
        x_odd = (lo - mir).astype(BF16)
        for g in range(F_GROUPS):
            sl = slice(g * gd, (g + 1) * gd)
            zc_ref[rows, sl] = jnp.dot(x_even[:, sl], cc_ref[...], preferred_element_type=F32).astype(BF16)
            zs_ref[rows, sl] = jnp.dot(x_odd[:, sl], sc_ref[...], preferred_element_type=F32).astype(BF16)
    sign = 1.0 - 2.0 * (lax.broadcasted_iota(jnp.int32, (V7X_F32_SUBLANES, 1), 0) & 1).astype(F32)
    x_half = f_ref[half:half + V7X_BF16_SUBLANES, :]
    for g in range(F_GROUPS):
        sl = slice(g * gd, (g + 1) * gd)
        z_half = jnp.dot(x_half[:, sl], cc_ref[...], preferred_element_type=F32)[0:1, :]
        edge_ref[:, sl] = sign * (z_half * (1.0 / math.sqrt(n)))


def _fourier_kernel(f_ref, cc_ref, sc_ref, rev_ref, wc_ref, ws_ref, o_ref, zc_ref, zs_ref, edge_ref):
    fd = f_ref.shape[1]

    @pl.when(pl.program_id(1) == 0)
    def _():
        _fourier_channel_stage(f_ref, cc_ref, sc_ref, rev_ref, zc_ref, zs_ref, edge_ref)

    y = (jnp.dot(wc_ref[...], zc_ref[...], preferred_element_type=F32)
         + jnp.dot(ws_ref[...], zs_ref[...], preferred_element_type=F32))
    tm = y.shape[0]
    y = y.reshape(tm // V7X_F32_SUBLANES, V7X_F32_SUBLANES, fd) + edge_ref[...][None]
    o_ref[...] = y.reshape(tm, fd).astype(o_ref.dtype)


def _fourier_fold_kernel(f_ref, cc_ref, sc_ref, rev_ref, wc_ref, ws_ref, wh_ref, o_ref, zc_ref, zs_ref,
                         edge_ref, next_ref):
    fd = f_ref.shape[1]
    t_rows = wc_ref.shape[0]
    sub = rev_ref.shape[0]
    n_sub = t_rows // sub

    @pl.when(pl.program_id(1) == 0)
    def _():
        _fourier_channel_stage(f_ref, cc_ref, sc_ref, rev_ref, zc_ref, zs_ref, edge_ref)
        y_half = jnp.dot(wh_ref[...], zc_ref[...], preferred_element_type=F32)[0:1, :] + edge_ref[0:1, :]
        next_ref[...] = jnp.broadcast_to(y_half, next_ref.shape)

    e = jnp.dot(wc_ref[...], zc_ref[...], preferred_element_type=F32)
    e = (e.reshape(t_rows // V7X_F32_SUBLANES, V7X_F32_SUBLANES, fd) + edge_ref[...][None]).reshape(t_rows, fd)
    o = -jnp.dot(ws_ref[...], zs_ref[...], preferred_element_type=F32)
    o_ref[0] = (e - o).astype(o_ref.dtype)
    upper = e + o
    upper_bf = upper.astype(BF16)
    first = lax.broadcasted_iota(jnp.int32, (sub, 1), 0) == 0
    for j in range(n_sub):
        src = n_sub - 1 - j
        blk = jnp.dot(rev_ref[...], upper_bf[src * sub:(src + 1) * sub, :], preferred_element_type=F32)
        row0 = next_ref[0:1, :] if j == 0 else upper[(n_sub - j) * sub:(n_sub - j) * sub + 1, :]
        o_ref[1, j * sub:(j + 1) * sub, :] = jnp.where(first, row0, blk).astype(o_ref.dtype)
    next_ref[...] = jnp.broadcast_to(upper[0:1, :], next_ref.shape)


def _fourier(proj, n_seq, seq):
    attn = N_HEADS * HEAD_DIM
    fd = attn
    gd = fd // F_GROUPS
    cc, sc = _dft_tables(gd)
    cp, sp = _dft_tables(seq)
    half = seq // 2
    tm, tile = min(TM_FOURIER, seq), min(TM_FOURIER, half)
    assert seq % tm == 0 and half % tile == 0 and tile % V7X_BF16_SUBLANES == 0 and proj.shape[1] // fd > 3
    consts = (jnp.asarray(cc), jnp.asarray(sc), jnp.asarray(_mirror_matrix(tile)),
              jnp.asarray(cp[:, :half]), jnp.asarray(-sp[:, :half]))
    const_specs = [pl.BlockSpec((gd, gd), lambda b, i: (0, 0)),
                   pl.BlockSpec((gd, gd), lambda b, i: (0, 0)),
                   pl.BlockSpec((tile, tile), lambda b, i: (0, 0))]
    row_tile = pltpu.VMEM((V7X_F32_SUBLANES, fd), F32)
    scratch = [pltpu.VMEM((half, fd), BF16), pltpu.VMEM((half, fd), BF16), row_tile]
    if half % TM_MERGE == 0:
        t_rows, n_tiles = TM_MERGE, half // TM_MERGE
        w_half = np.zeros((V7X_BF16_SUBLANES, half), np.float32)
        w_half[0] = cp[half, :half].astype(np.float32)
        rows = lambda b, i: (n_tiles - 1 - i, 0)
        return pl.pallas_call(
            _fourier_fold_kernel,
            grid=(n_seq, n_tiles),
            in_specs=[pl.BlockSpec((seq, fd), lambda b, i: (b, 3))] + const_specs
            + [pl.BlockSpec((t_rows, half), rows), pl.BlockSpec((t_rows, half), rows),
               pl.BlockSpec((V7X_BF16_SUBLANES, half), lambda b, i: (0, 0))],
            out_specs=pl.BlockSpec((None, None, 2, t_rows, fd), lambda b, i: (b, n_tiles - 1 - i, 0, 0, 0)),
            out_shape=jax.ShapeDtypeStruct((n_seq, n_tiles, 2, t_rows, fd), BF16),
            scratch_shapes=scratch + [row_tile],
            compiler_params=_cparams(2),
            name="fourier_fold_%d" % seq,
        )(proj, *consts, jnp.asarray(w_half.astype(BF16)))
    return pl.pallas_call(
        _fourier_kernel,
        grid=(n_seq, seq // tm),
        in_specs=[pl.BlockSpec((seq, fd), lambda b, i: (b, 3))] + const_specs
        + [pl.BlockSpec((tm, half), lambda b, i: (i, 0)), pl.BlockSpec((tm, half), lambda b, i: (i, 0))],
        out_specs=pl.BlockSpec((tm, fd), lambda b, i: (b * (seq // tm) + i, 0)),
        out_shape=jax.ShapeDtypeStruct((n_seq * seq, fd), BF16),
        scratch_shapes=scratch,
        compiler_params=_cparams(2),
        name="fourier_%d" % seq,
    )(proj, *consts)


def _merge_out_kernel(l_ref, x_ref, a_ref, fo_ref, gates_ref, mod_ref, wpa_ref, wpb_ref, wout_ref,
                      o_ref, y_ref, *, tc):
    ga_ref = gates_ref.at[:, 0:x_ref.shape[1]]
    gb_ref = gates_ref.at[:, x_ref.shape[1]:]
    d = x_ref.shape[1]
    for c in range(d // tc):
        sl = slice(c * tc, (c + 1) * tc)
        ya = jnp.dot(a_ref[...], wpa_ref[:, sl], preferred_element_type=F32)
        yb = jnp.dot(fo_ref[...], wpb_ref[:, sl], preferred_element_type=F32)
        y = (jax.nn.sigmoid(ga_ref[:, sl].astype(F32)) * ya
             + jax.nn.sigmoid(gb_ref[:, sl].astype(F32)) * yb)
        y_ref[:, sl] = y.astype(BF16)
    for c in range(d // tc):
        sl = slice(c * tc, (c + 1) * tc)
        o = jnp.dot(y_ref[...], wout_ref[:, sl], preferred_element_type=F32)
        o_ref[:, sl] = x_ref[:, sl] + mod_ref[2:3, sl] * o


def _merge_out(layer, x, a, fo, proj, mods, w_pa, w_pb, w_out, *, rows_per_mod, mod_row0):
    m, d = x.shape
    attn, fd = a.shape[1], fo.shape[-1]
    tm = min(TM_MERGE, m)
    assert m % tm == 0 and (3 * attn + fd) % (2 * d) == 0 and rows_per_mod % tm == 0
    gate_blk = (3 * attn + fd) // (2 * d)
    whole = lambda arr: pl.BlockSpec((None,) + arr.shape[1:], lambda i, l: (l[0], 0, 0),
                                     pipeline_mode=pl.Buffered(1))
    if fo.ndim == 2:
        fo_spec = pl.BlockSpec((tm, fd), lambda i, l: (i, 0))
    else:
        n_tiles = fo.shape[1]
        assert fo.shape[2:4] == (2, tm) and fo.shape[0] * 2 * n_tiles * tm == m

        def folded(i, l):
            j = i % (2 * n_tiles)
            upper = j >= n_tiles
            return (i // (2 * n_tiles), jnp.where(upper, 2 * n_tiles - 1 - j, j), upper.astype(jnp.int32), 0, 0)

        fo_spec = pl.BlockSpec((None, None, None, tm, fd), folded)
    return pl.pallas_call(
        functools.partial(_merge_out_kernel, tc=min(TC_MERGE, d)),
        grid_spec=pltpu.PrefetchScalarGridSpec(
            num_scalar_prefetch=1, grid=(m // tm,),
            in_specs=[pl.BlockSpec((tm, d), lambda i, l: (i, 0)),
                      pl.BlockSpec((tm, attn), lambda i, l: (i, 0)),
                      fo_spec,
                      pl.BlockSpec((tm, 2 * d), lambda i, l: (i, gate_blk)),
                      pl.BlockSpec((None, None, N_MOD, d),
                                   lambda i, l: (l[0], mod_row0 + (i * tm) // rows_per_mod, 0, 0)),
                      whole(w_pa), whole(w_pb), whole(w_out)],
            out_specs=pl.BlockSpec((tm, d), lambda i, l: (i, 0)),
            scratch_shapes=[pltpu.VMEM((tm, d), BF16)]),
        out_shape=jax.ShapeDtypeStruct((m, d), F32),
        compiler_params=_cparams(1),
        name="merge_out",
    )(_layer_arg(layer), x, a, fo, proj, mods, w_pa, w_pb, w_out)


def _ffn_kernel(l_ref, x_ref, xp_ref, xn_ref, mod_ref, gains_ref, wa_ref, wg_ref, cwa_ref, cwg_ref, wd_ref,
                o_ref, h_ref, u_ref, *maybe_acc, seq, tc, final_norm):
    tm = x_ref.shape[0]
    halo = xp_ref.shape[0]
    tf = wa_ref.shape[1]
    seg = min(seq, tm)
    n_seg, stride = tm // seg, seg + halo
    work_rows = h_ref.shape[0]
    pad = u_ref.shape[2] - work_rows
    last_halo = work_rows - halo
    acc_ref = maybe_acc[0] if maybe_acc else o_ref
    n_rows = acc_ref.shape[0]
    i, f = pl.program_id(0), pl.program_id(1)

    @pl.when(f == 0)
    def _():
        g, scale, shift = gains_ref[0:1, :], mod_ref[4:5, :], mod_ref[3:4, :]
        gs = g * (1.0 + scale)
        tok0 = i * tm
        for s in range(n_seg):
            _norm_mod_store(h_ref, s * stride, x_ref, g, scale, shift, src_row0=s * seg, n=seg)
            if s < n_seg - 1:
                h_ref[s * stride + seg:(s + 1) * stride, :] = jnp.zeros((halo, h_ref.shape[1]), BF16)
        after = jnp.where(((tok0 + tm) & (seq - 1)) != 0, _norm_mod_rows(xn_ref[...], gs, shift)[0:1, :], 0.0)
        before = jnp.where((tok0 & (seq - 1)) != 0, _norm_mod_rows(xp_ref[...], gs, shift)[halo - 1:halo, :], 0.0)
        r = lax.broadcasted_iota(jnp.int32, (halo, 1), 0)
        h_ref[last_halo:, :] = jnp.where(r == 0, after, jnp.where(r == 1, before, 0.0)).astype(BF16)
        acc_ref[...] = jnp.zeros_like(acc_ref)

    h = h_ref[...]

    def conv(c, idx, cw_ref, cs):
        return (u_ref[c, idx, pl.ds(pad - 1, n_rows), :] * cw_ref[0:1, cs]
                + u_ref[c, idx, pl.ds(pad, n_rows), :] * cw_ref[1:2, cs]
                + u_ref[c, idx, pl.ds(pad + 1, n_rows), :] * cw_ref[2:3, cs] + cw_ref[3:4, cs])

    def up_project(c, idx, w_ref, cs):
        res = jnp.dot(h, w_ref[:, cs], preferred_element_type=F32)
        u_ref[c, idx, pl.ds(pad, work_rows), :] = res
        u_ref[c, idx, pad - 1:pad, :] = res[last_halo + 1:last_halo + 2, :]

    n_chunks = tf // tc
    for c in range(n_chunks):
        cs = slice(c * tc, (c + 1) * tc)
        up_project(c, 0, wa_ref, cs)
        up_project(c, 1, wg_ref, cs)
    for c in range(n_chunks):
        cs = slice(c * tc, (c + 1) * tc)
        ca = conv(c, 0, cwa_ref, cs)
        cg = conv(c, 1, cwg_ref, cs)
        act = ((cg * jax.nn.sigmoid(cg)) * ca).astype(BF16)
        acc_ref[...] += jnp.dot(act, wd_ref[cs, :], preferred_element_type=F32)

    @pl.when(f == pl.num_programs(1) - 1)
    def _():
        for s in range(n_seg):
            rows = slice(s * seg, (s + 1) * seg)
            y = x_ref[rows, :] + mod_ref[5:6, :] * acc_ref[s * stride:s * stride + seg, :]
            if final_norm:
                y = (y * lax.rsqrt(jnp.mean(y * y, axis=-1, keepdims=True) + RMS_EPS)) * gains_ref[1:2, :]
            o_ref[rows, :] = y


def _conv_ffn(layer, x, mods, gains, w_up, conv_wb, w_down, *, seq, rows_per_mod, mod_row0, final_norm):
    m, d = x.shape
    ffn = w_down.shape[1]
    tm, tf = min(TM_FFN, m), min(TF_FFN, ffn)
    tc = min(TC_FFN, tf)
    halo = V7X_BF16_SUBLANES
    assert seq & (seq - 1) == 0 and (seq % tm == 0 or tm % seq == 0) and m % seq == 0
    assert m % tm == 0 and ffn % tf == 0 and tf % tc == 0 and min(seq, tm) % halo == 0 and rows_per_mod % tm == 0
    nf, nhb = ffn // tf, m // halo
    n_seg = tm // min(seq, tm)
    work_rows = tm + n_seg * halo
    acc = [pltpu.VMEM((work_rows - halo, d), F32)] if n_seg > 1 else []
    mat = lambda rows, cols, imap: pl.BlockSpec((None, rows, cols), imap)
    return pl.pallas_call(
        functools.partial(_ffn_kernel, seq=seq, tc=tc, final_norm=final_norm),
        grid_spec=pltpu.PrefetchScalarGridSpec(
            num_scalar_prefetch=1, grid=(m // tm, nf),
            in_specs=[pl.BlockSpec((tm, d), lambda i, f, l: (i, 0)),
                      pl.BlockSpec((halo, d), lambda i, f, l: (jnp.maximum(i * (tm // halo) - 1, 0), 0)),
                      pl.BlockSpec((halo, d),
                                   lambda i, f, l: (jnp.minimum((i + 1) * (tm // halo), nhb - 1), 0)),
                      pl.BlockSpec((None, None, N_MOD, d),
                                   lambda i, f, l: (l[0], mod_row0 + (i * tm) // rows_per_mod, 0, 0)),
                      mat(2, d, lambda i, f, l: (l[0], 0, 0)),
                      mat(d, tf, lambda i, f, l: (l[0], 0, f)),
                      mat(d, tf, lambda i, f, l: (l[0], 0, nf + f)),
                      mat(4, tf, lambda i, f, l: (l[0], 0, f)),
                      mat(4, tf, lambda i, f, l: (l[0], 0, nf + f)),
                      mat(tf, d, lambda i, f, l: (l[0], f, 0))],
            out_specs=pl.BlockSpec((tm, d), lambda i, f, l: (i, 0)),
            scratch_shapes=[pltpu.VMEM((work_rows, d), BF16),
                            pltpu.VMEM((tf // tc, 2, V7X_F32_SUBLANES + work_rows, tc), F32)] + acc),
        out_shape=jax.ShapeDtypeStruct((m, d), F32),
        compiler_params=_cparams(2),
        name="conv_ffn_final" if final_norm else "conv_ffn",
    )(_layer_arg(layer), x, x, x, mods, gains, w_up, w_up, conv_wb, conv_wb, w_down)


def kernel(x_prompt, x_sample, cache_k, cache_v, c, c_ctx, w_mod, b_mod, norm1_g, w_in, rpb, w_pa, w_pb,
           w_out, norm2_g, w_up, conv_w, conv_b, w_down, norm_f_g):
    n_ctx, seq, d = x_prompt.shape
    n_lat, n_tok, _ = x_sample.shape
    n_layers = w_mod.shape[0]
    assert n_lat < MOD_ROWS and n_layers >= 1

    c_all = jnp.zeros((MOD_ROWS, d), F32).at[:n_lat].set(c).at[n_lat].set(c_ctx)
    mods = _modulation(c_all, w_mod, b_mod).reshape(n_layers, MOD_ROWS, N_MOD, d)
    w_in, w_pa, w_pb, w_out, w_up, w_down = (w.astype(BF16) for w in (w_in, w_pa, w_pb, w_out, w_up, w_down))

    ffn_gains = jnp.stack([norm2_g, jnp.broadcast_to(norm_f_g, norm2_g.shape)], axis=1)
    conv_wb = jnp.concatenate([conv_w, conv_b[:, None, :]], axis=1)

    xp = x_prompt.reshape(n_ctx * seq, d)
    xs = x_sample.reshape(n_lat * n_tok, d)
    ctx_mod = dict(rows_per_mod=n_ctx * seq, mod_row0=n_lat)
    lat_mod = dict(rows_per_mod=n_tok, mod_row0=0)
    caches = (jnp.zeros((n_ctx, n_layers, N_HEADS, seq, HEAD_DIM), F32),) * 2
    for l in range(n_layers):
        last = l == n_layers - 1
        ffn_w = (mods, ffn_gains, w_up, conv_wb, w_down)
        proj, *caches = _in_proj(l, xp, mods, norm1_g, w_in, caches=caches, **ctx_mod)
        a = _ctx_attention(proj, n_ctx, seq)
        fo = _fourier(proj, n_ctx, seq)
        xp = _merge_out(l, xp, a, fo, proj, mods, w_pa, w_pb, w_out, **ctx_mod)
        xp = _conv_ffn(l, xp, *ffn_w, seq=seq, final_norm=last, **ctx_mod)

        (proj,) = _in_proj(l, xs, mods, norm1_g, w_in, **lat_mod)
        a = _nbr_attention(l, proj, cache_k, cache_v, rpb, n_lat, n_tok)
        fo = _fourier(proj, n_lat, n_tok)
        xs = _merge_out(l, xs, a, fo, proj, mods, w_pa, w_pb, w_out, **lat_mod)
        xs = _conv_ffn(l, xs, *ffn_w, seq=n_tok, final_norm=last, **lat_mod)

    return (xp.reshape(n_ctx, seq, d), xs.reshape(n_lat, n_tok, d), caches[0], caches[1])
```
